```python
import math, functools
import jax, jax.numpy as jnp
from jax import lax
import numpy as np

D_MODEL = 2048
BATCH = 4
SEQ = 2048
DEPTH = 1
DEC_BATCH = 128
DEC_SEQ = 4
PAST_LEN = 16384
PAGE_SIZE = 128

RMS_EPS = 1e-5
SSM_GROUP = 16
SSM_GROUPS = D_MODEL // 32
SSM_STATE = 64
D_SSM = SSM_GROUPS * SSM_GROUP
DT_MIN = 1e-3
DT_MAX = 1e-1
HEAD_DIM = 64
N_HEADS = D_MODEL // 128
N_KV_HEADS = N_HEADS // 4
Q_PER_KV = N_HEADS // N_KV_HEADS
D_ATTN = N_HEADS * HEAD_DIM
D_KV = N_KV_HEADS * HEAD_DIM
WINDOW = 128
ROPE_THETA = 10000.0
NEG_INF = -1e30
N_IN = D_SSM + D_ATTN + 2 * D_KV + 2 * D_MODEL
SPLITS = (D_SSM, D_SSM + D_ATTN, D_SSM + D_ATTN + D_KV, D_SSM + D_ATTN + 2 * D_KV,
          D_SSM + D_ATTN + 2 * D_KV + D_MODEL)
N_EXPERTS = 32
TOP_K = 4
D_FF = D_MODEL
SWIGLU_ALPHA = 1.702
SWIGLU_LIMIT = 7.0
MOE_BLOCK = 128

kernel_name = 'hybrid_s5_swa_sink_moe_step'


def rms_norm(x, g):
    x32 = x.astype(jnp.float32)
    y = x32 * lax.rsqrt(jnp.mean(x32 * x32, axis=-1, keepdims=True) + RMS_EPS)
    return (y * g.astype(jnp.float32)).astype(x.dtype)


def rope(x, pos):
    half = HEAD_DIM // 2
    inv = ROPE_THETA ** (-jnp.arange(half, dtype=jnp.float32) / half)
    ang = pos.astype(jnp.float32)[:, None] * inv[None, :]
    cos = jnp.cos(ang)[:, None, :]
    sin = jnp.sin(ang)[:, None, :]
    x1 = x[..., :half].astype(jnp.float32)
    x2 = x[..., half:].astype(jnp.float32)
    return jnp.concatenate([x1 * cos - x2 * sin, x2 * cos + x1 * sin], axis=-1).astype(x.dtype)


def s5_discretize(a_re, a_im, log_dt, b_re, b_im):
    a_re = a_re.astype(jnp.float32)
    a_im = a_im.astype(jnp.float32)
    dt = jnp.exp(log_dt.astype(jnp.float32))[:, None]
    mag = jnp.exp(a_re * dt)
    lb_re = mag * jnp.cos(a_im * dt)
    lb_im = mag * jnp.sin(a_im * dt)
    inv = 1.0 / (a_re * a_re + a_im * a_im)
    f_re = ((lb_re - 1.0) * a_re + lb_im * a_im) * inv
    f_im = (lb_im * a_re - (lb_re - 1.0) * a_im) * inv
    br = b_re.astype(jnp.float32)
    bi = b_im.astype(jnp.float32)
    bb_re = f_re[..., None] * br - f_im[..., None] * bi
    bb_im = f_re[..., None] * bi + f_im[..., None] * br
    return lb_re, lb_im, bb_re, bb_im


def _complex_affine_combine(left, right):
    ar1, ai1, br1, bi1 = left
    ar2, ai2, br2, bi2 = right
    return (ar2 * ar1 - ai2 * ai1,
            ar2 * ai1 + ai2 * ar1,
            ar2 * br1 - ai2 * bi1 + br2,
            ar2 * bi1 + ai2 * br1 + bi2)


def s5_scan(u, h0_re, h0_im, lb_re, lb_im, bb_re, bb_im, c_re, c_im, d_skip):
    u32 = u.astype(jnp.float32)
    t = u.shape[1]
    bu_re = jnp.einsum('ntgh,gph->ntgp', u32, bb_re)
    bu_im = jnp.einsum('ntgh,gph->ntgp', u32, bb_im)
    h0r = h0_re.astype(jnp.float32)
    h0i = h0_im.astype(jnp.float32)
    bu_re = bu_re.at[:, 0].add(lb_re * h0r - lb_im * h0i)
    bu_im = bu_im.at[:, 0].add(lb_re * h0i + lb_im * h0r)
    ar = jnp.broadcast_to(lb_re, (1, t) + lb_re.shape)
    ai = jnp.broadcast_to(lb_im, (1, t) + lb_im.shape)
    _, _, s_re, s_im = lax.associative_scan(_complex_affine_combine, (ar, ai, bu_re, bu_im), axis=1)
    y = (jnp.einsum('ntgp,ghp->ntgh', s_re, c_re.astype(jnp.float32))
         - jnp.einsum('ntgp,ghp->ntgh', s_im, c_im.astype(jnp.float32))
         + d_skip.astype(jnp.float32) * u32)
    return y.astype(u.dtype), s_re[:, -1], s_im[:, -1]


def _sink_attend(q, k, v, valid, sinks):
    s = jnp.einsum('...qkgd,...skd->...kgqs', q, k).astype(jnp.float32) * (HEAD_DIM ** -0.5)
    s = jnp.where(valid, s, NEG_INF)
    sink = sinks.astype(jnp.float32).reshape(N_KV_HEADS, Q_PER_KV, 1, 1)
    m = jnp.maximum(jnp.max(s, axis=-1, keepdims=True), sink)
    p = jnp.exp(s - m)
    p = p / (jnp.sum(p, axis=-1, keepdims=True) + jnp.exp(sink - m))
    return jnp.einsum('...kgqs,...skd->...qkgd', p.astype(v.dtype), v)


def attend_prompt(q, k, v, sinks):
    n, t = q.shape[:2]
    nb = t // WINDOW
    qb = q.reshape(n, nb, WINDOW, N_KV_HEADS, Q_PER_KV, HEAD_DIM)
    kb = k.reshape(n, nb, WINDOW, N_KV_HEADS, HEAD_DIM)
    vb = v.reshape(n, nb, WINDOW, N_KV_HEADS, HEAD_DIM)
    pad = ((0, 0), (1, 0), (0, 0), (0, 0), (0, 0))
    k_ctx = jnp.concatenate([jnp.pad(kb, pad)[:, :-1], kb], axis=2)
    v_ctx = jnp.concatenate([jnp.pad(vb, pad)[:, :-1], vb], axis=2)
    qi = jnp.arange(WINDOW)[:, None]
    kj = jnp.arange(2 * WINDOW)[None, :]
    blk = jnp.arange(nb)[:, None, None] * WINDOW
    k_pos = blk - WINDOW + kj
    rel = (blk + qi) - k_pos
    valid = (rel >= 0) & (rel < WINDOW) & (k_pos >= 0)
    o = _sink_attend(qb, k_ctx, v_ctx, valid[:, None, None], sinks)
    return o.reshape(n, t, N_HEADS, HEAD_DIM), k[:, -WINDOW:], v[:, -WINDOW:]


def attend_sample(q, k, v, sinks, k_past, v_past):
    n, t = q.shape[:2]
    w = k_past.shape[1]
    qg = q.reshape(n, t, N_KV_HEADS, Q_PER_KV, HEAD_DIM)
    k_ctx = jnp.concatenate([k_past.astype(k.dtype), k], axis=1)
    v_ctx = jnp.concatenate([v_past.astype(v.dtype), v], axis=1)
    rel = jnp.arange(t)[:, None] + w - jnp.arange(w + t)[None, :]
    valid = (rel >= 0) & (rel < WINDOW)
    o = _sink_attend(qg, k_ctx, v_ctx, valid, sinks)
    return o.reshape(n, t, N_HEADS, HEAD_DIM), k_ctx[:, -WINDOW:], v_ctx[:, -WINDOW:]


def clamped_swiglu(hid):
    x_glu, x_lin = jnp.split(hid, 2, axis=-1)
    x_glu = jnp.minimum(x_glu, SWIGLU_LIMIT)
    x_lin = jnp.clip(x_lin, -SWIGLU_LIMIT, SWIGLU_LIMIT)
    return x_glu * jax.nn.sigmoid(SWIGLU_ALPHA * x_glu) * (x_lin + 1.0)


def moe_ffn(h, w_router, b_router, w_up, b_up, w_down, b_down):
    n = h.shape[0]
    a = n * TOP_K
    logits = jnp.einsum('nd,de->ne', h.astype(jnp.float32), w_router.astype(jnp.float32)) + b_router.astype(jnp.float32)
    top_v, top_i = lax.top_k(logits, TOP_K)
    gate = jax.nn.softmax(top_v, axis=-1)
    e_flat = top_i.reshape(a)
    order = jnp.argsort(e_flat)
    e_sorted = e_flat[order]
    tok_sorted = (order // TOP_K).astype(jnp.int32)
    gate_sorted = gate.reshape(a)[order]
    counts = jnp.bincount(e_flat, length=N_EXPERTS)
    starts = jnp.cumsum(counts) - counts
    padded = (counts + MOE_BLOCK - 1) // MOE_BLOCK * MOE_BLOCK
    pad_end = jnp.cumsum(padded)
    dest = (pad_end - padded)[e_sorted] + jnp.arange(a) - starts[e_sorted]
    n_blocks = -(-a // MOE_BLOCK) + N_EXPERTS
    n_slots = n_blocks * MOE_BLOCK
    slot_tok = jnp.zeros((n_slots,), jnp.int32).at[dest].set(tok_sorted)
    slot_gate = jnp.zeros((n_slots,), jnp.float32).at[dest].set(gate_sorted)
    block_e = jnp.minimum(jnp.searchsorted(pad_end, jnp.arange(n_blocks) * MOE_BLOCK, side='right'), N_EXPERTS - 1)

    def expert_block(args):
        tok, e = args
        xb = h[tok]
        hid = xb @ w_up[e] + b_up[e]
        return clamped_swiglu(hid) @ w_down[e] + b_down[e]

    ys = lax.map(expert_block, (slot_tok.reshape(n_blocks, MOE_BLOCK), block_e))
    ys = (ys.reshape(n_slots, -1).astype(jnp.float32) * slot_gate[:, None]).astype(h.dtype)
    return jnp.zeros_like(h).at[slot_tok].add(ys)


def decoder_layer(x, positions, h0_re, h0_im, attend, lw):
    n, t, _ = x.shape
    h = rms_norm(x, lw['attn_norm_g'])
    proj = h @ lw['w_in'] + lw['b_in']
    u, q, k, v, g_ssm, g_attn = jnp.split(proj, list(SPLITS), axis=-1)
    lb_re, lb_im, bb_re, bb_im = s5_discretize(lw['ssm_a_re'], lw['ssm_a_im'], lw['ssm_log_dt'], lw['ssm_b_re'], lw['ssm_b_im'])
    y_ssm, fin_re, fin_im = s5_scan(u.reshape(n, t, SSM_GROUPS, SSM_GROUP), h0_re, h0_im,
                                    lb_re, lb_im, bb_re, bb_im, lw['ssm_c_re'], lw['ssm_c_im'], lw['ssm_d'])
    y_ssm = jax.nn.gelu(y_ssm.reshape(n, t, D_SSM))
    y_ssm = y_ssm * jax.nn.sigmoid(y_ssm @ lw['w_glu'] + lw['b_glu'])
    q = rope(q.reshape(n, t, N_HEADS, HEAD_DIM), positions)
    k = rope(k.reshape(n, t, N_KV_HEADS, HEAD_DIM), positions)
    v = v.reshape(n, t, N_KV_HEADS, HEAD_DIM)
    o, k_keep, v_keep = attend(q, k, v, lw['attn_sinks'])
    merged = (jax.nn.sigmoid(g_ssm) * (y_ssm @ lw['w_branch_ssm'])
              + jax.nn.sigmoid(g_attn) * (o.reshape(n, t, D_ATTN) @ lw['w_branch_attn']))
    x = x + merged @ lw['w_out']
    h2 = rms_norm(x, lw['ffn_norm_g'])
    x = x + moe_ffn(h2.reshape(n * t, D_MODEL), lw['w_router'], lw['b_router'], lw['w_up'], lw['b_up'],
                    lw['w_down'], lw['b_down']).reshape(n, t, D_MODEL)
    return x, fin_re, fin_im, k_keep, v_keep


def setup_inputs(seed: int = 0) -> dict:
    key = jax.random.key(seed)
    ks = jax.random.split(key, 32)
    f32 = jnp.float32

    def nrm(k, shape, scale):
        return jax.random.normal(k, shape, f32) * scale

    L, G, P, H = DEPTH, SSM_GROUPS, SSM_STATE, SSM_GROUP
    return {
        'x_prompt': nrm(ks[0], (BATCH, SEQ, D_MODEL), 1.0),
        'x_sample': nrm(ks[1], (DEC_BATCH, DEC_SEQ, D_MODEL), 1.0),
        'state_ssm_re': nrm(ks[2], (L, DEC_BATCH, G, P), 0.3),
        'state_ssm_im': nrm(ks[3], (L, DEC_BATCH, G, P), 0.3),
        'cache_k': nrm(ks[4], (L, DEC_BATCH, WINDOW, N_KV_HEADS, HEAD_DIM), 1.0),
        'cache_v': nrm(ks[5], (L, DEC_BATCH, WINDOW, N_KV_HEADS, HEAD_DIM), 1.0),
        'attn_norm_g': 1.0 + nrm(ks[6], (L, D_MODEL), 0.01),
        'w_in': nrm(ks[7], (L, D_MODEL, N_IN), D_MODEL ** -0.5),
        'b_in': nrm(ks[8], (L, N_IN), 0.01),
        'ssm_a_re': -0.5 + nrm(ks[9], (L, G, P), 0.01),
        'ssm_a_im': jnp.pi * jnp.arange(P, dtype=f32) + nrm(ks[10], (L, G, P), 0.01),
        'ssm_log_dt': jax.random.uniform(ks[11], (L, G), f32, math.log(DT_MIN), math.log(DT_MAX)),
        'ssm_b_re': nrm(ks[12], (L, G, P, H), (2.0 * H) ** -0.5),
        'ssm_b_im': nrm(ks[13], (L, G, P, H), (2.0 * H) ** -0.5),
        'ssm_c_re': nrm(ks[14], (L, G, H, P), (2.0 * P) ** -0.5 * 4.0),
        'ssm_c_im': nrm(ks[15], (L, G, H, P), (2.0 * P) ** -0.5 * 4.0),
        'ssm_d': nrm(ks[16], (L, G, H), 1.0),
        'w_glu': nrm(ks[17], (L, D_SSM, D_SSM), D_SSM ** -0.5),
        'b_glu': nrm(ks[18], (L, D_SSM), 0.01),
        'attn_sinks': nrm(ks[19], (L, N_HEADS), 0.5),
        'w_branch_ssm': nrm(ks[20], (L, D_SSM, D_MODEL), D_SSM ** -0.5),
        'w_branch_attn': nrm(ks[21], (L, D_ATTN, D_MODEL), D_ATTN ** -0.5),
        'w_out': nrm(ks[22], (L, D_MODEL, D_MODEL), D_MODEL ** -0.5),
        'ffn_norm_g': 1.0 + nrm(ks[23], (L, D_MODEL), 0.01),
        'w_router': nrm(ks[24], (L, D_MODEL, N_EXPERTS), D_MODEL ** -0.5),
        'b_router': nrm(ks[25], (L, N_EXPERTS), 0.01),
        'w_up': nrm(ks[26], (L, N_EXPERTS, D_MODEL, 2 * D_FF), D_MODEL ** -0.5),
        'b_up': nrm(ks[27], (L, N_EXPERTS, 2 * D_FF), 0.01),
        'w_down': nrm(ks[28], (L, N_EXPERTS, D_FF, D_MODEL), D_FF ** -0.5),
        'b_down': nrm(ks[29], (L, N_EXPERTS, D_MODEL), 0.01),
        'final_norm_g': 1.0 + nrm(ks[30], (D_MODEL,), 0.01),
    }


def reference(x_prompt, x_sample, state_ssm_re, state_ssm_im, cache_k, cache_v,
              attn_norm_g, w_in, b_in, ssm_a_re, ssm_a_im, ssm_log_dt, ssm_b_re, ssm_b_im,
              ssm_c_re, ssm_c_im, ssm_d, w_glu, b_glu, attn_sinks, w_branch_ssm, w_branch_attn,
              w_out, ffn_norm_g, w_router, b_router, w_up, b_up, w_down, b_down, final_norm_g):
    pos_p = jnp.arange(x_prompt.shape[1])
    pos_s = PAST_LEN + jnp.arange(x_sample.shape[1])
    zero_state = jnp.zeros((x_prompt.shape[0], SSM_GROUPS, SSM_STATE), jnp.float32)
    h_p, h_s = x_prompt, x_sample
    new_p, new_s = [], []
    for l in range(DEPTH):
        lw = dict(attn_norm_g=attn_norm_g[l], w_in=w_in[l], b_in=b_in[l],
                  ssm_a_re=ssm_a_re[l], ssm_a_im=ssm_a_im[l], ssm_log_dt=ssm_log_dt[l],
                  ssm_b_re=ssm_b_re[l], ssm_b_im=ssm_b_im[l], ssm_c_re=ssm_c_re[l], ssm_c_im=ssm_c_im[l],
                  ssm_d=ssm_d[l], w_glu=w_glu[l], b_glu=b_glu[l], attn_sinks=attn_sinks[l],
                  w_branch_ssm=w_branch_ssm[l], w_branch_attn=w_branch_attn[l], w_out=w_out[l],
                  ffn_norm_g=ffn_norm_g[l], w_router=w_router[l], b_router=b_router[l],
                  w_up=w_up[l], b_up=b_up[l], w_down=w_down[l], b_down=b_down[l])
        h_p, sp_re, sp_im, kp, vp = decoder_layer(h_p, pos_p, zero_state, zero_state, attend_prompt, lw)
        attend_s = functools.partial(attend_sample, k_past=cache_k[l], v_past=cache_v[l])
        h_s, ss_re, ss_im, ksn, vsn = decoder_layer(h_s, pos_s, state_ssm_re[l], state_ssm_im[l], attend_s, lw)
        new_p.append((sp_re, sp_im, kp, vp))
        new_s.append((ss_re, ss_im, ksn, vsn))
    y_prompt = rms_norm(h_p, final_norm_g)
    y_sample = rms_norm(h_s, final_norm_g)
    ssm_re_p = jnp.stack([s[0] for s in new_p])
    ssm_im_p = jnp.stack([s[1] for s in new_p])
    k_p = jnp.stack([s[2] for s in new_p])
    v_p = jnp.stack([s[3] for s in new_p])
    ssm_re_s = jnp.stack([s[0] for s in new_s])
    ssm_im_s = jnp.stack([s[1] for s in new_s])
    k_s = jnp.stack([s[2] for s in new_s])
    v_s = jnp.stack([s[3] for s in new_s])
    return (y_prompt, y_sample, ssm_re_p, ssm_im_p, k_p, v_p, ssm_re_s, ssm_im_s, k_s, v_s)
```

```python
import functools

import jax
import jax.numpy as jnp
from jax import lax
from jax.experimental import pallas as pl
from jax.experimental.pallas import tpu as pltpu

F32 = jnp.float32
BF16 = jnp.bfloat16
I32 = jnp.int32
HIGHEST = lax.Precision.HIGHEST

D_MODEL = 2048
RMS_EPS = 1e-5
SSM_GROUP = 16
SSM_GROUPS = 64
SSM_STATE = 64
D_SSM = 1024
HEAD_DIM = 64
N_HEADS = 16
N_KV_HEADS = 4
Q_PER_KV = 4
D_ATTN = 1024
D_KV = 256
WINDOW = 128
ROPE_THETA = 10000.0
NEG_INF = -1e30
PAST_LEN = 16384
N_IN = D_SSM + D_ATTN + 2 * D_KV + 2 * D_MODEL
N_EXPERTS = 32
TOP_K = 4
D_FF = 2048
SWIGLU_ALPHA = 1.702
SWIGLU_LIMIT = 7.0

COL_GS, COL_GA, COL_U, COL_Q, COL_K, COL_V = 0, 2048, 4096, 5120, 6144, 6400

VMEM_LIMIT = 56 * 1024 * 1024


def _sigmoid(x):
    return 1.0 / (1.0 + jnp.exp(-x))


def _cparams(sem, vmem=VMEM_LIMIT):
    return pltpu.CompilerParams(dimension_semantics=sem, vmem_limit_bytes=vmem)


TM1 = 512
TN1 = 512
_GATE_TILES = (COL_U // TN1)
_ROPE_TILE0 = COL_Q // TN1
_KV_TILE = COL_K // TN1


def _inproj_kernel(x_ref, g_ref, w_ref, b_ref, cos_ref, sin_ref, o_ref, h_scr):
    j = pl.program_id(1)

    @pl.when(j == 0)
    def _norm():
        x = x_ref[...]
        ms = jnp.mean(x * x, axis=-1, keepdims=True)
        h_scr[...] = (x * lax.rsqrt(ms + RMS_EPS) * g_ref[...]).astype(BF16)

    acc = jnp.dot(h_scr[...], w_ref[...], preferred_element_type=F32) + b_ref[...]

    @pl.when(j < _GATE_TILES)
    def _gate():
        o_ref[...] = _sigmoid(acc)

    @pl.when(jnp.logical_and(j >= _GATE_TILES, j < _ROPE_TILE0))
    def _plain():
        o_ref[...] = acc

    @pl.when(j >= _ROPE_TILE0)
    def _rope():
        c = jnp.concatenate([cos_ref[...]] * (TN1 // 128), axis=1)
        s = jnp.concatenate([sin_ref[...]] * (TN1 // 128), axis=1)
        lane = lax.broadcasted_iota(I32, acc.shape, 1)
        first_half = (lane & (HEAD_DIM // 2)) == 0
        partner = jnp.where(first_half, pltpu.roll(acc, TN1 - HEAD_DIM // 2, 1), pltpu.roll(acc, HEAD_DIM // 2, 1))
        roped = acc * c + partner * s
        n_rot = jnp.where(j == _KV_TILE, D_KV, TN1)
        o_ref[...] = jnp.where(lane < n_rot, roped, acc)


def _inproj(x, g, w_bf, b, cos_t, sin_t):
    m = x.shape[0]
    return pl.pallas_call(
        _inproj_kernel,
        grid=(m // TM1, N_IN // TN1),
        in_specs=[
            pl.BlockSpec((TM1, D_MODEL), lambda i, j: (i, 0)),
            pl.BlockSpec((1, D_MODEL), lambda i, j: (0, 0)),
            pl.BlockSpec((D_MODEL, TN1), lambda i, j: (0, j)),
            pl.BlockSpec((1, TN1), lambda i, j: (0, j)),
            pl.BlockSpec((TM1, 128), lambda i, j: (i, 0)),
            pl.BlockSpec((TM1, 128), lambda i, j: (i, 0)),
        ],
        out_specs=pl.BlockSpec((TM1, TN1), lambda i, j: (i, j)),
        out_shape=jax.ShapeDtypeStruct((m, N_IN), F32),
        scratch_shapes=[pltpu.VMEM((TM1, D_MODEL), BF16)],
        compiler_params=_cparams(("parallel", "arbitrary")),
        name="inproj",
    )(x, g, w_bf, b, cos_t, sin_t)


def _ssm_operators(a_re, a_im, log_dt, b_re, b_im, c_re, c_im, d_skip, chunk):
    a_re = a_re.astype(F32)
    a_im = a_im.astype(F32)
    dt = jnp.exp(log_dt.astype(F32))[:, None]
    mag = jnp.exp(a_re * dt)
    lb_re = mag * jnp.cos(a_im * dt)
    lb_im = mag * jnp.sin(a_im * dt)
    inv = 1.0 / (a_re * a_re + a_im * a_im)
    f_re = ((lb_re - 1.0) * a_re + lb_im * a_im) * inv
    f_im = (lb_im * a_re - (lb_re - 1.0) * a_im) * inv
    br = b_re.astype(F32)
    bi = b_im.astype(F32)
    bb_re = f_re[..., None] * br - f_im[..., None] * bi
    bb_im = f_re[..., None] * bi + f_im[..., None] * br
    k = jnp.arange(chunk + 1, dtype=F32)[:, None, None]
    pmag = jnp.exp(a_re * dt * k)
    pw_re = pmag * jnp.cos(a_im * dt * k)
    pw_im = pmag * jnp.sin(a_im * dt * k)
    g, p, h = bb_re.shape
    pb_re = pw_re[:chunk, :, :, None] * bb_re[None] - pw_im[:chunk, :, :, None] * bb_im[None]
    pb_im = pw_re[:chunk, :, :, None] * bb_im[None] + pw_im[:chunk, :, :, None] * bb_re[None]
    w_re = jnp.transpose(pb_re[::-1], (1, 0, 3, 2)).reshape(g, chunk * h, p)
    w_im = jnp.transpose(pb_im[::-1], (1, 0, 3, 2)).reshape(g, chunk * h, p)
    cr = c_re.astype(F32)
    ci = c_im.astype(F32)
    crt = jnp.transpose(cr, (0, 2, 1))[None]
    cit = jnp.transpose(ci, (0, 2, 1))[None]
    e_re = pw_re[1:, :, :, None]
    e_im = pw_im[1:, :, :, None]
    v_re = jnp.transpose(crt * e_re - cit * e_im, (1, 2, 0, 3)).reshape(g, p, chunk * h)
    v_im = jnp.transpose(-(crt * e_im + cit * e_re), (1, 2, 0, 3)).reshape(g, p, chunk * h)
    lagk = (jnp.einsum('gop,lgph->lgoh', cr, pb_re, precision=HIGHEST)
            - jnp.einsum('gop,lgph->lgoh', ci, pb_im, precision=HIGHEST))
    jj = jnp.arange(chunk)[:, None]
    tt = jnp.arange(chunk)[None, :]
    lag = tt - jj
    blocks = jnp.where((lag >= 0)[:, :, None, None, None], lagk[jnp.maximum(lag, 0)], 0.0)
    toep = jnp.transpose(blocks, (2, 0, 4, 1, 3)).reshape(g, chunk * h, chunk * h).astype(BF16)
    d_row = jnp.tile(d_skip.astype(F32), (1, chunk)).reshape(g, 1, chunk * h)
    return dict(toep=toep, w_re=w_re, w_im=w_im, v_re=v_re, v_im=v_im,
                lbl_re=pw_re[chunk][:, None, :], lbl_im=pw_im[chunk][:, None, :], d_row=d_row)


def _ssm_kernel(u_ref, t_ref, wr_ref, wi_ref, vr_ref, vi_ref, lr_ref, li_ref, d_ref, s0r_ref, s0i_ref,
                y_ref, sr_ref, si_ref, lor_scr, loi_scr, pr_scr, pi_scr, *, n_chunks, n_seq):
    u = u_ref[0]
    lor_scr[...] = jnp.dot(u, wr_ref[0], preferred_element_type=F32, precision=HIGHEST)
    loi_scr[...] = jnp.dot(u, wi_ref[0], preferred_element_type=F32, precision=HIGHEST)
    lr = lr_ref[0]
    li = li_ref[0]
    sr = s0r_ref[0]
    si = s0i_ref[0]
    for c in range(n_chunks):
        rows = slice(c * n_seq, (c + 1) * n_seq)
        pr_scr[rows, :] = sr
        pi_scr[rows, :] = si
        sr, si = (lr * sr - li * si + lor_scr[rows, :], lr * si + li * sr + loi_scr[rows, :])
    sr_ref[0] = sr
    si_ref[0] = si
    y = jnp.dot(u.astype(BF16), t_ref[0], preferred_element_type=F32)
    y += jnp.dot(pr_scr[...], vr_ref[0], preferred_element_type=F32, precision=HIGHEST)
    y += jnp.dot(pi_scr[...], vi_ref[0], preferred_element_type=F32, precision=HIGHEST)
    y_ref[0] = y + u * d_ref[0]


def _ssm(ug, ops, s0_re, s0_im, n_chunks, n_seq):
    g, rows, width = ug.shape
    p = SSM_STATE
    per_g = lambda *shape: pl.BlockSpec((1,) + shape, lambda i: (i, 0, 0))
    return pl.pallas_call(
        functools.partial(_ssm_kernel, n_chunks=n_chunks, n_seq=n_seq),
        grid=(g,),
        in_specs=[per_g(rows, width), per_g(width, width), per_g(width, p), per_g(width, p),
                  per_g(p, width), per_g(p, width), per_g(1, p), per_g(1, p), per_g(1, width),
                  per_g(n_seq, p), per_g(n_seq, p)],
        out_specs=[per_g(rows, width), per_g(n_seq, p), per_g(n_seq, p)],
        out_shape=[jax.ShapeDtypeStruct((g, rows, width), F32),
                   jax.ShapeDtypeStruct((g, n_seq, p), F32),
                   jax.ShapeDtypeStruct((g, n_seq, p), F32)],
        scratch_shapes=[pltpu.VMEM((rows, p), F32)] * 4,
        compiler_params=_cparams(("parallel",)),
        name=f"ssm_c{n_chunks}",
    )(ug, ops['toep'], ops['w_re'], ops['w_im'], ops['v_re'], ops['v_im'],
      ops['lbl_re'], ops['lbl_im'], ops['d_row'], s0_re, s0_im)


def _softmax_sink_pv(parts, sink):
    masked = [jnp.where(valid, s, NEG_INF) for s, valid, _ in parts]
    m = sink
    for s in masked:
        m = jnp.maximum(m, jnp.max(s, axis=-1, keepdims=True))
    denom = jnp.exp(sink - m)
    out = None
    for s, (_, _, v) in zip(masked, parts):
        e = jnp.exp(s - m)
        denom = denom + jnp.sum(e, axis=-1, keepdims=True)
        pv = jnp.dot(e.astype(BF16), v, preferred_element_type=F32)
        out = pv if out is None else out + pv
    return out / denom


def _qk(q, k):
    return lax.dot_general(q, k, (((1,), (1,)), ((), ())), preferred_element_type=F32)


def _attn_prompt_kernel(q_ref, kp_ref, kc_ref, vp_ref, vc_ref, sink_ref, o_ref):
    qb = pl.program_id(1)
    rows = Q_PER_KV * WINDOW
    t = lax.broadcasted_iota(I32, (rows, 2 * WINDOW), 0) & (WINDOW - 1)
    col = lax.broadcasted_iota(I32, (rows, 2 * WINDOW), 1)
    valid = jnp.logical_or(jnp.logical_and(jnp.logical_and(col < WINDOW, col > t), qb > 0),
                           jnp.logical_and(col >= WINDOW, (col - WINDOW) <= t))
    scale = HEAD_DIM ** -0.5
    for kv in range(N_KV_HEADS):
        lanes = slice(kv * HEAD_DIM, (kv + 1) * HEAD_DIM)
        k = jnp.concatenate([kp_ref[:, lanes], kc_ref[:, lanes]], axis=0).astype(BF16)
        v = jnp.concatenate([vp_ref[:, lanes], vc_ref[:, lanes]], axis=0).astype(BF16)
        heads = [kv * Q_PER_KV + i for i in range(Q_PER_KV)]
        q = jnp.concatenate([q_ref[:, h * HEAD_DIM:(h + 1) * HEAD_DIM] for h in heads], axis=0).astype(BF16)
        sink = jnp.concatenate(
            [jnp.broadcast_to(sink_ref[0:1, h:h + 1], (WINDOW, 1)) for h in heads], axis=0)
        s = _qk(q, k) * scale
        o = _softmax_sink_pv([(s, valid, v)], sink)
        for i, h in enumerate(heads):
            o_ref[:, h * HEAD_DIM:(h + 1) * HEAD_DIM] = o[i * WINDOW:(i + 1) * WINDOW, :]


def _attn_prompt(proj, sinks, n_batch, seq):
    nb = seq // WINDOW
    cq, ck, cv = COL_Q // D_ATTN, COL_K // D_KV, COL_V // D_KV
    cur = lambda col: (lambda n, b: (n * nb + b, col))
    prev = lambda col: (lambda n, b: (n * nb + jnp.maximum(b - 1, 0), col))
    return pl.pallas_call(
        _attn_prompt_kernel,
        grid=(n_batch, nb),
        in_specs=[
            pl.BlockSpec((WINDOW, D_ATTN), cur(cq)),
            pl.BlockSpec((WINDOW, D_KV), prev(ck)),
            pl.BlockSpec((WINDOW, D_KV), cur(ck)),
            pl.BlockSpec((WINDOW, D_KV), prev(cv)),
            pl.BlockSpec((WINDOW, D_KV), cur(cv)),
            pl.BlockSpec((1, N_HEADS), lambda n, b: (0, 0)),
        ],
        out_specs=pl.BlockSpec((WINDOW, D_ATTN), lambda n, b: (n * nb + b, 0)),
        out_shape=jax.ShapeDtypeStruct((n_batch * seq, D_ATTN), F32),
        compiler_params=_cparams(("parallel", "arbitrary")),
        name="attn_prompt",
    )(proj, proj, proj, proj, proj, sinks)


SAMPLE_GB = 8


def _attn_sample_kernel(q_ref, kn_ref, vn_ref, kc_ref, vc_ref, sink_ref, o_ref, *, t_new):
    tok = SAMPLE_GB * t_new
    rows = Q_PER_KV * tok
    ncache = SAMPLE_GB * WINDOW
    r_c = lax.broadcasted_iota(I32, (rows, ncache), 0)
    c_c = lax.broadcasted_iota(I32, (rows, ncache), 1)
    rb_c = (r_c % tok) // t_new
    rt_c = r_c % t_new
    valid_c = jnp.logical_and(rb_c == c_c // WINDOW, (c_c % WINDOW) > rt_c)
    r_n = lax.broadcasted_iota(I32, (rows, tok), 0)
    c_n = lax.broadcasted_iota(I32, (rows, tok), 1)
    valid_n = jnp.logical_and((r_n % tok) // t_new == c_n // t_new, (c_n % t_new) <= (r_n % t_new))
    scale = HEAD_DIM ** -0.5
    for kv in range(N_KV_HEADS):
        lanes = slice(kv * HEAD_DIM, (kv + 1) * HEAD_DIM)
        kc = kc_ref[:, lanes].astype(BF16)
        vc = vc_ref[:, lanes].astype(BF16)
        kn = kn_ref[:, lanes].astype(BF16)
        vn = vn_ref[:, lanes].astype(BF16)
        heads = [kv * Q_PER_KV + i for i in range(Q_PER_KV)]
        q = jnp.concatenate([q_ref[:, h * HEAD_DIM:(h + 1) * HEAD_DIM] for h in heads], axis=0).astype(BF16)
        sink = jnp.concatenate(
            [jnp.broadcast_to(sink_ref[0:1, h:h + 1], (tok, 1)) for h in heads], axis=0)
        o = _softmax_sink_pv([(_qk(q, kc) * scale, valid_c, vc), (_qk(q, kn) * scale, valid_n, vn)], sink)
        for i, h in enumerate(heads):
            o_ref[:, h * HEAD_DIM:(h + 1) * HEAD_DIM] = o[i * tok:(i + 1) * tok, :]


def _attn_sample(proj, row0, cache_k, cache_v, sinks, n_batch, t_new):
    tok = SAMPLE_GB * t_new
    blk0 = row0 // tok
    cq, ck, cv = COL_Q // D_ATTN, COL_K // D_KV, COL_V // D_KV
    new = lambda col: (lambda g: (blk0 + g, col))
    return pl.pallas_call(
        functools.partial(_attn_sample_kernel, t_new=t_new),
        grid=(n_batch // SAMPLE_GB,),
        in_specs=[
            pl.BlockSpec((tok, D_ATTN), new(cq)),
            pl.BlockSpec((tok, D_KV), new(ck)),
            pl.BlockSpec((tok, D_KV), new(cv)),
            pl.BlockSpec((SAMPLE_GB * WINDOW, D_KV), lambda g: (g, 0)),
            pl.BlockSpec((SAMPLE_GB * WINDOW, D_KV), lambda g: (g, 0)),
            pl.BlockSpec((1, N_HEADS), lambda g: (0, 0)),
        ],
        out_specs=pl.BlockSpec((tok, D_ATTN), lambda g: (g, 0)),
        out_shape=jax.ShapeDtypeStruct((n_batch * t_new, D_ATTN), F32),
        compiler_params=_cparams(("parallel",)),
        name="attn_sample",
    )(proj, proj, proj, cache_k, cache_v, sinks)


TM4 = 256
ROUTER_LANES = 128


def _gelu_tanh(x):
    return 0.5 * x * (1.0 + jnp.tanh(0.7978845608028654 * (x + 0.044715 * (x * x * x))))


def _merge_kernel(y_ref, o_ref, gs_ref, ga_ref, x_ref, wglu_ref, bglu_ref, wbs_ref, wba_ref, wout_ref,
                  g2_ref, wr_ref, br_ref, x1_ref, h2_ref, ti_ref, tg_ref):
    y1 = _gelu_tanh(y_ref[...])
    z = jnp.dot(y1.astype(BF16), wglu_ref[...], preferred_element_type=F32) + bglu_ref[...]
    y2 = y1 * _sigmoid(z)
    bs = jnp.dot(y2.astype(BF16), wbs_ref[...], preferred_element_type=F32)
    ba = jnp.dot(o_ref[...].astype(BF16), wba_ref[...], preferred_element_type=F32)
    merged = gs_ref[...] * bs + ga_ref[...] * ba
    x1 = x_ref[...] + jnp.dot(merged.astype(BF16), wout_ref[...], preferred_element_type=F32)
    x1_ref[...] = x1
    ms = jnp.mean(x1 * x1, axis=-1, keepdims=True)
    h2 = x1 * lax.rsqrt(ms + RMS_EPS) * g2_ref[...]
    h2_ref[...] = h2
    logits = jnp.dot(h2, wr_ref[...], preferred_element_type=F32, precision=HIGHEST) + br_ref[...]
    lane = lax.broadcasted_iota(I32, logits.shape, 1)
    cur = logits
    top_i = jnp.zeros(logits.shape, I32)
    top_v = jnp.zeros(logits.shape, F32)
    for k in range(TOP_K):
        mx = jnp.max(cur, axis=-1, keepdims=True)
        idx = jnp.min(jnp.where(cur == mx, lane, ROUTER_LANES), axis=-1, keepdims=True)
        top_i = jnp.where(lane == k, idx, top_i)
        top_v = jnp.where(lane == k, mx, top_v)
        cur = jnp.where(lane == idx, -jnp.inf, cur)
    v0 = jnp.max(jnp.where(lane < TOP_K, top_v, -jnp.inf), axis=-1, keepdims=True)
    e = jnp.where(lane < TOP_K, jnp.exp(top_v - v0), 0.0)
    ti_ref[...] = top_i
    tg_ref[...] = e / jnp.sum(e, axis=-1, keepdims=True)


def _merge(y_ssm, o_attn, proj, x, wglu, bglu, wbs, wba, wout, g2, wr, br):
    m = x.shape[0]
    row = lambda w: pl.BlockSpec((TM4, w), lambda i: (i, 0))
    const = lambda a, b: pl.BlockSpec((a, b), lambda i: (0, 0), pipeline_mode=pl.Buffered(1))
    return pl.pallas_call(
        _merge_kernel,
        grid=(m // TM4,),
        in_specs=[
            row(D_SSM), row(D_ATTN),
            pl.BlockSpec((TM4, D_MODEL), lambda i: (i, COL_GS // D_MODEL)),
            pl.BlockSpec((TM4, D_MODEL), lambda i: (i, COL_GA // D_MODEL)),
            row(D_MODEL),
            const(D_SSM, D_SSM), const(1, D_SSM), const(D_SSM, D_MODEL), const(D_ATTN, D_MODEL),
            const(D_MODEL, D_MODEL), const(1, D_MODEL), const(D_MODEL, ROUTER_LANES), const(1, ROUTER_LANES),
        ],
        out_specs=[row(D_MODEL), row(D_MODEL), row(ROUTER_LANES), row(ROUTER_LANES)],
        out_shape=[jax.ShapeDtypeStruct((m, D_MODEL), F32), jax.ShapeDtypeStruct((m, D_MODEL), F32),
                   jax.ShapeDtypeStruct((m, ROUTER_LANES), I32), jax.ShapeDtypeStruct((m, ROUTER_LANES), F32)],
        compiler_params=_cparams(("parallel",)),
        name="merge",
    )(y_ssm, o_attn, proj, proj, x, wglu, bglu, wbs, wba, wout, g2, wr, br)


SUB = 128
SUPER = 1280
TF = 256
NF = D_FF // TF


def _moe_sizes(m):
    a = m * TOP_K
    ns_max = N_EXPERTS + a // SUPER
    nb_max = N_EXPERTS + a // SUB
    return a, ns_max, nb_max


def _route(top_i, m):
    a, ns_max, nb_max = _moe_sizes(m)
    e_flat = top_i.reshape(a)
    order = jnp.argsort(e_flat, stable=True).astype(I32)
    e_sorted = e_flat[order]
    tok_sorted = order // TOP_K
    counts = jnp.sum((e_flat[:, None] == jnp.arange(N_EXPERTS, dtype=I32)[None, :]).astype(I32), axis=0)
    starts = jnp.cumsum(counts) - counts
    rank = jnp.arange(a, dtype=I32) - starts[e_sorted]
    nsb = (counts + SUPER - 1) // SUPER
    sb_end = jnp.cumsum(nsb)
    sb_start = sb_end - nsb
    pos = jnp.zeros((a,), I32).at[order].set(sb_start[e_sorted] * SUPER + rank)
    s_idx = jnp.arange(ns_max, dtype=I32)
    ns_used = sb_end[-1]
    s_eff = jnp.minimum(s_idx, ns_used - 1)
    se = jnp.minimum(jnp.searchsorted(sb_end, s_eff, side='right'), N_EXPERTS - 1).astype(I32)
    rows_in = jnp.clip(counts[se] - (s_eff - sb_start[se]) * SUPER, 0, SUPER)
    nsub = jnp.where(s_idx < ns_used, (rows_in + SUB - 1) // SUB, 0).astype(I32)
    nb = (counts + SUB - 1) // SUB
    b_end = jnp.cumsum(nb)
    b_start = b_end - nb
    b_idx = jnp.arange(nb_max, dtype=I32)
    used = b_idx < b_end[-1]
    eb = jnp.minimum(jnp.searchsorted(b_end, b_idx, side='right'), N_EXPERTS - 1).astype(I32)
    ib = b_idx - b_start[eb]
    dst = jnp.where(used, sb_start[eb] * (SUPER // SUB) + ib, ns_max * (SUPER // SUB)).astype(I32)
    r = jnp.arange(SUB, dtype=I32)[None, :]
    within = ib[:, None] * SUB + r
    valid = jnp.logical_and(used[:, None], within < counts[eb][:, None])
    src = jnp.clip(starts[eb][:, None] + within, 0, a - 1)
    tok = jnp.where(valid, tok_sorted[src], 0).astype(I32).reshape(nb_max * SUB)
    return dict(pos=pos, se=se, nsub=nsub, xblk=s_eff.astype(I32), dst=dst, tok=tok)


def _row_copy(src_hbm, row, dst_buf, slot, r, sem):
    return pltpu.make_async_copy(src_hbm.at[pl.ds(row, 1), :], dst_buf.at[slot, pl.ds(r, 1), :], sem.at[slot])


def _gather_kernel(dst_ref, tok_ref, h_hbm, o_ref, buf, sem):
    b = pl.program_id(0)
    nb = pl.num_programs(0)

    def issue(blk, slot):
        def body(r, c):
            _row_copy(h_hbm, tok_ref[blk * SUB + r], buf, slot, r, sem).start()
            return c
        lax.fori_loop(0, SUB, body, 0)

    @pl.when(b == 0)
    def _first():
        issue(0, 0)

    @pl.when(b + 1 < nb)
    def _next():
        issue(b + 1, (b + 1) % 2)

    slot = b % 2

    def wait_body(r, c):
        _row_copy(h_hbm, 0, buf, slot, r, sem).wait()
        return c
    lax.fori_loop(0, SUB, wait_body, 0)
    o_ref[...] = buf[slot].astype(BF16)


def _gather(h2, route, m):
    _, ns_max, nb_max = _moe_sizes(m)
    rows = ns_max * SUPER + SUB
    return pl.pallas_call(
        _gather_kernel,
        grid_spec=pltpu.PrefetchScalarGridSpec(
            num_scalar_prefetch=2,
            grid=(nb_max,),
            in_specs=[pl.BlockSpec(memory_space=pl.ANY)],
            out_specs=pl.BlockSpec((SUB, D_MODEL), lambda b, dst, tok: (dst[b], 0)),
            scratch_shapes=[pltpu.VMEM((2, SUB, D_MODEL), F32), pltpu.SemaphoreType.DMA((2,))],
        ),
        out_shape=jax.ShapeDtypeStruct((rows, D_MODEL), BF16),
        compiler_params=_cparams(("arbitrary",)),
        name="moe_gather",
    )(route['dst'], route['tok'], h2)


def _moe_kernel(se_ref, nsub_ref, xb_ref, x_ref, wg_ref, wl_ref, bg_ref, bl_ref, wd_ref, bd_ref, o_ref,
                wg_s, wl_s, wd_s):
    s = pl.program_id(0)
    f = pl.program_id(1)
    n = nsub_ref[s]

    @pl.when(n > 0)
    def _work():
        wg_s[...] = wg_ref[0].astype(BF16)
        wl_s[...] = wl_ref[0].astype(BF16)
        wd_s[...] = wd_ref[0].astype(BF16)

        @pl.when(f == 0)
        def _init():
            o_ref[...] = jnp.broadcast_to(bd_ref[0], o_ref.shape)

        bg = bg_ref[0]
        bl = bl_ref[0]

        def body(i, c):
            r0 = pl.multiple_of(i * SUB, SUB)
            xs = x_ref[pl.ds(r0, SUB), :]
            hg = jnp.dot(xs, wg_s[...], preferred_element_type=F32) + bg
            hl = jnp.dot(xs, wl_s[...], preferred_element_type=F32) + bl
            hg = jnp.minimum(hg, SWIGLU_LIMIT)
            hl = jnp.clip(hl, -SWIGLU_LIMIT, SWIGLU_LIMIT)
            act = hg * _sigmoid(SWIGLU_ALPHA * hg) * (hl + 1.0)
            o_ref[pl.ds(r0, SUB), :] += jnp.dot(act.astype(BF16), wd_s[...], preferred_element_type=F32)
            return c
        lax.fori_loop(0, n, body, 0)


def _moe(x_sorted, route, w_up, b_up, w_down, b_down, m):
    _, ns_max, _ = _moe_sizes(m)
    rows = x_sorted.shape[0]
    return pl.pallas_call(
        _moe_kernel,
        grid_spec=pltpu.PrefetchScalarGridSpec(
            num_scalar_prefetch=3,
            grid=(ns_max, NF),
            in_specs=[
                pl.BlockSpec((SUPER, D_MODEL), lambda s, f, se, ns, xb: (xb[s], 0)),
                pl.BlockSpec((1, D_MODEL, TF), lambda s, f, se, ns, xb: (se[s], 0, f)),
                pl.BlockSpec((1, D_MODEL, TF), lambda s, f, se, ns, xb: (se[s], 0, NF + f)),
                pl.BlockSpec((1, 1, TF), lambda s, f, se, ns, xb: (se[s], 0, f)),
                pl.BlockSpec((1, 1, TF), lambda s, f, se, ns, xb: (se[s], 0, NF + f)),
                pl.BlockSpec((1, TF, D_MODEL), lambda s, f, se, ns, xb: (se[s], f, 0)),
                pl.BlockSpec((1, 1, D_MODEL), lambda s, f, se, ns, xb: (se[s], 0, 0)),
            ],
            out_specs=pl.BlockSpec((SUPER, D_MODEL), lambda s, f, se, ns, xb: (xb[s], 0)),
            scratch_shapes=[pltpu.VMEM((D_MODEL, TF), BF16), pltpu.VMEM((D_MODEL, TF), BF16),
                            pltpu.VMEM((TF, D_MODEL), BF16)],
        ),
        out_shape=jax.ShapeDtypeStruct((rows, D_MODEL), F32),
        compiler_params=_cparams(("arbitrary", "arbitrary")),
        name="moe_experts",
    )(route['se'], route['nsub'], route['xblk'], x_sorted, w_up, w_up, b_up, b_up, w_down, b_down)


TC8 = 32


def _combine_kernel(pos_ref, y_hbm, x1_ref, gate_ref, g_ref, o_ref, buf, sem):
    b = pl.program_id(0)
    nb = pl.num_programs(0)

    def copies(blk, slot, wait):
        def body(r, c):
            for k in range(TOP_K):
                row = 0 if wait else pos_ref[(blk * TC8 + r) * TOP_K + k]
                cp = pltpu.make_async_copy(y_hbm.at[pl.ds(row, 1), :], buf.at[slot, k, pl.ds(r, 1), :], sem.at[slot])
                cp.wait() if wait else cp.start()
            return c
        lax.fori_loop(0, TC8, body, 0)

    @pl.when(b == 0)
    def _first():
        copies(0, 0, False)

    @pl.when(b + 1 < nb)
    def _next():
        copies(b + 1, (b + 1) % 2, False)

    slot = b % 2
    copies(b, slot, True)
    acc = x1_ref[...]
    gate = gate_ref[...]
    for k in range(TOP_K):
        acc = acc + gate[:, k:k + 1] * buf[slot, k]
    ms = jnp.mean(acc * acc, axis=-1, keepdims=True)
    o_ref[...] = acc * lax.rsqrt(ms + RMS_EPS) * g_ref[...]


def _combine(y_sorted, route, x1, gate, g_final):
    m = x1.shape[0]
    return pl.pallas_call(
        _combine_kernel,
        grid_spec=pltpu.PrefetchScalarGridSpec(
            num_scalar_prefetch=1,
            grid=(m // TC8,),
            in_specs=[
                pl.BlockSpec(memory_space=pl.ANY),
                pl.BlockSpec((TC8, D_MODEL), lambda b, pos: (b, 0)),
                pl.BlockSpec((TC8, ROUTER_LANES), lambda b, pos: (b, 0)),
                pl.BlockSpec((1, D_MODEL), lambda b, pos: (0, 0)),
            ],
            out_specs=pl.BlockSpec((TC8, D_MODEL), lambda b, pos: (b, 0)),
            scratch_shapes=[pltpu.VMEM((2, TOP_K, TC8, D_MODEL), F32), pltpu.SemaphoreType.DMA((2,))],
        ),
        out_shape=jax.ShapeDtypeStruct((m, D_MODEL), F32),
        compiler_params=_cparams(("arbitrary",)),
        name="moe_combine",
    )(route['pos'], y_sorted, x1, gate, g_final)


SSM_CHUNK = 32


def _rope_tables(seq, t_new, n_batch, n_dec):
    half = HEAD_DIM // 2
    inv = ROPE_THETA ** (-jnp.arange(half, dtype=F32) / half)
    pos = jnp.concatenate([jnp.tile(jnp.arange(seq), n_batch), jnp.tile(PAST_LEN + jnp.arange(t_new), n_dec)])
    ang = pos.astype(F32)[:, None] * inv[None, :]
    cos = jnp.cos(ang)
    sin = jnp.sin(ang)
    return jnp.tile(cos, (1, 4)), jnp.tile(jnp.concatenate([-sin, sin], axis=1), (1, 2))


def kernel(x_prompt, x_sample, state_ssm_re, state_ssm_im, cache_k, cache_v, attn_norm_g, w_in, b_in, ssm_a_re, ssm_a_im, ssm_log_dt, ssm_b_re, ssm_b_im, ssm_c_re, ssm_c_im, ssm_d, w_glu, b_glu, attn_sinks, w_branch_ssm, w_branch_attn, w_out, ffn_norm_g, w_router, b_router, w_up, b_up, w_down, b_down, final_norm_g):
    n_batch, seq, _ = x_prompt.shape
    n_dec, t_new, _ = x_sample.shape
    mp = n_batch * seq
    ms = n_dec * t_new
    m = mp + ms
    g, p, h = SSM_GROUPS, SSM_STATE, SSM_GROUP
    x = jnp.concatenate([x_prompt.reshape(mp, D_MODEL), x_sample.reshape(ms, D_MODEL)], axis=0)

    wi = w_in[0]
    sp = (0, D_SSM, D_SSM + D_ATTN, D_SSM + D_ATTN + D_KV, D_SSM + D_ATTN + 2 * D_KV,
          D_SSM + D_ATTN + 2 * D_KV + D_MODEL, N_IN)
    order = (4, 5, 0, 1, 2, 3)
    w_in_p = jnp.concatenate([wi[:, sp[i]:sp[i + 1]] for i in order], axis=1).astype(BF16)
    b_in_p = jnp.concatenate([b_in[0][sp[i]:sp[i + 1]] for i in order]).reshape(1, N_IN)
    cos_t, sin_t = _rope_tables(seq, t_new, n_batch, n_dec)
    proj = _inproj(x, attn_norm_g[0].reshape(1, D_MODEL), w_in_p, b_in_p, cos_t, sin_t)

    ssm_params = (ssm_a_re[0], ssm_a_im[0], ssm_log_dt[0], ssm_b_re[0], ssm_b_im[0], ssm_c_re[0], ssm_c_im[0], ssm_d[0])
    u_all = proj[:, COL_U:COL_U + D_SSM]
    nc = seq // SSM_CHUNK
    ops_p = _ssm_operators(*ssm_params, SSM_CHUNK)
    ug_p = jnp.transpose(u_all[:mp].reshape(n_batch, nc, SSM_CHUNK, g, h), (3, 1, 0, 2, 4)).reshape(g, nc * n_batch, SSM_CHUNK * h)
    zeros = jnp.zeros((g, n_batch, p), F32)
    yg_p, sp_re, sp_im = _ssm(ug_p, ops_p, zeros, zeros, nc, n_batch)
    y_p = jnp.transpose(yg_p.reshape(g, nc, n_batch, SSM_CHUNK, h), (2, 1, 3, 0, 4)).reshape(mp, D_SSM)
    ops_s = _ssm_operators(*ssm_params, t_new)
    ug_s = jnp.transpose(u_all[mp:].reshape(n_dec, t_new, g, h), (2, 0, 1, 3)).reshape(g, n_dec, t_new * h)
    s0_re = jnp.transpose(state_ssm_re[0], (1, 0, 2))
    s0_im = jnp.transpose(state_ssm_im[0], (1, 0, 2))
    yg_s, ss_re, ss_im = _ssm(ug_s, ops_s, s0_re, s0_im, 1, n_dec)
    y_s = jnp.transpose(yg_s.reshape(g, n_dec, t_new, h), (1, 2, 0, 3)).reshape(ms, D_SSM)
    y_ssm = jnp.concatenate([y_p, y_s], axis=0)

    sinks = attn_sinks[0].reshape(1, N_HEADS)
    o_p = _attn_prompt(proj, sinks, n_batch, seq)
    o_s = _attn_sample(proj, mp, cache_k[0].reshape(n_dec * WINDOW, D_KV), cache_v[0].reshape(n_dec * WINDOW, D_KV),
                       sinks, n_dec, t_new)
    o_attn = jnp.concatenate([o_p, o_s], axis=0)

    wr = jnp.pad(w_router[0], ((0, 0), (0, ROUTER_LANES - N_EXPERTS)))
    br = jnp.pad(b_router[0], (0, ROUTER_LANES - N_EXPERTS), constant_values=NEG_INF).reshape(1, ROUTER_LANES)
    x1, h2, top_i, gate = _merge(
        y_ssm, o_attn, proj, x, w_glu[0].astype(BF16), b_glu[0].reshape(1, D_SSM), w_branch_ssm[0].astype(BF16),
        w_branch_attn[0].astype(BF16), w_out[0].astype(BF16), ffn_norm_g[0].reshape(1, D_MODEL), wr, br)

    route = _route(top_i[:, :TOP_K], m)
    x_sorted = _gather(h2, route, m)
    y_sorted = _moe(x_sorted, route, w_up[0], b_up[0].reshape(N_EXPERTS, 1, 2 * D_FF), w_down[0],
                    b_down[0].reshape(N_EXPERTS, 1, D_MODEL), m)
    y = _combine(y_sorted, route, x1, gate, final_norm_g.reshape(1, D_MODEL))

    y_prompt = y[:mp].reshape(n_batch, seq, D_MODEL)
    y_sample = y[mp:].reshape(n_dec, t_new, D_MODEL)
    to_state = lambda s: jnp.transpose(s, (1, 0, 2))[None]
    k_all = proj[:, COL_K:COL_K + D_KV]
    v_all = proj[:, COL_V:COL_V + D_KV]
    kv_p = lambda a: a[:mp].reshape(n_batch, seq, N_KV_HEADS, HEAD_DIM)[:, -WINDOW:][None]
    kv_s = lambda a, c: jnp.concatenate([c[0][:, t_new:], a[mp:].reshape(n_dec, t_new, N_KV_HEADS, HEAD_DIM)], axis=1)[None]
    return (y_prompt, y_sample, to_state(sp_re), to_state(sp_im), kv_p(k_all), kv_p(v_all),
            to_state(ss_re), to_state(ss_im), kv_s(k_all, cache_k), kv_s(v_all, cache_v))
```

```python
import functools

import jax
import jax.numpy as jnp
from jax import lax
from jax.experimental import pallas as pl
from jax.experimental.pallas import tpu as pltpu

F32 = jnp.float32
BF16 = jnp.bfloat16
I32 = jnp.int32
HIGHEST = lax.Precision.HIGHEST

D_MODEL = 2048
RMS_EPS = 1e-5
SSM_GROUP = 16
SSM_GROUPS = 64
SSM_STATE = 64
D_SSM = 1024
HEAD_DIM = 64
N_HEADS = 16
N_KV_HEADS = 4
Q_PER_KV = 4
D_ATTN = 1024
D_KV = 256
WINDOW = 128
ROPE_THETA = 10000.0
NEG_INF = -1e30
PAST_LEN = 16384
N_IN = D_SSM + D_ATTN + 2 * D_KV + 2 * D_MODEL
N_EXPERTS = 32
TOP_K = 4
D_FF = 2048
SWIGLU_ALPHA = 1.702
SWIGLU_LIMIT = 7.0

COL_GS, COL_GA, COL_U, COL_Q, COL_K, COL_V = 0, 2048, 4096, 5120, 6144, 6400

VMEM_LIMIT = 56 * 1024 * 1024


def _sigmoid(x):
    return 1.0 / (1.0 + jnp.exp(-x))


def _cparams(sem, vmem=VMEM_LIMIT):
    return pltpu.CompilerParams(dimension_semantics=sem, vmem_limit_bytes=vmem)


TM1 = 512
TN1 = 512
_GATE_TILES = (COL_U // TN1)
_ROPE_TILE0 = COL_Q // TN1
_KV_TILE = COL_K // TN1


def _inproj_kernel(x_ref, g_ref, w_ref, b_ref, cos_ref, sin_ref, o_ref, h_scr):
    j = pl.program_id(1)

    @pl.when(j == 0)
    def _norm():
        x = x_ref[...]
        ms = jnp.mean(x * x, axis=-1, keepdims=True)
        h_scr[...] = (x * lax.rsqrt(ms + RMS_EPS) * g_ref[...]).astype(BF16)

    acc = jnp.dot(h_scr[...], w_ref[...], preferred_element_type=F32) + b_ref[...]

    @pl.when(j < _GATE_TILES)
    def _gate():
        o_ref[...] = _sigmoid(acc)

    @pl.when(jnp.logical_and(j >= _GATE_TILES, j < _ROPE_TILE0))
    def _plain():
        o_ref[...] = acc

    @pl.when(j >= _ROPE_TILE0)
    def _rope():
        c = jnp.concatenate([cos_ref[...]] * (TN1 // 128), axis=1)
        s = jnp.concatenate([sin_ref[...]] * (TN1 // 128), axis=1)
        lane = lax.broadcasted_iota(I32, acc.shape, 1)
        first_half = (lane & (HEAD_DIM // 2)) == 0
        partner = jnp.where(first_half, pltpu.roll(acc, TN1 - HEAD_DIM // 2, 1), pltpu.roll(acc, HEAD_DIM // 2, 1))
        roped = acc * c + partner * s
        n_rot = jnp.where(j == _KV_TILE, D_KV, TN1)
        o_ref[...] = jnp.where(lane < n_rot, roped, acc)


def _inproj(x, g, w_bf, b, cos_t, sin_t):
    m = x.shape[0]
    return pl.pallas_call(
        _inproj_kernel,
        grid=(m // TM1, N_IN // TN1),
        in_specs=[
            pl.BlockSpec((TM1, D_MODEL), lambda i, j: (i, 0)),
            pl.BlockSpec((1, D_MODEL), lambda i, j: (0, 0)),
            pl.BlockSpec((D_MODEL, TN1), lambda i, j: (0, j)),
            pl.BlockSpec((1, TN1), lambda i, j: (0, j)),
            pl.BlockSpec((TM1, 128), lambda i, j: (i, 0)),
            pl.BlockSpec((TM1, 128), lambda i, j: (i, 0)),
        ],
        out_specs=pl.BlockSpec((TM1, TN1), lambda i, j: (i, j)),
        out_shape=jax.ShapeDtypeStruct((m, N_IN), F32),
        scratch_shapes=[pltpu.VMEM((TM1, D_MODEL), BF16)],
        compiler_params=_cparams(("parallel", "arbitrary")),
        name="inproj",
    )(x, g, w_bf, b, cos_t, sin_t)


SSM_GB = 8


def _ssm_operators(a_re, a_im, log_dt, b_re, b_im, c_re, c_im, d_skip, chunk, w_dtype):
    a_re = a_re.astype(F32)
    a_im = a_im.astype(F32)
    dt = jnp.exp(log_dt.astype(F32))[:, None]
    mag = jnp.exp(a_re * dt)
    lb_re = mag * jnp.cos(a_im * dt)
    lb_im = mag * jnp.sin(a_im * dt)
    inv = 1.0 / (a_re * a_re + a_im * a_im)
    f_re = ((lb_re - 1.0) * a_re + lb_im * a_im) * inv
    f_im = (lb_im * a_re - (lb_re - 1.0) * a_im) * inv
    br = b_re.astype(F32)
    bi = b_im.astype(F32)
    bb_re = f_re[..., None] * br - f_im[..., None] * bi
    bb_im = f_re[..., None] * bi + f_im[..., None] * br
    k = jnp.arange(chunk + 1, dtype=F32)[:, None, None]
    pmag = jnp.exp(a_re * dt * k)
    pw_re = pmag * jnp.cos(a_im * dt * k)
    pw_im = pmag * jnp.sin(a_im * dt * k)
    g, p, h = bb_re.shape
    nlb = g // SSM_GB
    eye = jnp.eye(SSM_GB, dtype=F32)
    pb_re = pw_re[:chunk, :, :, None] * bb_re[None] - pw_im[:chunk, :, :, None] * bb_im[None]
    pb_im = pw_re[:chunk, :, :, None] * bb_im[None] + pw_im[:chunk, :, :, None] * bb_re[None]

    def to_state(pb):
        x = pb[::-1].reshape(chunk, nlb, SSM_GB, p, h)
        return jnp.einsum('jbgph,gk->bjghkp', x, eye).reshape(nlb, chunk * 128, SSM_GB * p).astype(w_dtype)
    cr = c_re.astype(F32)
    ci = c_im.astype(F32)
    crt = jnp.transpose(cr, (0, 2, 1))[None]
    cit = jnp.transpose(ci, (0, 2, 1))[None]
    e_re = pw_re[1:, :, :, None]
    e_im = pw_im[1:, :, :, None]

    def from_state(v):
        x = v.reshape(chunk, nlb, SSM_GB, p, h)
        return jnp.einsum('tbgph,gk->bgptkh', x, eye).reshape(nlb, SSM_GB * p, chunk * 128).astype(BF16)
    lagk = (jnp.einsum('gop,lgph->lgoh', cr, pb_re, precision=HIGHEST)
            - jnp.einsum('gop,lgph->lgoh', ci, pb_im, precision=HIGHEST))
    jj = jnp.arange(chunk)[:, None]
    tt = jnp.arange(chunk)[None, :]
    lag = tt - jj
    blocks = jnp.where((lag >= 0)[:, :, None, None, None], lagk[jnp.maximum(lag, 0)], 0.0)
    toep = jnp.einsum('jtbgoi,gk->bjgitko', blocks.reshape(chunk, chunk, nlb, SSM_GB, h, h), eye)
    toep = toep.reshape(nlb, chunk * 128, chunk * 128).astype(BF16)
    d_row = jnp.tile(d_skip.astype(F32).reshape(nlb, 1, 128), (1, 1, chunk))
    return dict(toep=toep, w_re=to_state(pb_re), w_im=to_state(pb_im),
                v_re=from_state(crt * e_re - cit * e_im), v_im=from_state(-(crt * e_im + cit * e_re)),
                lbl_re=pw_re[chunk].reshape(1, g * p), lbl_im=pw_im[chunk].reshape(1, g * p), d_row=d_row)


def _ssm_kernel(*refs, chunk, n_chunks, n_seq, aliased):
    (u_ref, t_ref, wr_ref, wi_ref, vr_ref, vi_ref, lr_ref, li_ref, d_ref, s0r_ref, s0i_ref) = refs[:11]
    y_ref, sr_ref, si_ref, z_scr, lor_scr, loi_scr, pr_scr, pi_scr = refs[11 + aliased:]
    rows = n_seq * n_chunks
    for j in range(chunk):
        z_scr[:, j * 128:(j + 1) * 128] = u_ref[pl.ds(j, rows, stride=chunk), :]
    z = z_scr[...]
    zw = z.astype(wr_ref.dtype)
    prec = HIGHEST if wr_ref.dtype == F32 else None
    lor = jnp.dot(zw, wr_ref[0], preferred_element_type=F32, precision=prec)
    loi = jnp.dot(zw, wi_ref[0], preferred_element_type=F32, precision=prec)
    lr = lr_ref[...]
    li = li_ref[...]
    sr = s0r_ref[...]
    si = s0i_ref[...]
    if n_chunks == 1:
        pr, pi = sr, si
        sr, si = lr * sr - li * si + lor, lr * si + li * sr + loi
    else:
        pieces = range(lor.shape[1] // 128)
        cut = lambda a: tuple(a[:, k * 128:(k + 1) * 128] for k in pieces)
        for k in pieces:
            lor_scr[k] = lor[:, k * 128:(k + 1) * 128]
            loi_scr[k] = loi[:, k * 128:(k + 1) * 128]
        lrs, lis = cut(lr), cut(li)

        def step(c, carry):
            idx = pl.ds(c, n_seq, stride=n_chunks)
            nxt = []
            for k, (a, b) in enumerate(zip(*carry)):
                pr_scr[k, idx, :] = a
                pi_scr[k, idx, :] = b
                nxt.append((lrs[k] * a - lis[k] * b + lor_scr[k, idx, :], lrs[k] * b + lis[k] * a + loi_scr[k, idx, :]))
            return tuple(n[0] for n in nxt), tuple(n[1] for n in nxt)
        srs, sis = lax.fori_loop(0, n_chunks, step, (cut(sr), cut(si)), unroll=8)
        sr, si = jnp.concatenate(srs, axis=1), jnp.concatenate(sis, axis=1)
        pr = jnp.concatenate([pr_scr[k] for k in pieces], axis=1)
        pi = jnp.concatenate([pi_scr[k] for k in pieces], axis=1)
    sr_ref[...] = sr
    si_ref[...] = si
    y = jnp.dot(z.astype(BF16), t_ref[0], preferred_element_type=F32)
    y += jnp.dot(pr.astype(BF16), vr_ref[0], preferred_element_type=F32)
    y += jnp.dot(pi.astype(BF16), vi_ref[0], preferred_element_type=F32)
    y += z * d_ref[0]
    for j in range(chunk):
        y_ref[pl.ds(j, rows, stride=chunk), :] = y[:, j * 128:(j + 1) * 128]


def _ssm(proj, row0, m_out, n_seq, n_chunks, chunk, ops, s0_re, s0_im, y_prev=None):
    rows = n_seq * n_chunks
    tok = rows * chunk
    nlb = SSM_GROUPS // SSM_GB
    width = chunk * 128
    sw = SSM_GB * SSM_STATE
    blk0 = row0 // tok
    per_lb = lambda a, b: pl.BlockSpec((1, a, b), lambda i: (i, 0, 0))
    lanes = lambda r, w: pl.BlockSpec((r, w), lambda i: (0, i))
    in_specs = [pl.BlockSpec((tok, 128), lambda i: (blk0, COL_U // 128 + i)),
                per_lb(width, width), per_lb(width, sw), per_lb(width, sw), per_lb(sw, width), per_lb(sw, width),
                lanes(1, sw), lanes(1, sw), per_lb(1, width), lanes(n_seq, sw), lanes(n_seq, sw)]
    args = [proj, ops['toep'], ops['w_re'], ops['w_im'], ops['v_re'], ops['v_im'],
            ops['lbl_re'], ops['lbl_im'], ops['d_row'], s0_re, s0_im]
    aliases = {}
    if y_prev is not None:
        in_specs.append(pl.BlockSpec(memory_space=pl.ANY))
        args.append(y_prev)
        aliases = {len(args) - 1: 0}
    state = jax.ShapeDtypeStruct((n_seq, SSM_GROUPS * SSM_STATE), F32)
    return pl.pallas_call(
        functools.partial(_ssm_kernel, chunk=chunk, n_chunks=n_chunks, n_seq=n_seq, aliased=int(y_prev is not None)),
        grid=(nlb,),
        in_specs=in_specs,
        out_specs=[pl.BlockSpec((tok, 128), lambda i: (blk0, i)), lanes(n_seq, sw), lanes(n_seq, sw)],
        out_shape=[jax.ShapeDtypeStruct((m_out, D_SSM), F32), state, state],
        scratch_shapes=[pltpu.VMEM((rows, width), F32)] + [pltpu.VMEM((sw // 128, rows, 128), F32)] * 4,
        input_output_aliases=aliases,
        compiler_params=_cparams(("parallel",)),
        name=f"ssm_c{n_chunks}",
    )(*args)


def _softmax_sink_pv(parts, sink):
    masked = [jnp.where(valid, s, NEG_INF) for s, valid, _ in parts]
    m = sink
    for s in masked:
        m = jnp.maximum(m, jnp.max(s, axis=-1, keepdims=True))
    denom = jnp.exp(sink - m)
    out = None
    for s, (_, _, v) in zip(masked, parts):
        e = jnp.exp(s - m)
        denom = denom + jnp.sum(e, axis=-1, keepdims=True)
        pv = jnp.dot(e.astype(BF16), v, preferred_element_type=F32)
        out = pv if out is None else out + pv
    return out / denom


def _qk(q, k):
    return lax.dot_general(q, k, (((1,), (1,)), ((), ())), preferred_element_type=F32)


def _attn_prompt_kernel(q_ref, kp_ref, kc_ref, vp_ref, vc_ref, sink_ref, o_ref):
    qb = pl.program_id(1)
    rows = Q_PER_KV * WINDOW
    t = lax.broadcasted_iota(I32, (rows, 2 * WINDOW), 0) & (WINDOW - 1)
    col = lax.broadcasted_iota(I32, (rows, 2 * WINDOW), 1)
    valid = jnp.logical_or(jnp.logical_and(jnp.logical_and(col < WINDOW, col > t), qb > 0),
                           jnp.logical_and(col >= WINDOW, (col - WINDOW) <= t))
    scale = HEAD_DIM ** -0.5
    for kv in range(N_KV_HEADS):
        lanes = slice(kv * HEAD_DIM, (kv + 1) * HEAD_DIM)
        k = jnp.concatenate([kp_ref[:, lanes], kc_ref[:, lanes]], axis=0).astype(BF16)
        v = jnp.concatenate([vp_ref[:, lanes], vc_ref[:, lanes]], axis=0).astype(BF16)
        heads = [kv * Q_PER_KV + i for i in range(Q_PER_KV)]
        q = jnp.concatenate([q_ref[:, h * HEAD_DIM:(h + 1) * HEAD_DIM] for h in heads], axis=0).astype(BF16)
        sink = jnp.concatenate(
            [jnp.broadcast_to(sink_ref[0:1, h:h + 1], (WINDOW, 1)) for h in heads], axis=0)
        s = _qk(q, k) * scale
        o = _softmax_sink_pv([(s, valid, v)], sink)
        for i, h in enumerate(heads):
            o_ref[:, h * HEAD_DIM:(h + 1) * HEAD_DIM] = o[i * WINDOW:(i + 1) * WINDOW, :]


def _attn_prompt(proj, sinks, n_batch, seq, m_out):
    nb = seq // WINDOW
    cq, ck, cv = COL_Q // D_ATTN, COL_K // D_KV, COL_V // D_KV
    cur = lambda col: (lambda n, b: (n * nb + b, col))
    prev = lambda col: (lambda n, b: (n * nb + jnp.maximum(b - 1, 0), col))
    return pl.pallas_call(
        _attn_prompt_kernel,
        grid=(n_batch, nb),
        in_specs=[
            pl.BlockSpec((WINDOW, D_ATTN), cur(cq)),
            pl.BlockSpec((WINDOW, D_KV), prev(ck)),
            pl.BlockSpec((WINDOW, D_KV), cur(ck)),
            pl.BlockSpec((WINDOW, D_KV), prev(cv)),
            pl.BlockSpec((WINDOW, D_KV), cur(cv)),
            pl.BlockSpec((1, N_HEADS), lambda n, b: (0, 0)),
        ],
        out_specs=pl.BlockSpec((WINDOW, D_ATTN), lambda n, b: (n * nb + b, 0)),
        out_shape=jax.ShapeDtypeStruct((m_out, D_ATTN), F32),
        compiler_params=_cparams(("parallel", "arbitrary")),
        name="attn_prompt",
    )(proj, proj, proj, proj, proj, sinks)


SAMPLE_GB = 8


def _attn_sample_kernel(q_ref, kn_ref, vn_ref, kc_ref, vc_ref, sink_ref, o_ref, *, t_new):
    tok = SAMPLE_GB * t_new
    rows = Q_PER_KV * tok
    ncache = SAMPLE_GB * WINDOW
    r_c = lax.broadcasted_iota(I32, (rows, ncache), 0)
    c_c = lax.broadcasted_iota(I32, (rows, ncache), 1)
    rb_c = (r_c % tok) // t_new
    rt_c = r_c % t_new
    valid_c = jnp.logical_and(rb_c == c_c // WINDOW, (c_c % WINDOW) > rt_c)
    r_n = lax.broadcasted_iota(I32, (rows, tok), 0)
    c_n = lax.broadcasted_iota(I32, (rows, tok), 1)
    valid_n = jnp.logical_and((r_n % tok) // t_new == c_n // t_new, (c_n % t_new) <= (r_n % t_new))
    scale = HEAD_DIM ** -0.5
    for kv in range(N_KV_HEADS):
        lanes = slice(kv * HEAD_DIM, (kv + 1) * HEAD_DIM)
        kc = kc_ref[:, lanes].astype(BF16)
        vc = vc_ref[:, lanes].astype(BF16)
        kn = kn_ref[:, lanes].astype(BF16)
        vn = vn_ref[:, lanes].astype(BF16)
        heads = [kv * Q_PER_KV + i for i in range(Q_PER_KV)]
        q = jnp.concatenate([q_ref[:, h * HEAD_DIM:(h + 1) * HEAD_DIM] for h in heads], axis=0).astype(BF16)
        sink = jnp.concatenate(
            [jnp.broadcast_to(sink_ref[0:1, h:h + 1], (tok, 1)) for h in heads], axis=0)
        o = _softmax_sink_pv([(_qk(q, kc) * scale, valid_c, vc), (_qk(q, kn) * scale, valid_n, vn)], sink)
        for i, h in enumerate(heads):
            o_ref[:, h * HEAD_DIM:(h + 1) * HEAD_DIM] = o[i * tok:(i + 1) * tok, :]


def _attn_sample_aliased(q_ref, kn_ref, vn_ref, kc_ref, vc_ref, sink_ref, o_prev, o_ref, *, t_new):
    del o_prev
    _attn_sample_kernel(q_ref, kn_ref, vn_ref, kc_ref, vc_ref, sink_ref, o_ref, t_new=t_new)


def _attn_sample(proj, row0, cache_k, cache_v, sinks, n_batch, t_new, o_prev):
    tok = SAMPLE_GB * t_new
    blk0 = row0 // tok
    cq, ck, cv = COL_Q // D_ATTN, COL_K // D_KV, COL_V // D_KV
    new = lambda col: (lambda g: (blk0 + g, col))
    return pl.pallas_call(
        functools.partial(_attn_sample_aliased, t_new=t_new),
        grid=(n_batch // SAMPLE_GB,),
        in_specs=[
            pl.BlockSpec((tok, D_ATTN), new(cq)),
            pl.BlockSpec((tok, D_KV), new(ck)),
            pl.BlockSpec((tok, D_KV), new(cv)),
            pl.BlockSpec((SAMPLE_GB * WINDOW, D_KV), lambda g: (g, 0)),
            pl.BlockSpec((SAMPLE_GB * WINDOW, D_KV), lambda g: (g, 0)),
            pl.BlockSpec((1, N_HEADS), lambda g: (0, 0)),
            pl.BlockSpec(memory_space=pl.ANY),
        ],
        out_specs=pl.BlockSpec((tok, D_ATTN), lambda g: (blk0 + g, 0)),
        out_shape=jax.ShapeDtypeStruct(o_prev.shape, F32),
        input_output_aliases={6: 0},
        compiler_params=_cparams(("parallel",)),
        name="attn_sample",
    )(proj, proj, proj, cache_k, cache_v, sinks, o_prev)


TM4 = 256
ROUTER_LANES = 128


def _gelu_tanh(x):
    return 0.5 * x * (1.0 + jnp.tanh(0.7978845608028654 * (x + 0.044715 * (x * x * x))))


def _merge_kernel(y_ref, o_ref, gs_ref, ga_ref, x_ref, wglu_ref, bglu_ref, wbs_ref, wba_ref, wout_ref,
                  g2_ref, wr_ref, br_ref, x1_ref, h2_ref, ti_ref, tg_ref):
    y1 = _gelu_tanh(y_ref[...])
    z = jnp.dot(y1.astype(BF16), wglu_ref[...], preferred_element_type=F32) + bglu_ref[...]
    y2 = y1 * _sigmoid(z)
    bs = jnp.dot(y2.astype(BF16), wbs_ref[...], preferred_element_type=F32)
    ba = jnp.dot(o_ref[...].astype(BF16), wba_ref[...], preferred_element_type=F32)
    merged = gs_ref[...] * bs + ga_ref[...] * ba
    x1 = x_ref[...] + jnp.dot(merged.astype(BF16), wout_ref[...], preferred_element_type=F32)
    x1_ref[...] = x1
    ms = jnp.mean(x1 * x1, axis=-1, keepdims=True)
    h2 = x1 * lax.rsqrt(ms + RMS_EPS) * g2_ref[...]
    h2_ref[...] = h2
    logits = jnp.dot(h2, wr_ref[...], preferred_element_type=F32, precision=HIGHEST) + br_ref[...]
    lane = lax.broadcasted_iota(I32, logits.shape, 1)
    cur = logits
    top_i = jnp.zeros(logits.shape, I32)
    top_v = jnp.zeros(logits.shape, F32)
    for k in range(TOP_K):
        mx = jnp.max(cur, axis=-1, keepdims=True)
        idx = jnp.min(jnp.where(cur == mx, lane, ROUTER_LANES), axis=-1, keepdims=True)
        top_i = jnp.where(lane == k, idx, top_i)
        top_v = jnp.where(lane == k, mx, top_v)
        cur = jnp.where(lane == idx, -jnp.inf, cur)
    v0 = jnp.max(jnp.where(lane < TOP_K, top_v, -jnp.inf), axis=-1, keepdims=True)
    e = jnp.where(lane < TOP_K, jnp.exp(top_v - v0), 0.0)
    ti_ref[...] = top_i
    tg_ref[...] = e / jnp.sum(e, axis=-1, keepdims=True)


def _merge(y_ssm, o_attn, proj, x, wglu, bglu, wbs, wba, wout, g2, wr, br):
    m = x.shape[0]
    row = lambda w: pl.BlockSpec((TM4, w), lambda i: (i, 0))
    const = lambda a, b: pl.BlockSpec((a, b), lambda i: (0, 0), pipeline_mode=pl.Buffered(1))
    return pl.pallas_call(
        _merge_kernel,
        grid=(m // TM4,),
        in_specs=[
            row(D_SSM), row(D_ATTN),
            pl.BlockSpec((TM4, D_MODEL), lambda i: (i, COL_GS // D_MODEL)),
            pl.BlockSpec((TM4, D_MODEL), lambda i: (i, COL_GA // D_MODEL)),
            row(D_MODEL),
            const(D_SSM, D_SSM), const(1, D_SSM), const(D_SSM, D_MODEL), const(D_ATTN, D_MODEL),
            const(D_MODEL, D_MODEL), const(1, D_MODEL), const(D_MODEL, ROUTER_LANES), const(1, ROUTER_LANES),
        ],
        out_specs=[row(D_MODEL), row(D_MODEL), row(ROUTER_LANES), row(ROUTER_LANES)],
        out_shape=[jax.ShapeDtypeStruct((m, D_MODEL), F32), jax.ShapeDtypeStruct((m, D_MODEL), F32),
                   jax.ShapeDtypeStruct((m, ROUTER_LANES), I32), jax.ShapeDtypeStruct((m, ROUTER_LANES), F32)],
        compiler_params=_cparams(("parallel",)),
        name="merge",
    )(y_ssm, o_attn, proj, proj, x, wglu, bglu, wbs, wba, wout, g2, wr, br)


SUB = 128
SUPER = 1280
TF = 256
NF = D_FF // TF
MOE_WIDE = 8


def _moe_sizes(m):
    a = m * TOP_K
    ns_max = N_EXPERTS + a // SUPER
    nb_max = N_EXPERTS + a // SUB
    return a, ns_max, nb_max


def _route(top_i, m):
    a, ns_max, nb_max = _moe_sizes(m)
    e_flat = top_i.reshape(a)
    order = jnp.argsort(e_flat, stable=True).astype(I32)
    e_sorted = e_flat[order]
    tok_sorted = order // TOP_K
    counts = jnp.sum((e_flat[:, None] == jnp.arange(N_EXPERTS, dtype=I32)[None, :]).astype(I32), axis=0)
    starts = jnp.cumsum(counts) - counts
    rank = jnp.arange(a, dtype=I32) - starts[e_sorted]
    nsb = (counts + SUPER - 1) // SUPER
    sb_end = jnp.cumsum(nsb)
    sb_start = sb_end - nsb
    pos = jnp.zeros((a,), I32).at[order].set(sb_start[e_sorted] * SUPER + rank)
    s_idx = jnp.arange(ns_max, dtype=I32)
    ns_used = sb_end[-1]
    s_eff = jnp.minimum(s_idx, ns_used - 1)
    se = jnp.minimum(jnp.searchsorted(sb_end, s_eff, side='right'), N_EXPERTS - 1).astype(I32)
    rows_in = jnp.clip(counts[se] - (s_eff - sb_start[se]) * SUPER, 0, SUPER)
    nsub = jnp.where(s_idx < ns_used, (rows_in + SUB - 1) // SUB, 0).astype(I32)
    nb = (counts + SUB - 1) // SUB
    b_end = jnp.cumsum(nb)
    b_start = b_end - nb
    b_idx = jnp.arange(nb_max, dtype=I32)
    used = b_idx < b_end[-1]
    eb = jnp.minimum(jnp.searchsorted(b_end, b_idx, side='right'), N_EXPERTS - 1).astype(I32)
    ib = b_idx - b_start[eb]
    dst = jnp.where(used, sb_start[eb] * (SUPER // SUB) + ib, ns_max * (SUPER // SUB)).astype(I32)
    r = jnp.arange(SUB, dtype=I32)[None, :]
    within = ib[:, None] * SUB + r
    valid = jnp.logical_and(used[:, None], within < counts[eb][:, None])
    src = jnp.clip(starts[eb][:, None] + within, 0, a - 1)
    tok = jnp.where(valid, tok_sorted[src], 0).astype(I32).reshape(nb_max * SUB)
    return dict(pos=pos, se=se, nsub=nsub, xblk=s_eff.astype(I32), dst=dst, tok=tok)


def _row_copy(src_hbm, row, dst_buf, slot, r, sem):
    return pltpu.make_async_copy(src_hbm.at[pl.ds(row, 1), :], dst_buf.at[slot, pl.ds(r, 1), :], sem.at[slot])


def _gather_kernel(dst_ref, tok_ref, h_hbm, o_ref, buf, sem):
    b = pl.program_id(0)
    nb = pl.num_programs(0)

    def issue(blk, slot):
        def body(r, c):
            _row_copy(h_hbm, tok_ref[blk * SUB + r], buf, slot, r, sem).start()
            return c
        lax.fori_loop(0, SUB, body, 0, unroll=16)

    @pl.when(b == 0)
    def _first():
        issue(0, 0)

    @pl.when(b + 1 < nb)
    def _next():
        issue(b + 1, (b + 1) % 2)

    slot = b % 2

    def wait_body(r, c):
        _row_copy(h_hbm, 0, buf, slot, r, sem).wait()
        return c
    lax.fori_loop(0, SUB, wait_body, 0, unroll=16)
    o_ref[...] = buf[slot].astype(BF16)


def _gather(h2, route, m):
    _, ns_max, nb_max = _moe_sizes(m)
    rows = ns_max * SUPER + SUB
    return pl.pallas_call(
        _gather_kernel,
        grid_spec=pltpu.PrefetchScalarGridSpec(
            num_scalar_prefetch=2,
            grid=(nb_max,),
            in_specs=[pl.BlockSpec(memory_space=pl.ANY)],
            out_specs=pl.BlockSpec((SUB, D_MODEL), lambda b, dst, tok: (dst[b], 0)),
            scratch_shapes=[pltpu.VMEM((2, SUB, D_MODEL), F32), pltpu.SemaphoreType.DMA((2,))],
        ),
        out_shape=jax.ShapeDtypeStruct((rows, D_MODEL), BF16),
        compiler_params=_cparams(("arbitrary",)),
        name="moe_gather",
    )(route['dst'], route['tok'], h2)


def _moe_kernel(se_ref, nsub_ref, xb_ref, x_ref, wg_ref, wl_ref, bg_ref, bl_ref, wd_ref, bd_ref, o_ref,
                wg_s, wl_s, wd_s):
    s = pl.program_id(0)
    f = pl.program_id(1)
    n = nsub_ref[s]

    @pl.when(n > 0)
    def _work():
        wg_s[...] = wg_ref[0].astype(BF16)
        wl_s[...] = wl_ref[0].astype(BF16)
        wd_s[...] = wd_ref[0].astype(BF16)

        @pl.when(f == 0)
        def _init():
            o_ref[...] = jnp.broadcast_to(bd_ref[0], o_ref.shape)

        bg = bg_ref[0]
        bl = bl_ref[0]

        def expert_rows(r0, rows):
            xs = x_ref[pl.ds(r0, rows), :]
            hg = jnp.dot(xs, wg_s[...], preferred_element_type=F32) + bg
            hl = jnp.dot(xs, wl_s[...], preferred_element_type=F32) + bl
            hg = jnp.minimum(hg, SWIGLU_LIMIT)
            hl = jnp.clip(hl, -SWIGLU_LIMIT, SWIGLU_LIMIT)
            act = hg * _sigmoid(SWIGLU_ALPHA * hg) * (hl + 1.0)
            o_ref[pl.ds(r0, rows), :] += jnp.dot(act.astype(BF16), wd_s[...], preferred_element_type=F32)

        def wide_body(i, c):
            expert_rows(pl.multiple_of(i * (MOE_WIDE * SUB), MOE_WIDE * SUB), MOE_WIDE * SUB)
            return c
        n_wide = n // MOE_WIDE
        lax.fori_loop(0, n_wide, wide_body, 0)

        def single_body(i, c):
            expert_rows(pl.multiple_of(i * SUB, SUB), SUB)
            return c
        lax.fori_loop(n_wide * MOE_WIDE, n, single_body, 0)


def _moe(x_sorted, route, w_up, b_up, w_down, b_down, m):
    _, ns_max, _ = _moe_sizes(m)
    rows = x_sorted.shape[0]
    return pl.pallas_call(
        _moe_kernel,
        grid_spec=pltpu.PrefetchScalarGridSpec(
            num_scalar_prefetch=3,
            grid=(ns_max, NF),
            in_specs=[
                pl.BlockSpec((SUPER, D_MODEL), lambda s, f, se, ns, xb: (xb[s], 0)),
                pl.BlockSpec((1, D_MODEL, TF), lambda s, f, se, ns, xb: (se[s], 0, f)),
                pl.BlockSpec((1, D_MODEL, TF), lambda s, f, se, ns, xb: (se[s], 0, NF + f)),
                pl.BlockSpec((1, 1, TF), lambda s, f, se, ns, xb: (se[s], 0, f)),
                pl.BlockSpec((1, 1, TF), lambda s, f, se, ns, xb: (se[s], 0, NF + f)),
                pl.BlockSpec((1, TF, D_MODEL), lambda s, f, se, ns, xb: (se[s], f, 0)),
                pl.BlockSpec((1, 1, D_MODEL), lambda s, f, se, ns, xb: (se[s], 0, 0)),
            ],
            out_specs=pl.BlockSpec((SUPER, D_MODEL), lambda s, f, se, ns, xb: (xb[s], 0)),
            scratch_shapes=[pltpu.VMEM((D_MODEL, TF), BF16), pltpu.VMEM((D_MODEL, TF), BF16),
                            pltpu.VMEM((TF, D_MODEL), BF16)],
        ),
        out_shape=jax.ShapeDtypeStruct((rows, D_MODEL), F32),
        compiler_params=_cparams(("arbitrary", "arbitrary")),
        name="moe_experts",
    )(route['se'], route['nsub'], route['xblk'], x_sorted, w_up, w_up, b_up, b_up, w_down, b_down)


TC8 = 32


def _combine_kernel(pos_ref, y_hbm, x1_ref, gate_ref, g_ref, o_ref, buf, sem):
    b = pl.program_id(0)
    nb = pl.num_programs(0)

    def copies(blk, slot, wait):
        def body(r, c):
            for k in range(TOP_K):
                row = 0 if wait else pos_ref[(blk * TC8 + r) * TOP_K + k]
                cp = pltpu.make_async_copy(y_hbm.at[pl.ds(row, 1), :], buf.at[slot, k, pl.ds(r, 1), :], sem.at[slot])
                cp.wait() if wait else cp.start()
            return c
        lax.fori_loop(0, TC8, body, 0)

    @pl.when(b == 0)
    def _first():
        copies(0, 0, False)

    @pl.when(b + 1 < nb)
    def _next():
        copies(b + 1, (b + 1) % 2, False)

    slot = b % 2
    copies(b, slot, True)
    acc = x1_ref[...]
    gate = gate_ref[...]
    for k in range(TOP_K):
        acc = acc + gate[:, k:k + 1] * buf[slot, k]
    ms = jnp.mean(acc * acc, axis=-1, keepdims=True)
    o_ref[...] = acc * lax.rsqrt(ms + RMS_EPS) * g_ref[...]


def _combine(y_sorted, route, x1, gate, g_final):
    m = x1.shape[0]
    return pl.pallas_call(
        _combine_kernel,
        grid_spec=pltpu.PrefetchScalarGridSpec(
            num_scalar_prefetch=1,
            grid=(m // TC8,),
            in_specs=[
                pl.BlockSpec(memory_space=pl.ANY),
                pl.BlockSpec((TC8, D_MODEL), lambda b, pos: (b, 0)),
                pl.BlockSpec((TC8, ROUTER_LANES), lambda b, pos: (b, 0)),
                pl.BlockSpec((1, D_MODEL), lambda b, pos: (0, 0)),
            ],
            out_specs=pl.BlockSpec((TC8, D_MODEL), lambda b, pos: (b, 0)),
            scratch_shapes=[pltpu.VMEM((2, TOP_K, TC8, D_MODEL), F32), pltpu.SemaphoreType.DMA((2,))],
        ),
        out_shape=jax.ShapeDtypeStruct((m, D_MODEL), F32),
        compiler_params=_cparams(("arbitrary",)),
        name="moe_combine",
    )(route['pos'], y_sorted, x1, gate, g_final)


SSM_CHUNK = 8


def _rope_tables(seq, t_new, n_batch, n_dec):
    half = HEAD_DIM // 2
    inv = ROPE_THETA ** (-jnp.arange(half, dtype=F32) / half)
    pos = jnp.concatenate([jnp.tile(jnp.arange(seq), n_batch), jnp.tile(PAST_LEN + jnp.arange(t_new), n_dec)])
    ang = pos.astype(F32)[:, None] * inv[None, :]
    cos = jnp.cos(ang)
    sin = jnp.sin(ang)
    return jnp.tile(cos, (1, 4)), jnp.tile(jnp.concatenate([-sin, sin], axis=1), (1, 2))


def kernel(x_prompt, x_sample, state_ssm_re, state_ssm_im, cache_k, cache_v, attn_norm_g, w_in, b_in, ssm_a_re, ssm_a_im, ssm_log_dt, ssm_b_re, ssm_b_im, ssm_c_re, ssm_c_im, ssm_d, w_glu, b_glu, attn_sinks, w_branch_ssm, w_branch_attn, w_out, ffn_norm_g, w_router, b_router, w_up, b_up, w_down, b_down, final_norm_g):
    n_batch, seq, _ = x_prompt.shape
    n_dec, t_new, _ = x_sample.shape
    mp = n_batch * seq
    ms = n_dec * t_new
    m = mp + ms
    g, p, h = SSM_GROUPS, SSM_STATE, SSM_GROUP
    x = jnp.concatenate([x_prompt.reshape(mp, D_MODEL), x_sample.reshape(ms, D_MODEL)], axis=0)

    wi = w_in[0]
    sp = (0, D_SSM, D_SSM + D_ATTN, D_SSM + D_ATTN + D_KV, D_SSM + D_ATTN + 2 * D_KV,
          D_SSM + D_ATTN + 2 * D_KV + D_MODEL, N_IN)
    order = (4, 5, 0, 1, 2, 3)
    w_in_p = jnp.concatenate([wi[:, sp[i]:sp[i + 1]] for i in order], axis=1).astype(BF16)
    b_in_p = jnp.concatenate([b_in[0][sp[i]:sp[i + 1]] for i in order]).reshape(1, N_IN)
    cos_t, sin_t = _rope_tables(seq, t_new, n_batch, n_dec)
    proj = _inproj(x, attn_norm_g[0].reshape(1, D_MODEL), w_in_p, b_in_p, cos_t, sin_t)

    ssm_params = (ssm_a_re[0], ssm_a_im[0], ssm_log_dt[0], ssm_b_re[0], ssm_b_im[0], ssm_c_re[0], ssm_c_im[0], ssm_d[0])
    zeros = jnp.zeros((n_batch, g * p), F32)
    y_ssm, sp_re, sp_im = _ssm(proj, 0, m, n_batch, seq // SSM_CHUNK, SSM_CHUNK,
                               _ssm_operators(*ssm_params, SSM_CHUNK, BF16), zeros, zeros)
    y_ssm, ss_re, ss_im = _ssm(proj, mp, m, n_dec, 1, t_new, _ssm_operators(*ssm_params, t_new, F32),
                               state_ssm_re[0].reshape(n_dec, g * p), state_ssm_im[0].reshape(n_dec, g * p), y_prev=y_ssm)

    sinks = attn_sinks[0].reshape(1, N_HEADS)
    o_attn = _attn_prompt(proj, sinks, n_batch, seq, m)
    o_attn = _attn_sample(proj, mp, cache_k[0].reshape(n_dec * WINDOW, D_KV), cache_v[0].reshape(n_dec * WINDOW, D_KV),
                          sinks, n_dec, t_new, o_attn)

    wr = jnp.pad(w_router[0], ((0, 0), (0, ROUTER_LANES - N_EXPERTS)))
    br = jnp.pad(b_router[0], (0, ROUTER_LANES - N_EXPERTS), constant_values=NEG_INF).reshape(1, ROUTER_LANES)
    x1, h2, top_i, gate = _merge(
        y_ssm, o_attn, proj, x, w_glu[0].astype(BF16), b_glu[0].reshape(1, D_SSM), w_branch_ssm[0].astype(BF16),
        w_branch_attn[0].astype(BF16), w_out[0].astype(BF16), ffn_norm_g[0].reshape(1, D_MODEL), wr, br)

    route = _route(top_i[:, :TOP_K], m)
    x_sorted = _gather(h2, route, m)
    y_sorted = _moe(x_sorted, route, w_up[0], b_up[0].reshape(N_EXPERTS, 1, 2 * D_FF), w_down[0],
                    b_down[0].reshape(N_EXPERTS, 1, D_MODEL), m)
    y = _combine(y_sorted, route, x1, gate, final_norm_g.reshape(1, D_MODEL))

    y_prompt = y[:mp].reshape(n_batch, seq, D_MODEL)
    y_sample = y[mp:].reshape(n_dec, t_new, D_MODEL)
    to_state = lambda s: s.reshape(1, s.shape[0], g, p)
    k_all = proj[:, COL_K:COL_K + D_KV]
    v_all = proj[:, COL_V:COL_V + D_KV]
    kv_p = lambda a: a[:mp].reshape(n_batch, seq, N_KV_HEADS, HEAD_DIM)[:, -WINDOW:][None]
    kv_s = lambda a, c: jnp.concatenate([c[0][:, t_new:], a[mp:].reshape(n_dec, t_new, N_KV_HEADS, HEAD_DIM)], axis=1)[None]
    return (y_prompt, y_sample, to_state(sp_re), to_state(sp_im), kv_p(k_all), kv_p(v_all),
            to_state(ss_re), to_state(ss_im), kv_s(k_all, cache_k), kv_s(v_all, cache_v))
```

```python
import functools

import jax
import jax.numpy as jnp
from jax import lax
from jax.experimental import pallas as pl
from jax.experimental.pallas import tpu as pltpu

F32 = jnp.float32
BF16 = jnp.bfloat16
I32 = jnp.int32
HIGHEST = lax.Precision.HIGHEST

D_MODEL = 2048
RMS_EPS = 1e-5
SSM_GROUP = 16
SSM_GROUPS = 64
SSM_STATE = 64
D_SSM = 1024
HEAD_DIM = 64
N_HEADS = 16
N_KV_HEADS = 4
Q_PER_KV = 4
D_ATTN = 1024
D_KV = 256
WINDOW = 128
ROPE_THETA = 10000.0
NEG_INF = -1e30
PAST_LEN = 16384
N_IN = D_SSM + D_ATTN + 2 * D_KV + 2 * D_MODEL
N_EXPERTS = 32
TOP_K = 4
D_FF = 2048
SWIGLU_ALPHA = 1.702
SWIGLU_LIMIT = 7.0

COL_GS, COL_GA, COL_U, COL_Q, COL_K, COL_V = 0, 2048, 4096, 5120, 6144, 6400

VMEM_LIMIT = 56 * 1024 * 1024


def _sigmoid(x):
    return 1.0 / (1.0 + jnp.exp(-x))


def _cparams(sem, vmem=VMEM_LIMIT):
    return pltpu.CompilerParams(dimension_semantics=sem, vmem_limit_bytes=vmem)


TM1 = 512
TN1 = 512
_GATE_TILES = (COL_U // TN1)
_ROPE_TILE0 = COL_Q // TN1
_KV_TILE = COL_K // TN1


def _two_source(i, n_first, a_ref, b_ref):
    return jnp.where(i < n_first, a_ref[...], b_ref[...])


def _two_source_specs(tile, width, n_first):
    return [pl.BlockSpec((tile, width), lambda i, *_: (jnp.minimum(i, n_first - 1), 0)),
            pl.BlockSpec((tile, width), lambda i, *_: (jnp.maximum(i - n_first, 0), 0))]


def _inproj_kernel(xa_ref, xb_ref, g_ref, w_ref, b_ref, cos_ref, sin_ref, o_ref, h_scr, *, n_first):
    j = pl.program_id(1)

    @pl.when(j == 0)
    def _norm():
        x = _two_source(pl.program_id(0), n_first, xa_ref, xb_ref)
        ms = jnp.mean(x * x, axis=-1, keepdims=True)
        h_scr[...] = (x * lax.rsqrt(ms + RMS_EPS) * g_ref[...]).astype(BF16)

    acc = jnp.dot(h_scr[...], w_ref[...], preferred_element_type=F32) + b_ref[...]

    @pl.when(j < _GATE_TILES)
    def _gate():
        o_ref[...] = _sigmoid(acc)

    @pl.when(jnp.logical_and(j >= _GATE_TILES, j < _ROPE_TILE0))
    def _plain():
        o_ref[...] = acc

    @pl.when(j >= _ROPE_TILE0)
    def _rope():
        c = jnp.concatenate([cos_ref[...]] * (TN1 // 128), axis=1)
        s = jnp.concatenate([sin_ref[...]] * (TN1 // 128), axis=1)
        lane = lax.broadcasted_iota(I32, acc.shape, 1)
        first_half = (lane & (HEAD_DIM // 2)) == 0
        partner = jnp.where(first_half, pltpu.roll(acc, TN1 - HEAD_DIM // 2, 1), pltpu.roll(acc, HEAD_DIM // 2, 1))
        roped = acc * c + partner * s
        n_rot = jnp.where(j == _KV_TILE, D_KV, TN1)
        o_ref[...] = jnp.where(lane < n_rot, roped, acc)


def _inproj(xa, xb, g, w_bf, b, cos_t, sin_t):
    assert xa.shape[0] % TM1 == 0 and xb.shape[0] % TM1 == 0
    m = xa.shape[0] + xb.shape[0]
    n_first = xa.shape[0] // TM1
    return pl.pallas_call(
        functools.partial(_inproj_kernel, n_first=n_first),
        grid=(m // TM1, N_IN // TN1),
        in_specs=_two_source_specs(TM1, D_MODEL, n_first) + [
            pl.BlockSpec((1, D_MODEL), lambda i, j: (0, 0)),
            pl.BlockSpec((D_MODEL, TN1), lambda i, j: (0, j)),
            pl.BlockSpec((1, TN1), lambda i, j: (0, j)),
            pl.BlockSpec((TM1, 128), lambda i, j: (i, 0)),
            pl.BlockSpec((TM1, 128), lambda i, j: (i, 0)),
        ],
        out_specs=pl.BlockSpec((TM1, TN1), lambda i, j: (i, j)),
        out_shape=jax.ShapeDtypeStruct((m, N_IN), F32),
        scratch_shapes=[pltpu.VMEM((TM1, D_MODEL), BF16)],
        compiler_params=_cparams(("parallel", "arbitrary")),
        name="inproj",
    )(xa, xb, g, w_bf, b, cos_t, sin_t)


SSM_GB = 8


def _ssm_operators(a_re, a_im, log_dt, b_re, b_im, c_re, c_im, d_skip, chunk, w_dtype):
    a_re = a_re.astype(F32)
    a_im = a_im.astype(F32)
    dt = jnp.exp(log_dt.astype(F32))[:, None]
    mag = jnp.exp(a_re * dt)
    lb_re = mag * jnp.cos(a_im * dt)
    lb_im = mag * jnp.sin(a_im * dt)
    inv = 1.0 / (a_re * a_re + a_im * a_im)
    f_re = ((lb_re - 1.0) * a_re + lb_im * a_im) * inv
    f_im = (lb_im * a_re - (lb_re - 1.0) * a_im) * inv
    br = b_re.astype(F32)
    bi = b_im.astype(F32)
    bb_re = f_re[..., None] * br - f_im[..., None] * bi
    bb_im = f_re[..., None] * bi + f_im[..., None] * br
    k = jnp.arange(chunk + 1, dtype=F32)[:, None, None]
    pmag = jnp.exp(a_re * dt * k)
    pw_re = pmag * jnp.cos(a_im * dt * k)
    pw_im = pmag * jnp.sin(a_im * dt * k)
    g, p, h = bb_re.shape
    nlb = g // SSM_GB
    eye = jnp.eye(SSM_GB, dtype=F32)
    pb_re = pw_re[:chunk, :, :, None] * bb_re[None] - pw_im[:chunk, :, :, None] * bb_im[None]
    pb_im = pw_re[:chunk, :, :, None] * bb_im[None] + pw_im[:chunk, :, :, None] * bb_re[None]

    def to_state(pb):
        x = pb[::-1].reshape(chunk, nlb, SSM_GB, p, h)
        return jnp.einsum('jbgph,gk->bjghkp', x, eye).reshape(nlb, chunk * 128, SSM_GB * p).astype(w_dtype)
    cr = c_re.astype(F32)
    ci = c_im.astype(F32)
    crt = jnp.transpose(cr, (0, 2, 1))[None]
    cit = jnp.transpose(ci, (0, 2, 1))[None]
    e_re = pw_re[1:, :, :, None]
    e_im = pw_im[1:, :, :, None]

    def from_state(v):
        x = v.reshape(chunk, nlb, SSM_GB, p, h)
        return jnp.einsum('tbgph,gk->bgptkh', x, eye).reshape(nlb, SSM_GB * p, chunk * 128).astype(BF16)
    lagk = (jnp.einsum('gop,lgph->lgoh', cr, pb_re, precision=HIGHEST)
            - jnp.einsum('gop,lgph->lgoh', ci, pb_im, precision=HIGHEST))
    lag_blocks = jnp.einsum('lbgoi,gk->blgiko', lagk.reshape(chunk, nlb, SSM_GB, h, h), eye)
    lag_blocks = lag_blocks.reshape(nlb, chunk, 128, 128).astype(BF16)
    d_row = jnp.tile(d_skip.astype(F32).reshape(nlb, 1, 128), (1, 1, chunk))
    return dict(lag_blocks=lag_blocks, w_re=to_state(pb_re), w_im=to_state(pb_im),
                v_re=from_state(crt * e_re - cit * e_im), v_im=from_state(-(crt * e_im + cit * e_re)),
                lbl_re=pw_re[chunk].reshape(1, g * p), lbl_im=pw_im[chunk].reshape(1, g * p), d_row=d_row)


def _ssm_kernel(*refs, chunk, n_chunks, n_seq, aliased):
    (u_ref, t_ref, wr_ref, wi_ref, vr_ref, vi_ref, lr_ref, li_ref, d_ref, s0r_ref, s0i_ref) = refs[:11]
    y_ref, sr_ref, si_ref, z_scr, t_scr, lor_scr, loi_scr, pr_scr, pi_scr = refs[11 + aliased:]
    rows = n_seq * n_chunks
    for j in range(chunk):
        for t in range(chunk):
            blk = t_ref[0, t - j] if t >= j else jnp.zeros((128, 128), BF16)
            t_scr[j * 128:(j + 1) * 128, t * 128:(t + 1) * 128] = blk
    for j in range(chunk):
        z_scr[:, j * 128:(j + 1) * 128] = u_ref[pl.ds(j, rows, stride=chunk), :]
    z = z_scr[...]
    zw = z.astype(wr_ref.dtype)
    prec = HIGHEST if wr_ref.dtype == F32 else None
    lor = jnp.dot(zw, wr_ref[0], preferred_element_type=F32, precision=prec)
    loi = jnp.dot(zw, wi_ref[0], preferred_element_type=F32, precision=prec)
    lr = lr_ref[...]
    li = li_ref[...]
    sr = s0r_ref[...]
    si = s0i_ref[...]
    if n_chunks == 1:
        pr, pi = sr, si
        sr, si = lr * sr - li * si + lor, lr * si + li * sr + loi
    else:
        pieces = range(lor.shape[1] // 128)
        cut = lambda a: tuple(a[:, k * 128:(k + 1) * 128] for k in pieces)
        for k in pieces:
            lor_scr[k] = lor[:, k * 128:(k + 1) * 128]
            loi_scr[k] = loi[:, k * 128:(k + 1) * 128]
        lrs, lis = cut(lr), cut(li)

        def step(c, carry):
            idx = pl.ds(c, n_seq, stride=n_chunks)
            nxt = []
            for k, (a, b) in enumerate(zip(*carry)):
                pr_scr[k, idx, :] = a
                pi_scr[k, idx, :] = b
                nxt.append((lrs[k] * a - lis[k] * b + lor_scr[k, idx, :], lrs[k] * b + lis[k] * a + loi_scr[k, idx, :]))
            return tuple(n[0] for n in nxt), tuple(n[1] for n in nxt)
        srs, sis = lax.fori_loop(0, n_chunks, step, (cut(sr), cut(si)), unroll=8)
        sr, si = jnp.concatenate(srs, axis=1), jnp.concatenate(sis, axis=1)
        pr = jnp.concatenate([pr_scr[k] for k in pieces], axis=1)
        pi = jnp.concatenate([pi_scr[k] for k in pieces], axis=1)
    sr_ref[...] = sr
    si_ref[...] = si
    y = jnp.dot(z.astype(BF16), t_scr[...], preferred_element_type=F32)
    y += jnp.dot(pr.astype(BF16), vr_ref[0], preferred_element_type=F32)
    y += jnp.dot(pi.astype(BF16), vi_ref[0], preferred_element_type=F32)
    y += z * d_ref[0]
    for j in range(chunk):
        y_ref[pl.ds(j, rows, stride=chunk), :] = y[:, j * 128:(j + 1) * 128]


def _ssm(proj, row0, m_out, n_seq, n_chunks, chunk, ops, s0_re, s0_im, y_prev=None):
    rows = n_seq * n_chunks
    tok = rows * chunk
    nlb = SSM_GROUPS // SSM_GB
    width = chunk * 128
    sw = SSM_GB * SSM_STATE
    blk0 = row0 // tok
    per_lb = lambda a, b: pl.BlockSpec((1, a, b), lambda i: (i, 0, 0))
    lanes = lambda r, w: pl.BlockSpec((r, w), lambda i: (0, i))
    in_specs = [pl.BlockSpec((tok, 128), lambda i: (blk0, COL_U // 128 + i)),
                pl.BlockSpec((1, chunk, 128, 128), lambda i: (i, 0, 0, 0)),
                per_lb(width, sw), per_lb(width, sw), per_lb(sw, width), per_lb(sw, width),
                lanes(1, sw), lanes(1, sw), per_lb(1, width), lanes(n_seq, sw), lanes(n_seq, sw)]
    args = [proj, ops['lag_blocks'], ops['w_re'], ops['w_im'], ops['v_re'], ops['v_im'],
            ops['lbl_re'], ops['lbl_im'], ops['d_row'], s0_re, s0_im]
    aliases = {}
    if y_prev is not None:
        in_specs.append(pl.BlockSpec(memory_space=pl.ANY))
        args.append(y_prev)
        aliases = {len(args) - 1: 0}
    state = jax.ShapeDtypeStruct((n_seq, SSM_GROUPS * SSM_STATE), F32)
    return pl.pallas_call(
        functools.partial(_ssm_kernel, chunk=chunk, n_chunks=n_chunks, n_seq=n_seq, aliased=int(y_prev is not None)),
        grid=(nlb,),
        in_specs=in_specs,
        out_specs=[pl.BlockSpec((tok, 128), lambda i: (blk0, i)), lanes(n_seq, sw), lanes(n_seq, sw)],
        out_shape=[jax.ShapeDtypeStruct((m_out, D_SSM), F32), state, state],
        scratch_shapes=[pltpu.VMEM((rows, width), F32), pltpu.VMEM((width, width), BF16)]
        + [pltpu.VMEM((sw // 128, rows, 128), F32)] * 4,
        input_output_aliases=aliases,
        compiler_params=_cparams(("parallel",)),
        name=f"ssm_c{n_chunks}",
    )(*args)


def _softmax_sink_pv(parts, sink):
    masked = [jnp.where(valid, s, NEG_INF) for s, valid, _ in parts]
    m = sink
    for s in masked:
        m = jnp.maximum(m, jnp.max(s, axis=-1, keepdims=True))
    denom = jnp.exp(sink - m)
    out = None
    for s, (_, _, v) in zip(masked, parts):
        e = jnp.exp(s - m)
        denom = denom + jnp.sum(e, axis=-1, keepdims=True)
        pv = jnp.dot(e.astype(BF16), v, preferred_element_type=F32)
        out = pv if out is None else out + pv
    return out / denom


def _qk(q, k):
    return lax.dot_general(q, k, (((1,), (1,)), ((), ())), preferred_element_type=F32)


def _attn_prompt_kernel(q_ref, kp_ref, kc_ref, vp_ref, vc_ref, sink_ref, o_ref):
    qb = pl.program_id(1)
    rows = Q_PER_KV * WINDOW
    t = lax.broadcasted_iota(I32, (rows, 2 * WINDOW), 0) & (WINDOW - 1)
    col = lax.broadcasted_iota(I32, (rows, 2 * WINDOW), 1)
    valid = jnp.logical_or(jnp.logical_and(jnp.logical_and(col < WINDOW, col > t), qb > 0),
                           jnp.logical_and(col >= WINDOW, (col - WINDOW) <= t))
    scale = HEAD_DIM ** -0.5
    for kv in range(N_KV_HEADS):
        lanes = slice(kv * HEAD_DIM, (kv + 1) * HEAD_DIM)
        k = jnp.concatenate([kp_ref[:, lanes], kc_ref[:, lanes]], axis=0).astype(BF16)
        v = jnp.concatenate([vp_ref[:, lanes], vc_ref[:, lanes]], axis=0).astype(BF16)
        heads = [kv * Q_PER_KV + i for i in range(Q_PER_KV)]
        q = jnp.concatenate([q_ref[:, h * HEAD_DIM:(h + 1) * HEAD_DIM] for h in heads], axis=0).astype(BF16)
        sink = jnp.concatenate(
            [jnp.broadcast_to(sink_ref[0:1, h:h + 1], (WINDOW, 1)) for h in heads], axis=0)
        s = _qk(q, k) * scale
        o = _softmax_sink_pv([(s, valid, v)], sink)
        for i, h in enumerate(heads):
            o_ref[:, h * HEAD_DIM:(h + 1) * HEAD_DIM] = o[i * WINDOW:(i + 1) * WINDOW, :]


def _attn_prompt(proj, sinks, n_batch, seq, m_out):
    nb = seq // WINDOW
    cq, ck, cv = COL_Q // D_ATTN, COL_K // D_KV, COL_V // D_KV
    cur = lambda col: (lambda n, b: (n * nb + b, col))
    prev = lambda col: (lambda n, b: (n * nb + jnp.maximum(b - 1, 0), col))
    return pl.pallas_call(
        _attn_prompt_kernel,
        grid=(n_batch, nb),
        in_specs=[
            pl.BlockSpec((WINDOW, D_ATTN), cur(cq)),
            pl.BlockSpec((WINDOW, D_KV), prev(ck)),
            pl.BlockSpec((WINDOW, D_KV), cur(ck)),
            pl.BlockSpec((WINDOW, D_KV), prev(cv)),
            pl.BlockSpec((WINDOW, D_KV), cur(cv)),
            pl.BlockSpec((1, N_HEADS), lambda n, b: (0, 0)),
        ],
        out_specs=pl.BlockSpec((WINDOW, D_ATTN), lambda n, b: (n * nb + b, 0)),
        out_shape=jax.ShapeDtypeStruct((m_out, D_ATTN), F32),
        compiler_params=_cparams(("parallel", "arbitrary")),
        name="attn_prompt",
    )(proj, proj, proj, proj, proj, sinks)


SAMPLE_GB = 8


def _attn_sample_kernel(q_ref, kn_ref, vn_ref, kc_ref, vc_ref, sink_ref, o_ref, *, t_new):
    tok = SAMPLE_GB * t_new
    rows = Q_PER_KV * tok
    ncache = SAMPLE_GB * WINDOW
    r_c = lax.broadcasted_iota(I32, (rows, ncache), 0)
    c_c = lax.broadcasted_iota(I32, (rows, ncache), 1)
    rb_c = (r_c % tok) // t_new
    rt_c = r_c % t_new
    valid_c = jnp.logical_and(rb_c == c_c // WINDOW, (c_c % WINDOW) > rt_c)
    r_n = lax.broadcasted_iota(I32, (rows, tok), 0)
    c_n = lax.broadcasted_iota(I32, (rows, tok), 1)
    valid_n = jnp.logical_and((r_n % tok) // t_new == c_n // t_new, (c_n % t_new) <= (r_n % t_new))
    scale = HEAD_DIM ** -0.5
    for kv in range(N_KV_HEADS):
        lanes = slice(kv * HEAD_DIM, (kv + 1) * HEAD_DIM)
        kc = kc_ref[:, lanes].astype(BF16)
        vc = vc_ref[:, lanes].astype(BF16)
        kn = kn_ref[:, lanes].astype(BF16)
        vn = vn_ref[:, lanes].astype(BF16)
        heads = [kv * Q_PER_KV + i for i in range(Q_PER_KV)]
        q = jnp.concatenate([q_ref[:, h * HEAD_DIM:(h + 1) * HEAD_DIM] for h in heads], axis=0).astype(BF16)
        sink = jnp.concatenate(
            [jnp.broadcast_to(sink_ref[0:1, h:h + 1], (tok, 1)) for h in heads], axis=0)
        o = _softmax_sink_pv([(_qk(q, kc) * scale, valid_c, vc), (_qk(q, kn) * scale, valid_n, vn)], sink)
        for i, h in enumerate(heads):
            o_ref[:, h * HEAD_DIM:(h + 1) * HEAD_DIM] = o[i * tok:(i + 1) * tok, :]


def _attn_sample_aliased(q_ref, kn_ref, vn_ref, kc_ref, vc_ref, sink_ref, o_prev, o_ref, *, t_new):
    del o_prev
    _attn_sample_kernel(q_ref, kn_ref, vn_ref, kc_ref, vc_ref, sink_ref, o_ref, t_new=t_new)


def _attn_sample(proj, row0, cache_k, cache_v, sinks, n_batch, t_new, o_prev):
    tok = SAMPLE_GB * t_new
    blk0 = row0 // tok
    cq, ck, cv = COL_Q // D_ATTN, COL_K // D_KV, COL_V // D_KV
    new = lambda col: (lambda g: (blk0 + g, col))
    return pl.pallas_call(
        functools.partial(_attn_sample_aliased, t_new=t_new),
        grid=(n_batch // SAMPLE_GB,),
        in_specs=[
            pl.BlockSpec((tok, D_ATTN), new(cq)),
            pl.BlockSpec((tok, D_KV), new(ck)),
            pl.BlockSpec((tok, D_KV), new(cv)),
            pl.BlockSpec((SAMPLE_GB * WINDOW, D_KV), lambda g: (g, 0)),
            pl.BlockSpec((SAMPLE_GB * WINDOW, D_KV), lambda g: (g, 0)),
            pl.BlockSpec((1, N_HEADS), lambda g: (0, 0)),
            pl.BlockSpec(memory_space=pl.ANY),
        ],
        out_specs=pl.BlockSpec((tok, D_ATTN), lambda g: (blk0 + g, 0)),
        out_shape=jax.ShapeDtypeStruct(o_prev.shape, F32),
        input_output_aliases={6: 0},
        compiler_params=_cparams(("parallel",)),
        name="attn_sample",
    )(proj, proj, proj, cache_k, cache_v, sinks, o_prev)


TM4 = 256
ROUTER_LANES = 128


def _gelu_tanh(x):
    return 0.5 * x * (1.0 + jnp.tanh(0.7978845608028654 * (x + 0.044715 * (x * x * x))))


def _merge_kernel(y_ref, o_ref, gs_ref, ga_ref, xa_ref, xb_ref, wglu_ref, bglu_ref, wbs_ref, wba_ref, wout_ref,
                  g2_ref, wr_ref, br_ref, x1_ref, h2_ref, ti_ref, tg_ref, rank_ref, cnt_ref, seen_scr, *, n_first):
    @pl.when(pl.program_id(0) == 0)
    def _reset():
        seen_scr[...] = jnp.zeros(seen_scr.shape, F32)

    y1 = _gelu_tanh(y_ref[...])
    z = jnp.dot(y1.astype(BF16), wglu_ref[...], preferred_element_type=F32) + bglu_ref[...]
    y2 = y1 * _sigmoid(z)
    bs = jnp.dot(y2.astype(BF16), wbs_ref[...], preferred_element_type=F32)
    ba = jnp.dot(o_ref[...].astype(BF16), wba_ref[...], preferred_element_type=F32)
    merged = gs_ref[...] * bs + ga_ref[...] * ba
    x = _two_source(pl.program_id(0), n_first, xa_ref, xb_ref)
    x1 = x + jnp.dot(merged.astype(BF16), wout_ref[...], preferred_element_type=F32)
    x1_ref[...] = x1
    ms = jnp.mean(x1 * x1, axis=-1, keepdims=True)
    h2 = x1 * lax.rsqrt(ms + RMS_EPS) * g2_ref[...]
    h2_ref[...] = h2
    logits = jnp.dot(h2, wr_ref[...], preferred_element_type=F32, precision=HIGHEST) + br_ref[...]
    lane = lax.broadcasted_iota(I32, logits.shape, 1)
    cur = logits
    top_i = jnp.zeros(logits.shape, I32)
    top_v = jnp.zeros(logits.shape, F32)
    chosen = []
    for k in range(TOP_K):
        mx = jnp.max(cur, axis=-1, keepdims=True)
        idx = jnp.min(jnp.where(cur == mx, lane, ROUTER_LANES), axis=-1, keepdims=True)
        top_i = jnp.where(lane == k, idx, top_i)
        top_v = jnp.where(lane == k, mx, top_v)
        chosen.append(lane == idx)
        cur = jnp.where(chosen[-1], -jnp.inf, cur)
    v0 = jnp.max(jnp.where(lane < TOP_K, top_v, -jnp.inf), axis=-1, keepdims=True)
    e = jnp.where(lane < TOP_K, jnp.exp(top_v - v0), 0.0)
    ti_ref[...] = top_i
    tg_ref[...] = e / jnp.sum(e, axis=-1, keepdims=True)
    hits = sum(jnp.where(c, 1.0, 0.0) for c in chosen)
    r = lax.broadcasted_iota(I32, (TM4, TM4), 0)
    c = lax.broadcasted_iota(I32, (TM4, TM4), 1)
    before = jnp.dot(jnp.where(c < r, 1.0, 0.0).astype(BF16), hits.astype(BF16), preferred_element_type=F32)
    before = before + seen_scr[...]
    rank = jnp.zeros(logits.shape, F32)
    for k in range(TOP_K):
        rank = jnp.where(lane == k, jnp.sum(jnp.where(chosen[k], before, 0.0), axis=-1, keepdims=True), rank)
    rank_ref[...] = rank.astype(I32)
    seen_scr[...] = seen_scr[...] + jnp.sum(hits, axis=0, keepdims=True)
    cnt_ref[...] = seen_scr[...].astype(I32)


def _merge(y_ssm, o_attn, proj, xa, xb, wglu, bglu, wbs, wba, wout, g2, wr, br):
    assert xa.shape[0] % TM4 == 0 and xb.shape[0] % TM4 == 0
    m = xa.shape[0] + xb.shape[0]
    n_first = xa.shape[0] // TM4
    row = lambda w: pl.BlockSpec((TM4, w), lambda i: (i, 0))
    const = lambda a, b: pl.BlockSpec((a, b), lambda i: (0, 0), pipeline_mode=pl.Buffered(1))
    return pl.pallas_call(
        functools.partial(_merge_kernel, n_first=n_first),
        grid=(m // TM4,),
        in_specs=[
            row(D_SSM), row(D_ATTN),
            pl.BlockSpec((TM4, D_MODEL), lambda i: (i, COL_GS // D_MODEL)),
            pl.BlockSpec((TM4, D_MODEL), lambda i: (i, COL_GA // D_MODEL)),
            *_two_source_specs(TM4, D_MODEL, n_first),
            const(D_SSM, D_SSM), const(1, D_SSM), const(D_SSM, D_MODEL), const(D_ATTN, D_MODEL),
            const(D_MODEL, D_MODEL), const(1, D_MODEL), const(D_MODEL, ROUTER_LANES), const(1, ROUTER_LANES),
        ],
        out_specs=[row(D_MODEL), row(D_MODEL), row(ROUTER_LANES), row(ROUTER_LANES), row(ROUTER_LANES),
                   pl.BlockSpec((1, ROUTER_LANES), lambda i: (0, 0))],
        out_shape=[jax.ShapeDtypeStruct((m, D_MODEL), F32), jax.ShapeDtypeStruct((m, D_MODEL), F32),
                   jax.ShapeDtypeStruct((m, ROUTER_LANES), I32), jax.ShapeDtypeStruct((m, ROUTER_LANES), F32),
                   jax.ShapeDtypeStruct((m, ROUTER_LANES), I32), jax.ShapeDtypeStruct((1, ROUTER_LANES), I32)],
        scratch_shapes=[pltpu.VMEM((1, ROUTER_LANES), F32)],
        compiler_params=_cparams(("arbitrary",)),
        name="merge",
    )(y_ssm, o_attn, proj, proj, xa, xb, wglu, bglu, wbs, wba, wout, g2, wr, br)


SUB = 128
SUPER = 1280
TF = 256
NF = D_FF // TF
MOE_WIDE = 8


def _moe_sizes(m):
    a = m * TOP_K
    ns_max = N_EXPERTS + a // SUPER
    nb_max = N_EXPERTS + a // SUB
    return a, ns_max, nb_max


def _route(top_i, rank, counts, m):
    a, ns_max, nb_max = _moe_sizes(m)
    experts = jnp.arange(N_EXPERTS, dtype=I32)
    nsb = (counts + SUPER - 1) // SUPER
    sb_end = jnp.cumsum(nsb)
    sb_start = sb_end - nsb
    base = jnp.sum(jnp.where(top_i[:, :, None] == experts, sb_start * SUPER, 0), axis=-1)
    pos = (base + rank).reshape(a).astype(I32)
    n_slots = ns_max * SUPER + SUB
    tok_of_slot = jnp.zeros((n_slots,), I32).at[pos].set(jnp.arange(a, dtype=I32) // TOP_K, unique_indices=True)
    s_idx = jnp.arange(ns_max, dtype=I32)
    ns_used = sb_end[-1]
    s_eff = jnp.minimum(s_idx, ns_used - 1)
    se = jnp.minimum(jnp.sum((s_eff[:, None] >= sb_end[None, :]).astype(I32), axis=1), N_EXPERTS - 1)
    of_se = lambda v: jnp.sum(jnp.where(se[:, None] == experts, v, 0), axis=-1)
    rows_in = jnp.clip(of_se(counts) - (s_eff - of_se(sb_start)) * SUPER, 0, SUPER)
    nsub = jnp.where(s_idx < ns_used, (rows_in + SUB - 1) // SUB, 0).astype(I32)
    nb = (counts + SUB - 1) // SUB
    b_end = jnp.cumsum(nb)
    b_idx = jnp.arange(nb_max, dtype=I32)
    eb = jnp.minimum(jnp.sum((b_idx[:, None] >= b_end[None, :]).astype(I32), axis=1), N_EXPERTS - 1)
    of_eb = lambda v: jnp.sum(jnp.where(eb[:, None] == experts, v, 0), axis=-1)
    ib = b_idx - of_eb(b_end - nb)
    dst = jnp.where(b_idx < b_end[-1], of_eb(sb_start) * (SUPER // SUB) + ib, ns_max * (SUPER // SUB)).astype(I32)
    tok = tok_of_slot.reshape(n_slots // SUB, SUB)[dst].reshape(nb_max * SUB)
    return dict(pos=pos, se=se.astype(I32), nsub=nsub, xblk=s_eff.astype(I32), dst=dst, tok=tok)


def _row_copy(src_hbm, row, dst_buf, slot, r, sem):
    return pltpu.make_async_copy(src_hbm.at[pl.ds(row, 1), :], dst_buf.at[slot, pl.ds(r, 1), :], sem.at[slot])


def _gather_kernel(dst_ref, tok_ref, h_hbm, o_ref, buf, sem):
    b = pl.program_id(0)
    nb = pl.num_programs(0)

    def issue(blk, slot):
        def body(r, c):
            _row_copy(h_hbm, tok_ref[blk * SUB + r], buf, slot, r, sem).start()
            return c
        lax.fori_loop(0, SUB, body, 0, unroll=16)

    @pl.when(b == 0)
    def _first():
        issue(0, 0)

    @pl.when(b + 1 < nb)
    def _next():
        issue(b + 1, (b + 1) % 2)

    slot = b % 2

    def wait_body(r, c):
        _row_copy(h_hbm, 0, buf, slot, r, sem).wait()
        return c
    lax.fori_loop(0, SUB, wait_body, 0, unroll=16)
    o_ref[...] = buf[slot].astype(BF16)


def _gather(h2, route, m):
    _, ns_max, nb_max = _moe_sizes(m)
    rows = ns_max * SUPER + SUB
    return pl.pallas_call(
        _gather_kernel,
        grid_spec=pltpu.PrefetchScalarGridSpec(
            num_scalar_prefetch=2,
            grid=(nb_max,),
            in_specs=[pl.BlockSpec(memory_space=pl.ANY)],
            out_specs=pl.BlockSpec((SUB, D_MODEL), lambda b, dst, tok: (dst[b], 0)),
            scratch_shapes=[pltpu.VMEM((2, SUB, D_MODEL), F32), pltpu.SemaphoreType.DMA((2,))],
        ),
        out_shape=jax.ShapeDtypeStruct((rows, D_MODEL), BF16),
        compiler_params=_cparams(("arbitrary",)),
        name="moe_gather",
    )(route['dst'], route['tok'], h2)


def _moe_kernel(se_ref, nsub_ref, xb_ref, x_ref, wg_ref, wl_ref, bg_ref, bl_ref, wd_ref, bd_ref, o_ref,
                wg_s, wl_s, wd_s):
    s = pl.program_id(0)
    f = pl.program_id(1)
    n = nsub_ref[s]

    @pl.when(n > 0)
    def _work():
        wg_s[...] = wg_ref[0].astype(BF16)
        wl_s[...] = wl_ref[0].astype(BF16)
        wd_s[...] = wd_ref[0].astype(BF16)

        @pl.when(f == 0)
        def _init():
            o_ref[...] = jnp.broadcast_to(bd_ref[0], o_ref.shape)

        bg = bg_ref[0]
        bl = bl_ref[0]

        def expert_rows(r0, rows):
            xs = x_ref[pl.ds(r0, rows), :]
            hg = jnp.dot(xs, wg_s[...], preferred_element_type=F32) + bg
            hl = jnp.dot(xs, wl_s[...], preferred_element_type=F32) + bl
            hg = jnp.minimum(hg, SWIGLU_LIMIT)
            hl = jnp.clip(hl, -SWIGLU_LIMIT, SWIGLU_LIMIT)
            act = hg * _sigmoid(SWIGLU_ALPHA * hg) * (hl + 1.0)
            o_ref[pl.ds(r0, rows), :] += jnp.dot(act.astype(BF16), wd_s[...], preferred_element_type=F32)

        def wide_body(i, c):
            expert_rows(pl.multiple_of(i * (MOE_WIDE * SUB), MOE_WIDE * SUB), MOE_WIDE * SUB)
            return c
        n_wide = n // MOE_WIDE
        lax.fori_loop(0, n_wide, wide_body, 0)

        def single_body(i, c):
            expert_rows(pl.multiple_of(i * SUB, SUB), SUB)
            return c
        lax.fori_loop(n_wide * MOE_WIDE, n, single_body, 0)


def _moe(x_sorted, route, w_up, b_up, w_down, b_down, m):
    _, ns_max, _ = _moe_sizes(m)
    rows = x_sorted.shape[0]
    return pl.pallas_call(
        _moe_kernel,
        grid_spec=pltpu.PrefetchScalarGridSpec(
            num_scalar_prefetch=3,
            grid=(ns_max, NF),
            in_specs=[
                pl.BlockSpec((SUPER, D_MODEL), lambda s, f, se, ns, xb: (xb[s], 0)),
                pl.BlockSpec((1, D_MODEL, TF), lambda s, f, se, ns, xb: (se[s], 0, f)),
                pl.BlockSpec((1, D_MODEL, TF), lambda s, f, se, ns, xb: (se[s], 0, NF + f)),
                pl.BlockSpec((1, 1, TF), lambda s, f, se, ns, xb: (se[s], 0, f)),
                pl.BlockSpec((1, 1, TF), lambda s, f, se, ns, xb: (se[s], 0, NF + f)),
                pl.BlockSpec((1, TF, D_MODEL), lambda s, f, se, ns, xb: (se[s], f, 0)),
                pl.BlockSpec((1, 1, D_MODEL), lambda s, f, se, ns, xb: (se[s], 0, 0)),
            ],
            out_specs=pl.BlockSpec((SUPER, D_MODEL), lambda s, f, se, ns, xb: (xb[s], 0)),
            scratch_shapes=[pltpu.VMEM((D_MODEL, TF), BF16), pltpu.VMEM((D_MODEL, TF), BF16),
                            pltpu.VMEM((TF, D_MODEL), BF16)],
        ),
        out_shape=jax.ShapeDtypeStruct((rows, D_MODEL), F32),
        compiler_params=_cparams(("arbitrary", "arbitrary")),
        name="moe_experts",
    )(route['se'], route['nsub'], route['xblk'], x_sorted, w_up, w_up, b_up, b_up, w_down, b_down)


TC8 = 32


def _combine_kernel(pos_ref, y_hbm, x1_ref, gate_ref, g_ref, o1_ref, o2_ref, buf, sem, *, n_first):
    b = pl.program_id(0)
    nb = pl.num_programs(0)

    def copies(blk, slot, wait):
        def body(r, c):
            for k in range(TOP_K):
                row = 0 if wait else pos_ref[(blk * TC8 + r) * TOP_K + k]
                cp = pltpu.make_async_copy(y_hbm.at[pl.ds(row, 1), :], buf.at[slot, k, pl.ds(r, 1), :], sem.at[slot])
                cp.wait() if wait else cp.start()
            return c
        lax.fori_loop(0, TC8, body, 0)

    @pl.when(b == 0)
    def _first():
        copies(0, 0, False)

    @pl.when(b + 1 < nb)
    def _next():
        copies(b + 1, (b + 1) % 2, False)

    slot = b % 2
    copies(b, slot, True)
    acc = x1_ref[...]
    gate = gate_ref[...]
    for k in range(TOP_K):
        acc = acc + gate[:, k:k + 1] * buf[slot, k]
    ms = jnp.mean(acc * acc, axis=-1, keepdims=True)
    res = acc * lax.rsqrt(ms + RMS_EPS) * g_ref[...]

    @pl.when(b < n_first)
    def _to_first():
        o1_ref[...] = res

    @pl.when(b >= n_first)
    def _to_second():
        o2_ref[...] = res


def _combine(y_sorted, route, x1, gate, g_final, m_first):
    m = x1.shape[0]
    n_first = m_first // TC8
    return pl.pallas_call(
        functools.partial(_combine_kernel, n_first=n_first),
        grid_spec=pltpu.PrefetchScalarGridSpec(
            num_scalar_prefetch=1,
            grid=(m // TC8,),
            in_specs=[
                pl.BlockSpec(memory_space=pl.ANY),
                pl.BlockSpec((TC8, D_MODEL), lambda b, pos: (b, 0)),
                pl.BlockSpec((TC8, ROUTER_LANES), lambda b, pos: (b, 0)),
                pl.BlockSpec((1, D_MODEL), lambda b, pos: (0, 0)),
            ],
            out_specs=[pl.BlockSpec((TC8, D_MODEL), lambda b, pos: (jnp.minimum(b, n_first - 1), 0)),
                       pl.BlockSpec((TC8, D_MODEL), lambda b, pos: (jnp.maximum(b - n_first, 0), 0))],
            scratch_shapes=[pltpu.VMEM((2, TOP_K, TC8, D_MODEL), F32), pltpu.SemaphoreType.DMA((2,))],
        ),
        out_shape=[jax.ShapeDtypeStruct((m_first, D_MODEL), F32), jax.ShapeDtypeStruct((m - m_first, D_MODEL), F32)],
        compiler_params=_cparams(("arbitrary",)),
        name="moe_combine",
    )(route['pos'], y_sorted, x1, gate, g_final)


SSM_CHUNK = 8


def _rope_tables(seq, t_new, n_batch, n_dec):
    half = HEAD_DIM // 2
    inv = ROPE_THETA ** (-jnp.arange(half, dtype=F32) / half)
    pos = jnp.concatenate([jnp.tile(jnp.arange(seq), n_batch), jnp.tile(PAST_LEN + jnp.arange(t_new), n_dec)])
    ang = pos.astype(F32)[:, None] * inv[None, :]
    cos = jnp.cos(ang)
    sin = jnp.sin(ang)
    return jnp.tile(cos, (1, 4)), jnp.tile(jnp.concatenate([-sin, sin], axis=1), (1, 2))


def kernel(x_prompt, x_sample, state_ssm_re, state_ssm_im, cache_k, cache_v, attn_norm_g, w_in, b_in, ssm_a_re, ssm_a_im, ssm_log_dt, ssm_b_re, ssm_b_im, ssm_c_re, ssm_c_im, ssm_d, w_glu, b_glu, attn_sinks, w_branch_ssm, w_branch_attn, w_out, ffn_norm_g, w_router, b_router, w_up, b_up, w_down, b_down, final_norm_g):
    n_batch, seq, _ = x_prompt.shape
    n_dec, t_new, _ = x_sample.shape
    mp = n_batch * seq
    ms = n_dec * t_new
    m = mp + ms
    g, p, h = SSM_GROUPS, SSM_STATE, SSM_GROUP
    xp = x_prompt.reshape(mp, D_MODEL)
    xs = x_sample.reshape(ms, D_MODEL)

    wi = w_in[0]
    sp = (0, D_SSM, D_SSM + D_ATTN, D_SSM + D_ATTN + D_KV, D_SSM + D_ATTN + 2 * D_KV,
          D_SSM + D_ATTN + 2 * D_KV + D_MODEL, N_IN)
    order = (4, 5, 0, 1, 2, 3)
    w_in_p = jnp.concatenate([wi[:, sp[i]:sp[i + 1]] for i in order], axis=1).astype(BF16)
    b_in_p = jnp.concatenate([b_in[0][sp[i]:sp[i + 1]] for i in order]).reshape(1, N_IN)
    cos_t, sin_t = _rope_tables(seq, t_new, n_batch, n_dec)
    proj = _inproj(xp, xs, attn_norm_g[0].reshape(1, D_MODEL), w_in_p, b_in_p, cos_t, sin_t)

    ssm_params = (ssm_a_re[0], ssm_a_im[0], ssm_log_dt[0], ssm_b_re[0], ssm_b_im[0], ssm_c_re[0], ssm_c_im[0], ssm_d[0])
    zeros = jnp.zeros((n_batch, g * p), F32)
    y_ssm, sp_re, sp_im = _ssm(proj, 0, m, n_batch, seq // SSM_CHUNK, SSM_CHUNK,
                               _ssm_operators(*ssm_params, SSM_CHUNK, BF16), zeros, zeros)
    y_ssm, ss_re, ss_im = _ssm(proj, mp, m, n_dec, 1, t_new, _ssm_operators(*ssm_params, t_new, F32),
                               state_ssm_re[0].reshape(n_dec, g * p), state_ssm_im[0].reshape(n_dec, g * p), y_prev=y_ssm)

    sinks = attn_sinks[0].reshape(1, N_HEADS)
    o_attn = _attn_prompt(proj, sinks, n_batch, seq, m)
    o_attn = _attn_sample(proj, mp, cache_k[0].reshape(n_dec * WINDOW, D_KV), cache_v[0].reshape(n_dec * WINDOW, D_KV),
                          sinks, n_dec, t_new, o_attn)

    wr = jnp.pad(w_router[0], ((0, 0), (0, ROUTER_LANES - N_EXPERTS)))
    br = jnp.pad(b_router[0], (0, ROUTER_LANES - N_EXPERTS), constant_values=NEG_INF).reshape(1, ROUTER_LANES)
    x1, h2, top_i, gate, rank, counts = _merge(
        y_ssm, o_attn, proj, xp, xs, w_glu[0].astype(BF16), b_glu[0].reshape(1, D_SSM), w_branch_ssm[0].astype(BF16),
        w_branch_attn[0].astype(BF16), w_out[0].astype(BF16), ffn_norm_g[0].reshape(1, D_MODEL), wr, br)

    route = _route(top_i[:, :TOP_K], rank[:, :TOP_K], counts[0, :N_EXPERTS], m)
    x_sorted = _gather(h2, route, m)
    y_sorted = _moe(x_sorted, route, w_up[0], b_up[0].reshape(N_EXPERTS, 1, 2 * D_FF), w_down[0],
                    b_down[0].reshape(N_EXPERTS, 1, D_MODEL), m)
    y_p, y_s = _combine(y_sorted, route, x1, gate, final_norm_g.reshape(1, D_MODEL), mp)

    y_prompt = y_p.reshape(n_batch, seq, D_MODEL)
    y_sample = y_s.reshape(n_dec, t_new, D_MODEL)
    to_state = lambda s: s.reshape(1, s.shape[0], g, p)
    k_all = proj[:, COL_K:COL_K + D_KV]
    v_all = proj[:, COL_V:COL_V + D_KV]
    kv_p = lambda a: a[:mp].reshape(n_batch, seq, N_KV_HEADS, HEAD_DIM)[:, -WINDOW:][None]
    kv_s = lambda a, c: jnp.concatenate([c[0][:, t_new:], a[mp:].reshape(n_dec, t_new, N_KV_HEADS, HEAD_DIM)], axis=1)[None]
    return (y_prompt, y_sample, to_state(sp_re), to_state(sp_im), kv_p(k_all), kv_p(v_all),
            to_state(ss_re), to_state(ss_im), kv_s(k_all, cache_k), kv_s(v_all, cache_v))
```

```python
import functools

import jax
import jax.numpy as jnp
from jax import lax
from jax.experimental import pallas as pl
from jax.experimental.pallas import tpu as pltpu

F32 = jnp.float32
BF16 = jnp.bfloat16
I32 = jnp.int32
HIGHEST = lax.Precision.HIGHEST

D_MODEL = 2048
RMS_EPS = 1e-5
SSM_GROUP = 16
SSM_GROUPS = 64
SSM_STATE = 64
D_SSM = 1024
HEAD_DIM = 64
N_HEADS = 16
N_KV_HEADS = 4
Q_PER_KV = 4
D_ATTN = 1024
D_KV = 256
WINDOW = 128
ROPE_THETA = 10000.0
NEG_INF = -1e30
PAST_LEN = 16384
N_IN = D_SSM + D_ATTN + 2 * D_KV + 2 * D_MODEL
N_EXPERTS = 32
TOP_K = 4
D_FF = 2048
SWIGLU_ALPHA = 1.702
SWIGLU_LIMIT = 7.0

COL_U, COL_Q, COL_K, COL_V, COL_GS, COL_GA = 0, 1024, 2048, 2304, 2560, 4608

VMEM_LIMIT = 56 * 1024 * 1024


def _sigmoid(x):
    return 1.0 / (1.0 + jnp.exp(-x))


def _qk(q, k):
    return lax.dot_general(q, k, (((1,), (1,)), ((), ())), preferred_element_type=F32)


def _cparams(sem, vmem=VMEM_LIMIT):
    return pltpu.CompilerParams(dimension_semantics=sem, vmem_limit_bytes=vmem)


TM1 = 512
TN1 = 512
_ROPE_TILE0 = COL_Q // TN1
_KV_TILE = COL_K // TN1
_GATE_TILE0 = COL_GS // TN1


def _two_source(i, n_first, a_ref, b_ref):
    return jnp.where(i < n_first, a_ref[...], b_ref[...])


def _two_source_specs(tile, width, n_first):
    return [pl.BlockSpec((tile, width), lambda i, *_: (jnp.minimum(i, n_first - 1), 0)),
            pl.BlockSpec((tile, width), lambda i, *_: (jnp.maximum(i - n_first, 0), 0))]


def _inproj_kernel(xa_ref, xb_ref, g_ref, w_ref, b_ref, cos_ref, sin_ref, o_ref, h_scr, *, n_first):
    j = pl.program_id(1)

    @pl.when(j == 0)
    def _norm():
        x = _two_source(pl.program_id(0), n_first, xa_ref, xb_ref)
        ms = jnp.mean(x * x, axis=-1, keepdims=True)
        h_scr[...] = (x * lax.rsqrt(ms + RMS_EPS) * g_ref[...]).astype(BF16)

    acc = jnp.dot(h_scr[...], w_ref[...], preferred_element_type=F32) + b_ref[...]

    @pl.when(j >= _GATE_TILE0)
    def _gate():
        o_ref[...] = _sigmoid(acc)

    @pl.when(j < _ROPE_TILE0)
    def _plain():
        o_ref[...] = acc

    @pl.when(jnp.logical_and(j >= _ROPE_TILE0, j <= _KV_TILE))
    def _rope():
        c = jnp.concatenate([cos_ref[...]] * (TN1 // 128), axis=1)
        s = jnp.concatenate([sin_ref[...]] * (TN1 // 128), axis=1)
        lane = lax.broadcasted_iota(I32, acc.shape, 1)
        first_half = (lane & (HEAD_DIM // 2)) == 0
        partner = jnp.where(first_half, pltpu.roll(acc, TN1 - HEAD_DIM // 2, 1), pltpu.roll(acc, HEAD_DIM // 2, 1))
        roped = acc * c + partner * s
        n_rot = jnp.where(j == _KV_TILE, D_KV, TN1)
        o_ref[...] = jnp.where(lane < n_rot, roped, acc)


def _inproj(xa, xb, g, w_bf, b, cos_t, sin_t):
    assert xa.shape[0] % TM1 == 0 and xb.shape[0] % TM1 == 0
    m = xa.shape[0] + xb.shape[0]
    n_first = xa.shape[0] // TM1
    return pl.pallas_call(
        functools.partial(_inproj_kernel, n_first=n_first),
        grid=(m // TM1, N_IN // TN1),
        in_specs=_two_source_specs(TM1, D_MODEL, n_first) + [
            pl.BlockSpec((1, D_MODEL), lambda i, j: (0, 0)),
            pl.BlockSpec((D_MODEL, TN1), lambda i, j: (0, j)),
            pl.BlockSpec((1, TN1), lambda i, j: (0, j)),
            pl.BlockSpec((TM1, 128), lambda i, j: (i, 0)),
            pl.BlockSpec((TM1, 128), lambda i, j: (i, 0)),
        ],
        out_specs=pl.BlockSpec((TM1, TN1), lambda i, j: (i, j)),
        out_shape=jax.ShapeDtypeStruct((m, N_IN), F32),
        scratch_shapes=[pltpu.VMEM((TM1, D_MODEL), BF16)],
        compiler_params=_cparams(("parallel", "arbitrary")),
        name="inproj",
    )(xa, xb, g, w_bf, b, cos_t, sin_t)


SSM_GB = 8


def _ssm_operators(a_re, a_im, log_dt, b_re, b_im, c_re, c_im, d_skip, chunk, w_dtype):
    a_re = a_re.astype(F32)
    a_im = a_im.astype(F32)
    dt = jnp.exp(log_dt.astype(F32))[:, None]
    mag = jnp.exp(a_re * dt)
    lb_re = mag * jnp.cos(a_im * dt)
    lb_im = mag * jnp.sin(a_im * dt)
    inv = 1.0 / (a_re * a_re + a_im * a_im)
    f_re = ((lb_re - 1.0) * a_re + lb_im * a_im) * inv
    f_im = (lb_im * a_re - (lb_re - 1.0) * a_im) * inv
    g, p, h = b_re.shape
    nlb = g // SSM_GB
    sw = SSM_GB * p
    br = jnp.transpose(b_re.astype(F32), (0, 2, 1))
    bi = jnp.transpose(b_im.astype(F32), (0, 2, 1))
    bb_re = f_re[:, None, :] * br - f_im[:, None, :] * bi
    bb_im = f_re[:, None, :] * bi + f_im[:, None, :] * br
    k = jnp.arange(chunk + 1, dtype=F32)[:, None, None]
    pmag = jnp.exp(a_re * dt * k)
    pw_re = (pmag * jnp.cos(a_im * dt * k))[:, :, None, :]
    pw_im = (pmag * jnp.sin(a_im * dt * k))[:, :, None, :]
    pb_re = pw_re[:chunk] * bb_re[None] - pw_im[:chunk] * bb_im[None]
    pb_im = pw_re[:chunk] * bb_im[None] + pw_im[:chunk] * bb_re[None]
    cr = c_re.astype(F32)
    ci = c_im.astype(F32)
    cv_re = cr[None] * pw_re[1:] - ci[None] * pw_im[1:]
    cv_im = -(cr[None] * pw_im[1:] + ci[None] * pw_re[1:])

    def lane_block_rows(x):
        return jnp.transpose(x.reshape(x.shape[0], nlb, SSM_GB * h, x.shape[-1]), (1, 0, 2, 3))

    def same_group(rows_per, cols_per, n_rows, n_cols):
        return (jnp.arange(n_rows)[:, None] // rows_per == jnp.arange(n_cols)[None, :] // cols_per).astype(F32)
    spread = jnp.tile(jnp.eye(p, dtype=F32), (1, SSM_GB))

    def block_diag_states(x, dtype):
        y = jnp.einsum('blrp,pc->blrc', lane_block_rows(x), spread, precision=HIGHEST) * same_group(h, p, 128, sw)
        return y.reshape(nlb, chunk * 128, sw).astype(dtype)
    c_cat = jnp.concatenate([cr.reshape(nlb, 128, p), -ci.reshape(nlb, 128, p)], axis=-1)
    pb_cat = jnp.concatenate([lane_block_rows(pb_re), lane_block_rows(pb_im)], axis=-1)
    lag_blocks = jnp.einsum('blrq,bcq->blrc', pb_cat, c_cat, precision=HIGHEST) * same_group(h, h, 128, 128)
    d_row = jnp.tile(d_skip.astype(F32).reshape(nlb, 1, 128), (1, 1, chunk))
    return dict(lag_blocks=lag_blocks.astype(BF16),
                w_re=block_diag_states(pb_re[::-1], w_dtype), w_im=block_diag_states(pb_im[::-1], w_dtype),
                v_re=block_diag_states(cv_re, BF16), v_im=block_diag_states(cv_im, BF16),
                lbl_re=pw_re[chunk].reshape(1, g * p), lbl_im=pw_im[chunk].reshape(1, g * p), d_row=d_row)


def _ssm_kernel(*refs, chunk, n_chunks, n_seq, aliased):
    (u_ref, t_ref, wr_ref, wi_ref, vr_ref, vi_ref, lr_ref, li_ref, d_ref, s0r_ref, s0i_ref) = refs[:11]
    y_ref, sr_ref, si_ref, z_scr, t_scr, lor_scr, loi_scr, pr_scr, pi_scr = refs[11 + aliased:]
    rows = n_seq * n_chunks
    for j in range(chunk):
        for t in range(chunk):
            blk = t_ref[0, t - j] if t >= j else jnp.zeros((128, 128), BF16)
            t_scr[j * 128:(j + 1) * 128, t * 128:(t + 1) * 128] = blk
    for j in range(chunk):
        z_scr[:, j * 128:(j + 1) * 128] = u_ref[pl.ds(j, rows, stride=chunk), :]
    z = z_scr[...]
    zw = z.astype(wr_ref.dtype)
    prec = HIGHEST if wr_ref.dtype == F32 else None
    lor = jnp.dot(zw, wr_ref[0], preferred_element_type=F32, precision=prec)
    loi = jnp.dot(zw, wi_ref[0], preferred_element_type=F32, precision=prec)
    lr = lr_ref[...]
    li = li_ref[...]
    sr = s0r_ref[...]
    si = s0i_ref[...]
    if n_chunks == 1:
        pr, pi = sr, si
        sr, si = lr * sr - li * si + lor, lr * si + li * sr + loi
    else:
        pieces = range(lor.shape[1] // 128)
        cut = lambda a: tuple(a[:, k * 128:(k + 1) * 128] for k in pieces)
        for k in pieces:
            lor_scr[k] = lor[:, k * 128:(k + 1) * 128]
            loi_scr[k] = loi[:, k * 128:(k + 1) * 128]
        lrs, lis = cut(lr), cut(li)

        def step(c, carry):
            idx = pl.ds(c, n_seq, stride=n_chunks)
            nxt = []
            for k, (a, b) in enumerate(zip(*carry)):
                pr_scr[k, idx, :] = a
                pi_scr[k, idx, :] = b
                nxt.append((lrs[k] * a - lis[k] * b + lor_scr[k, idx, :], lrs[k] * b + lis[k] * a + loi_scr[k, idx, :]))
            return tuple(n[0] for n in nxt), tuple(n[1] for n in nxt)
        srs, sis = lax.fori_loop(0, n_chunks, step, (cut(sr), cut(si)), unroll=8)
        sr, si = jnp.concatenate(srs, axis=1), jnp.concatenate(sis, axis=1)
        pr = jnp.concatenate([pr_scr[k] for k in pieces], axis=1)
        pi = jnp.concatenate([pi_scr[k] for k in pieces], axis=1)
    sr_ref[...] = sr
    si_ref[...] = si
    y = jnp.dot(z.astype(BF16), t_scr[...], preferred_element_type=F32)
    y += _qk(pr.astype(BF16), vr_ref[0])
    y += _qk(pi.astype(BF16), vi_ref[0])
    y += z * d_ref[0]
    for j in range(chunk):
        y_ref[pl.ds(j, rows, stride=chunk), :] = y[:, j * 128:(j + 1) * 128]


def _ssm(proj, row0, m_out, n_seq, n_chunks, chunk, ops, s0_re, s0_im, y_prev=None):
    rows = n_seq * n_chunks
    tok = rows * chunk
    nlb = SSM_GROUPS // SSM_GB
    width = chunk * 128
    sw = SSM_GB * SSM_STATE
    blk0 = row0 // tok
    per_lb = lambda a, b: pl.BlockSpec((1, a, b), lambda i: (i, 0, 0))
    lanes = lambda r, w: pl.BlockSpec((r, w), lambda i: (0, i))
    in_specs = [pl.BlockSpec((tok, 128), lambda i: (blk0, COL_U // 128 + i)),
                pl.BlockSpec((1, chunk, 128, 128), lambda i: (i, 0, 0, 0)),
                per_lb(width, sw), per_lb(width, sw), per_lb(width, sw), per_lb(width, sw),
                lanes(1, sw), lanes(1, sw), per_lb(1, width), lanes(n_seq, sw), lanes(n_seq, sw)]
    args = [proj, ops['lag_blocks'], ops['w_re'], ops['w_im'], ops['v_re'], ops['v_im'],
            ops['lbl_re'], ops['lbl_im'], ops['d_row'], s0_re, s0_im]
    aliases = {}
    if y_prev is not None:
        in_specs.append(pl.BlockSpec(memory_space=pl.ANY))
        args.append(y_prev)
        aliases = {len(args) - 1: 0}
    state = jax.ShapeDtypeStruct((n_seq, SSM_GROUPS * SSM_STATE), F32)
    return pl.pallas_call(
        functools.partial(_ssm_kernel, chunk=chunk, n_chunks=n_chunks, n_seq=n_seq, aliased=int(y_prev is not None)),
        grid=(nlb,),
        in_specs=in_specs,
        out_specs=[pl.BlockSpec((tok, 128), lambda i: (blk0, i)), lanes(n_seq, sw), lanes(n_seq, sw)],
        out_shape=[jax.ShapeDtypeStruct((m_out, D_SSM), F32), state, state],
        scratch_shapes=[pltpu.VMEM((rows, width), F32), pltpu.VMEM((width, width), BF16)]
        + [pltpu.VMEM((sw // 128, rows, 128), F32)] * 4,
        input_output_aliases=aliases,
        compiler_params=_cparams(("parallel",)),
        name=f"ssm_c{n_chunks}",
    )(*args)


def _softmax_sink_pv(parts, sink):
    masked = [jnp.where(valid, s, NEG_INF) for s, valid, _ in parts]
    m = sink
    for s in masked:
        m = jnp.maximum(m, jnp.max(s, axis=-1, keepdims=True))
    denom = jnp.exp(sink - m)
    out = None
    for s, (_, _, v) in zip(masked, parts):
        e = jnp.exp(s - m)
        denom = denom + jnp.sum(e, axis=-1, keepdims=True)
        pv = jnp.dot(e.astype(BF16), v, preferred_element_type=F32)
        out = pv if out is None else out + pv
    return out / denom


def _attn_prompt_kernel(q_ref, kp_ref, kc_ref, vp_ref, vc_ref, sink_ref, o_ref):
    qb = pl.program_id(1)
    rows = Q_PER_KV * WINDOW
    t = lax.broadcasted_iota(I32, (rows, 2 * WINDOW), 0) & (WINDOW - 1)
    col = lax.broadcasted_iota(I32, (rows, 2 * WINDOW), 1)
    valid = jnp.logical_or(jnp.logical_and(jnp.logical_and(col < WINDOW, col > t), qb > 0),
                           jnp.logical_and(col >= WINDOW, (col - WINDOW) <= t))
    scale = HEAD_DIM ** -0.5
    for kv in range(N_KV_HEADS):
        lanes = slice(kv * HEAD_DIM, (kv + 1) * HEAD_DIM)
        k = jnp.concatenate([kp_ref[:, lanes], kc_ref[:, lanes]], axis=0).astype(BF16)
        v = jnp.concatenate([vp_ref[:, lanes], vc_ref[:, lanes]], axis=0).astype(BF16)
        heads = [kv * Q_PER_KV + i for i in range(Q_PER_KV)]
        q = jnp.concatenate([q_ref[:, h * HEAD_DIM:(h + 1) * HEAD_DIM] for h in heads], axis=0).astype(BF16)
        sink = jnp.concatenate(
            [jnp.broadcast_to(sink_ref[0:1, h:h + 1], (WINDOW, 1)) for h in heads], axis=0)
        s = _qk(q, k) * scale
        o = _softmax_sink_pv([(s, valid, v)], sink)
        for i, h in enumerate(heads):
            o_ref[:, h * HEAD_DIM:(h + 1) * HEAD_DIM] = o[i * WINDOW:(i + 1) * WINDOW, :]


def _attn_prompt(proj, sinks, n_batch, seq, m_out):
    nb = seq // WINDOW
    cq, ck, cv = COL_Q // D_ATTN, COL_K // D_KV, COL_V // D_KV
    cur = lambda col: (lambda n, b: (n * nb + b, col))
    prev = lambda col: (lambda n, b: (n * nb + jnp.maximum(b - 1, 0), col))
    return pl.pallas_call(
        _attn_prompt_kernel,
        grid=(n_batch, nb),
        in_specs=[
            pl.BlockSpec((WINDOW, D_ATTN), cur(cq)),
            pl.BlockSpec((WINDOW, D_KV), prev(ck)),
            pl.BlockSpec((WINDOW, D_KV), cur(ck)),
            pl.BlockSpec((WINDOW, D_KV), prev(cv)),
            pl.BlockSpec((WINDOW, D_KV), cur(cv)),
            pl.BlockSpec((1, N_HEADS), lambda n, b: (0, 0)),
        ],
        out_specs=pl.BlockSpec((WINDOW, D_ATTN), lambda n, b: (n * nb + b, 0)),
        out_shape=jax.ShapeDtypeStruct((m_out, D_ATTN), F32),
        compiler_params=_cparams(("parallel", "arbitrary")),
        name="attn_prompt",
    )(proj, proj, proj, proj, proj, sinks)


SAMPLE_GB = 8


def _attn_sample_kernel(q_ref, kn_ref, vn_ref, kc_ref, vc_ref, sink_ref, o_ref, *, t_new):
    tok = SAMPLE_GB * t_new
    rows = Q_PER_KV * tok
    ncache = SAMPLE_GB * WINDOW
    r_c = lax.broadcasted_iota(I32, (rows, ncache), 0)
    c_c = lax.broadcasted_iota(I32, (rows, ncache), 1)
    rb_c = (r_c % tok) // t_new
    rt_c = r_c % t_new
    valid_c = jnp.logical_and(rb_c == c_c // WINDOW, (c_c % WINDOW) > rt_c)
    r_n = lax.broadcasted_iota(I32, (rows, tok), 0)
    c_n = lax.broadcasted_iota(I32, (rows, tok), 1)
    valid_n = jnp.logical_and((r_n % tok) // t_new == c_n // t_new, (c_n % t_new) <= (r_n % t_new))
    scale = HEAD_DIM ** -0.5
    for kv in range(N_KV_HEADS):
        lanes = slice(kv * HEAD_DIM, (kv + 1) * HEAD_DIM)
        kc = kc_ref[:, lanes].astype(BF16)
        vc = vc_ref[:, lanes].astype(BF16)
        kn = kn_ref[:, lanes].astype(BF16)
        vn = vn_ref[:, lanes].astype(BF16)
        heads = [kv * Q_PER_KV + i for i in range(Q_PER_KV)]
        q = jnp.concatenate([q_ref[:, h * HEAD_DIM:(h + 1) * HEAD_DIM] for h in heads], axis=0).astype(BF16)
        sink = jnp.concatenate(
            [jnp.broadcast_to(sink_ref[0:1, h:h + 1], (tok, 1)) for h in heads], axis=0)
        o = _softmax_sink_pv([(_qk(q, kc) * scale, valid_c, vc), (_qk(q, kn) * scale, valid_n, vn)], sink)
        for i, h in enumerate(heads):
            o_ref[:, h * HEAD_DIM:(h + 1) * HEAD_DIM] = o[i * tok:(i + 1) * tok, :]


def _attn_sample_aliased(q_ref, kn_ref, vn_ref, kc_ref, vc_ref, sink_ref, o_prev, o_ref, *, t_new):
    del o_prev
    _attn_sample_kernel(q_ref, kn_ref, vn_ref, kc_ref, vc_ref, sink_ref, o_ref, t_new=t_new)


def _attn_sample(proj, row0, cache_k, cache_v, sinks, n_batch, t_new, o_prev):
    tok = SAMPLE_GB * t_new
    blk0 = row0 // tok
    cq, ck, cv = COL_Q // D_ATTN, COL_K // D_KV, COL_V // D_KV
    new = lambda col: (lambda g: (blk0 + g, col))
    return pl.pallas_call(
        functools.partial(_attn_sample_aliased, t_new=t_new),
        grid=(n_batch // SAMPLE_GB,),
        in_specs=[
            pl.BlockSpec((tok, D_ATTN), new(cq)),
            pl.BlockSpec((tok, D_KV), new(ck)),
            pl.BlockSpec((tok, D_KV), new(cv)),
            pl.BlockSpec((SAMPLE_GB * WINDOW, D_KV), lambda g: (g, 0)),
            pl.BlockSpec((SAMPLE_GB * WINDOW, D_KV), lambda g: (g, 0)),
            pl.BlockSpec((1, N_HEADS), lambda g: (0, 0)),
            pl.BlockSpec(memory_space=pl.ANY),
        ],
        out_specs=pl.BlockSpec((tok, D_ATTN), lambda g: (blk0 + g, 0)),
        out_shape=jax.ShapeDtypeStruct(o_prev.shape, F32),
        input_output_aliases={6: 0},
        compiler_params=_cparams(("parallel",)),
        name="attn_sample",
    )(proj, proj, proj, cache_k, cache_v, sinks, o_prev)


TM4 = 256
ROUTER_LANES = 128
GATE_W = 512
GATE_BLOCKS = D_MODEL // GATE_W
ROW_TILES = D_MODEL // 128


def _gelu_tanh(x):
    return 0.5 * x * (1.0 + jnp.tanh(0.7978845608028654 * (x + 0.044715 * (x * x * x))))


def _merge_kernel(y_ref, o_ref, *refs, n_first):
    gs_refs, ga_refs = refs[:GATE_BLOCKS], refs[GATE_BLOCKS:2 * GATE_BLOCKS]
    (xa_ref, xb_ref, wglu_ref, bglu_ref, wbs_ref, wba_ref, wout_ref, g2_ref, wr_ref, wrl_ref, br_ref,
     x1_ref, h2_ref, ti_ref, tg_ref, rank_ref, cnt_ref, seen_scr) = refs[2 * GATE_BLOCKS:]

    @pl.when(pl.program_id(0) == 0)
    def _reset():
        seen_scr[...] = jnp.zeros(seen_scr.shape, F32)

    y1 = _gelu_tanh(y_ref[...])
    z = jnp.dot(y1.astype(BF16), wglu_ref[...], preferred_element_type=F32) + bglu_ref[...]
    y2 = y1 * _sigmoid(z)
    bs = jnp.dot(y2.astype(BF16), wbs_ref[...], preferred_element_type=F32)
    ba = jnp.dot(o_ref[...].astype(BF16), wba_ref[...], preferred_element_type=F32)
    merged = jnp.concatenate(
        [gs[...] * bs[:, c * GATE_W:(c + 1) * GATE_W] + ga[...] * ba[:, c * GATE_W:(c + 1) * GATE_W]
         for c, (gs, ga) in enumerate(zip(gs_refs, ga_refs))], axis=1)
    x = _two_source(pl.program_id(0), n_first, xa_ref, xb_ref)
    x1 = x + jnp.dot(merged.astype(BF16), wout_ref[...], preferred_element_type=F32)
    x1_ref[...] = x1
    ms = jnp.mean(x1 * x1, axis=-1, keepdims=True)
    h2 = x1 * lax.rsqrt(ms + RMS_EPS) * g2_ref[...]
    for s in range(ROW_TILES):
        h2_ref[pl.ds(s, TM4, stride=ROW_TILES), :] = h2[:, s * 128:(s + 1) * 128]
    h_hi = h2.astype(BF16)
    h_lo = (h2 - h_hi.astype(F32)).astype(BF16)
    logits = (jnp.dot(h_hi, wr_ref[...], preferred_element_type=F32)
              + (jnp.dot(h_hi, wrl_ref[...], preferred_element_type=F32)
                 + jnp.dot(h_lo, wr_ref[...], preferred_element_type=F32))) + br_ref[...]
    lane = lax.broadcasted_iota(I32, logits.shape, 1)
    cur = logits
    top_i = jnp.zeros(logits.shape, I32)
    top_v = jnp.zeros(logits.shape, F32)
    chosen = []
    for k in range(TOP_K):
        mx = jnp.max(cur, axis=-1, keepdims=True)
        idx = jnp.min(jnp.where(cur == mx, lane, ROUTER_LANES), axis=-1, keepdims=True)
        top_i = jnp.where(lane == k, idx, top_i)
        top_v = jnp.where(lane == k, mx, top_v)
        chosen.append(lane == idx)
        cur = jnp.where(chosen[-1], -jnp.inf, cur)
    v0 = jnp.max(jnp.where(lane < TOP_K, top_v, -jnp.inf), axis=-1, keepdims=True)
    e = jnp.where(lane < TOP_K, jnp.exp(top_v - v0), 0.0)
    ti_ref[...] = top_i
    tg_ref[...] = e / jnp.sum(e, axis=-1, keepdims=True)
    hits = sum(jnp.where(c, 1.0, 0.0) for c in chosen)
    r = lax.broadcasted_iota(I32, (TM4, TM4), 0)
    c = lax.broadcasted_iota(I32, (TM4, TM4), 1)
    before = jnp.dot(jnp.where(c < r, 1.0, 0.0).astype(BF16), hits.astype(BF16), preferred_element_type=F32)
    before = before + seen_scr[...]
    rank = jnp.zeros(logits.shape, F32)
    for k in range(TOP_K):
        rank = jnp.where(lane == k, jnp.sum(jnp.where(chosen[k], before, 0.0), axis=-1, keepdims=True), rank)
    rank_ref[...] = rank.astype(I32)
    seen_scr[...] = seen_scr[...] + jnp.sum(hits, axis=0, keepdims=True)
    cnt_ref[...] = seen_scr[...].astype(I32)


def _merge(y_ssm, o_attn, proj, xa, xb, wglu, bglu, wbs, wba, wout, g2, wr_hi, wr_lo, br):
    assert xa.shape[0] % TM4 == 0 and xb.shape[0] % TM4 == 0
    m = xa.shape[0] + xb.shape[0]
    n_first = xa.shape[0] // TM4
    row = lambda w: pl.BlockSpec((TM4, w), lambda i: (i, 0))
    const = lambda a, b: pl.BlockSpec((a, b), lambda i: (0, 0), pipeline_mode=pl.Buffered(1))
    return pl.pallas_call(
        functools.partial(_merge_kernel, n_first=n_first),
        grid=(m // TM4,),
        in_specs=[
            row(D_SSM), row(D_ATTN),
            *[pl.BlockSpec((TM4, GATE_W), lambda i, c=col // GATE_W + k: (i, c))
              for col in (COL_GS, COL_GA) for k in range(GATE_BLOCKS)],
            *_two_source_specs(TM4, D_MODEL, n_first),
            const(D_SSM, D_SSM), const(1, D_SSM), const(D_SSM, D_MODEL), const(D_ATTN, D_MODEL),
            const(D_MODEL, D_MODEL), const(1, D_MODEL), const(D_MODEL, ROUTER_LANES), const(D_MODEL, ROUTER_LANES),
            const(1, ROUTER_LANES),
        ],
        out_specs=[row(D_MODEL), pl.BlockSpec((TM4 * ROW_TILES, 128), lambda i: (i, 0)),
                   row(ROUTER_LANES), row(ROUTER_LANES), row(ROUTER_LANES),
                   pl.BlockSpec((1, ROUTER_LANES), lambda i: (0, 0))],
        out_shape=[jax.ShapeDtypeStruct((m, D_MODEL), F32), jax.ShapeDtypeStruct((m * ROW_TILES, 128), F32),
                   jax.ShapeDtypeStruct((m, ROUTER_LANES), I32), jax.ShapeDtypeStruct((m, ROUTER_LANES), F32),
                   jax.ShapeDtypeStruct((m, ROUTER_LANES), I32), jax.ShapeDtypeStruct((1, ROUTER_LANES), I32)],
        scratch_shapes=[pltpu.VMEM((1, ROUTER_LANES), F32)],
        compiler_params=_cparams(("arbitrary",)),
        name="merge",
    )(y_ssm, o_attn, *([proj] * (2 * GATE_BLOCKS)), xa, xb, wglu, bglu, wbs, wba, wout, g2, wr_hi, wr_lo, br)


SUB = 128
SUPER = 1280
TF = 256
NF = D_FF // TF
MOE_WIDE = 8


def _moe_sizes(m):
    a = m * TOP_K
    ns_max = N_EXPERTS + a // SUPER
    nb_max = N_EXPERTS + a // SUB
    return a, ns_max, nb_max


def _route(top_i, rank, counts, m):
    a, ns_max, nb_max = _moe_sizes(m)
    experts = jnp.arange(N_EXPERTS, dtype=I32)
    nsb = (counts + SUPER - 1) // SUPER
    sb_end = jnp.cumsum(nsb)
    sb_start = sb_end - nsb
    base = jnp.sum(jnp.where(top_i[:, :, None] == experts, sb_start * SUPER, 0), axis=-1)
    pos = (base + rank).reshape(a).astype(I32)
    n_slots = ns_max * SUPER + SUB
    tok_of_slot = jnp.zeros((n_slots,), I32).at[pos].set(jnp.arange(a, dtype=I32) // TOP_K, unique_indices=True)
    s_idx = jnp.arange(ns_max, dtype=I32)
    ns_used = sb_end[-1]
    s_eff = jnp.maximum(jnp.minimum(s_idx, ns_used - 1), 0)
    se = jnp.minimum(jnp.sum((s_eff[:, None] >= sb_end[None, :]).astype(I32), axis=1), N_EXPERTS - 1)
    of_se = lambda v: jnp.sum(jnp.where(se[:, None] == experts, v, 0), axis=-1)
    rows_in = jnp.clip(of_se(counts) - (s_eff - of_se(sb_start)) * SUPER, 0, SUPER)
    nsub = jnp.where(s_idx < ns_used, (rows_in + SUB - 1) // SUB, 0).astype(I32)
    nb = (counts + SUB - 1) // SUB
    b_end = jnp.cumsum(nb)
    b_idx = jnp.arange(nb_max, dtype=I32)
    eb = jnp.minimum(jnp.sum((b_idx[:, None] >= b_end[None, :]).astype(I32), axis=1), N_EXPERTS - 1)
    of_eb = lambda v: jnp.sum(jnp.where(eb[:, None] == experts, v, 0), axis=-1)
    ib = b_idx - of_eb(b_end - nb)
    dst = jnp.where(b_idx < b_end[-1], of_eb(sb_start) * (SUPER // SUB) + ib, ns_max * (SUPER // SUB)).astype(I32)
    tok = tok_of_slot.reshape(n_slots // SUB, SUB)[dst].reshape(nb_max * SUB)
    return dict(pos=pos, se=se.astype(I32), nsub=nsub, xblk=s_eff.astype(I32), dst=dst, tok=tok)


def _row_copy(src_hbm, row, dst_buf, slot, r, sem):
    return pltpu.make_async_copy(src_hbm.at[pl.ds(pl.multiple_of(row * ROW_TILES, ROW_TILES), ROW_TILES), :],
                                 dst_buf.at[slot, pl.ds(pl.multiple_of(r * ROW_TILES, ROW_TILES), ROW_TILES), :],
                                 sem.at[slot])


def _gather_kernel(dst_ref, tok_ref, h_hbm, o_ref, buf, sem):
    b = pl.program_id(0)
    nb = pl.num_programs(0)

    def issue(blk, slot):
        def body(r, c):
            _row_copy(h_hbm, tok_ref[blk * SUB + r], buf, slot, r, sem).start()
            return c
        lax.fori_loop(0, SUB, body, 0, unroll=16)

    @pl.when(b == 0)
    def _first():
        issue(0, 0)

    @pl.when(b + 1 < nb)
    def _next():
        issue(b + 1, (b + 1) % 2)

    slot = b % 2

    def wait_body(r, c):
        _row_copy(h_hbm, 0, buf, slot, r, sem).wait()
        return c
    lax.fori_loop(0, SUB, wait_body, 0, unroll=16)
    for s in range(ROW_TILES):
        o_ref[:, s * 128:(s + 1) * 128] = buf[slot, pl.ds(s, SUB, stride=ROW_TILES), :].astype(BF16)


def _gather(h2, route, m):
    _, ns_max, nb_max = _moe_sizes(m)
    rows = ns_max * SUPER + SUB
    return pl.pallas_call(
        _gather_kernel,
        grid_spec=pltpu.PrefetchScalarGridSpec(
            num_scalar_prefetch=2,
            grid=(nb_max,),
            in_specs=[pl.BlockSpec(memory_space=pl.ANY)],
            out_specs=pl.BlockSpec((SUB, D_MODEL), lambda b, dst, tok: (dst[b], 0)),
            scratch_shapes=[pltpu.VMEM((2, SUB * ROW_TILES, 128), F32), pltpu.SemaphoreType.DMA((2,))],
        ),
        out_shape=jax.ShapeDtypeStruct((rows, D_MODEL), BF16),
        compiler_params=_cparams(("arbitrary",)),
        name="moe_gather",
    )(route['dst'], route['tok'], h2)


def _moe_kernel(se_ref, nsub_ref, xb_ref, x_ref, wg_ref, wl_ref, bg_ref, bl_ref, wd_ref, bd_ref, o_ref,
                wg_s, wl_s, wd_s):
    s = pl.program_id(0)
    f = pl.program_id(1)
    n = nsub_ref[s]

    @pl.when(n > 0)
    def _work():
        wg_s[...] = wg_ref[0].astype(BF16)
        wl_s[...] = wl_ref[0].astype(BF16)
        wd_s[...] = wd_ref[0].astype(BF16)

        @pl.when(f == 0)
        def _init():
            o_ref[...] = jnp.broadcast_to(bd_ref[0], o_ref.shape)

        bg = bg_ref[0]
        bl = bl_ref[0]

        def expert_rows(r0, rows):
            xs = x_ref[pl.ds(r0, rows), :]
            hg = jnp.dot(xs, wg_s[...], preferred_element_type=F32) + bg
            hl = jnp.dot(xs, wl_s[...], preferred_element_type=F32) + bl
            hg = jnp.minimum(hg, SWIGLU_LIMIT)
            hl = jnp.clip(hl, -SWIGLU_LIMIT, SWIGLU_LIMIT)
            act = hg * _sigmoid(SWIGLU_ALPHA * hg) * (hl + 1.0)
            o_ref[pl.ds(r0, rows), :] += jnp.dot(act.astype(BF16), wd_s[...], preferred_element_type=F32)

        def wide_body(i, c):
            expert_rows(pl.multiple_of(i * (MOE_WIDE * SUB), MOE_WIDE * SUB), MOE_WIDE * SUB)
            return c
        n_wide = n // MOE_WIDE
        lax.fori_loop(0, n_wide, wide_body, 0)

        def single_body(i, c):
            expert_rows(pl.multiple_of(i * SUB, SUB), SUB)
            return c
        lax.fori_loop(n_wide * MOE_WIDE, n, single_body, 0)


def _moe(x_sorted, route, w_up, b_up, w_down, b_down, m):
    _, ns_max, _ = _moe_sizes(m)
    rows = x_sorted.shape[0]
    return pl.pallas_call(
        _moe_kernel,
        grid_spec=pltpu.PrefetchScalarGridSpec(
            num_scalar_prefetch=3,
            grid=(ns_max, NF),
            in_specs=[
                pl.BlockSpec((SUPER, D_MODEL), lambda s, f, se, ns, xb: (xb[s], 0)),
                pl.BlockSpec((1, D_MODEL, TF), lambda s, f, se, ns, xb: (se[s], 0, f)),
                pl.BlockSpec((1, D_MODEL, TF), lambda s, f, se, ns, xb: (se[s], 0, NF + f)),
                pl.BlockSpec((1, 1, TF), lambda s, f, se, ns, xb: (se[s], 0, f)),
                pl.BlockSpec((1, 1, TF), lambda s, f, se, ns, xb: (se[s], 0, NF + f)),
                pl.BlockSpec((1, TF, D_MODEL), lambda s, f, se, ns, xb: (se[s], f, 0)),
                pl.BlockSpec((1, 1, D_MODEL), lambda s, f, se, ns, xb: (se[s], 0, 0)),
            ],
            out_specs=pl.BlockSpec((SUPER, D_MODEL), lambda s, f, se, ns, xb: (xb[s], 0)),
            scratch_shapes=[pltpu.VMEM((D_MODEL, TF), BF16), pltpu.VMEM((D_MODEL, TF), BF16),
                            pltpu.VMEM((TF, D_MODEL), BF16)],
        ),
        out_shape=jax.ShapeDtypeStruct((rows, D_MODEL), F32),
        compiler_params=_cparams(("arbitrary", "arbitrary")),
        name="moe_experts",
    )(route['se'], route['nsub'], route['xblk'], x_sorted, w_up, w_up, b_up, b_up, w_down, b_down)


TC8 = 32


def _combine_kernel(pos_ref, y_hbm, x1_ref, gate_ref, g_ref, o1_ref, o2_ref, buf, sem, *, n_first):
    b = pl.program_id(0)
    nb = pl.num_programs(0)

    def copies(blk, slot, wait):
        def body(r, c):
            for k in range(TOP_K):
                row = 0 if wait else pos_ref[(blk * TC8 + r) * TOP_K + k]
                cp = pltpu.make_async_copy(y_hbm.at[pl.ds(row, 1), :], buf.at[slot, k, pl.ds(r, 1), :], sem.at[slot])
                cp.wait() if wait else cp.start()
            return c
        lax.fori_loop(0, TC8, body, 0)

    @pl.when(b == 0)
    def _first():
        copies(0, 0, False)

    @pl.when(b + 1 < nb)
    def _next():
        copies(b + 1, (b + 1) % 2, False)

    slot = b % 2
    copies(b, slot, True)
    acc = x1_ref[...]
    gate = gate_ref[...]
    for k in range(TOP_K):
        acc = acc + gate[:, k:k + 1] * buf[slot, k]
    ms = jnp.mean(acc * acc, axis=-1, keepdims=True)
    res = acc * lax.rsqrt(ms + RMS_EPS) * g_ref[...]

    @pl.when(b < n_first)
    def _to_first():
        o1_ref[...] = res

    @pl.when(b >= n_first)
    def _to_second():
        o2_ref[...] = res


def _combine(y_sorted, route, x1, gate, g_final, m_first):
    m = x1.shape[0]
    n_first = m_first // TC8
    return pl.pallas_call(
        functools.partial(_combine_kernel, n_first=n_first),
        grid_spec=pltpu.PrefetchScalarGridSpec(
            num_scalar_prefetch=1,
            grid=(m // TC8,),
            in_specs=[
                pl.BlockSpec(memory_space=pl.ANY),
                pl.BlockSpec((TC8, D_MODEL), lambda b, pos: (b, 0)),
                pl.BlockSpec((TC8, ROUTER_LANES), lambda b, pos: (b, 0)),
                pl.BlockSpec((1, D_MODEL), lambda b, pos: (0, 0)),
            ],
            out_specs=[pl.BlockSpec((TC8, D_MODEL), lambda b, pos: (jnp.minimum(b, n_first - 1), 0)),
                       pl.BlockSpec((TC8, D_MODEL), lambda b, pos: (jnp.maximum(b - n_first, 0), 0))],
            scratch_shapes=[pltpu.VMEM((2, TOP_K, TC8, D_MODEL), F32), pltpu.SemaphoreType.DMA((2,))],
        ),
        out_shape=[jax.ShapeDtypeStruct((m_first, D_MODEL), F32), jax.ShapeDtypeStruct((m - m_first, D_MODEL), F32)],
        compiler_params=_cparams(("arbitrary",)),
        name="moe_combine",
    )(route['pos'], y_sorted, x1, gate, g_final)


SSM_CHUNK = 8


def _rope_tables(seq, t_new, n_batch, n_dec):
    half = HEAD_DIM // 2
    inv = ROPE_THETA ** (-jnp.arange(half, dtype=F32) / half)
    pos = jnp.concatenate([jnp.tile(jnp.arange(seq), n_batch), jnp.tile(PAST_LEN + jnp.arange(t_new), n_dec)])
    ang = pos.astype(F32)[:, None] * inv[None, :]
    cos = jnp.cos(ang)
    sin = jnp.sin(ang)
    return jnp.tile(cos, (1, 4)), jnp.tile(jnp.concatenate([-sin, sin], axis=1), (1, 2))


def kernel(x_prompt, x_sample, state_ssm_re, state_ssm_im, cache_k, cache_v, attn_norm_g, w_in, b_in, ssm_a_re, ssm_a_im, ssm_log_dt, ssm_b_re, ssm_b_im, ssm_c_re, ssm_c_im, ssm_d, w_glu, b_glu, attn_sinks, w_branch_ssm, w_branch_attn, w_out, ffn_norm_g, w_router, b_router, w_up, b_up, w_down, b_down, final_norm_g):
    n_batch, seq, _ = x_prompt.shape
    n_dec, t_new, _ = x_sample.shape
    mp = n_batch * seq
    ms = n_dec * t_new
    m = mp + ms
    g, p, h = SSM_GROUPS, SSM_STATE, SSM_GROUP
    xp = x_prompt.reshape(mp, D_MODEL)
    xs = x_sample.reshape(ms, D_MODEL)

    cos_t, sin_t = _rope_tables(seq, t_new, n_batch, n_dec)
    proj = _inproj(xp, xs, attn_norm_g[0].reshape(1, D_MODEL), w_in[0].astype(BF16), b_in[0].reshape(1, N_IN),
                   cos_t, sin_t)

    ssm_params = (ssm_a_re[0], ssm_a_im[0], ssm_log_dt[0], ssm_b_re[0], ssm_b_im[0], ssm_c_re[0], ssm_c_im[0], ssm_d[0])
    zeros = jnp.zeros((n_batch, g * p), F32)
    y_ssm, sp_re, sp_im = _ssm(proj, 0, m, n_batch, seq // SSM_CHUNK, SSM_CHUNK,
                               _ssm_operators(*ssm_params, SSM_CHUNK, BF16), zeros, zeros)
    y_ssm, ss_re, ss_im = _ssm(proj, mp, m, n_dec, 1, t_new, _ssm_operators(*ssm_params, t_new, F32),
                               state_ssm_re[0].reshape(n_dec, g * p), state_ssm_im[0].reshape(n_dec, g * p), y_prev=y_ssm)

    sinks = attn_sinks[0].reshape(1, N_HEADS)
    o_attn = _attn_prompt(proj, sinks, n_batch, seq, m)
    o_attn = _attn_sample(proj, mp, cache_k[0].reshape(n_dec * WINDOW, D_KV), cache_v[0].reshape(n_dec * WINDOW, D_KV),
                          sinks, n_dec, t_new, o_attn)

    wr = jnp.pad(w_router[0], ((0, 0), (0, ROUTER_LANES - N_EXPERTS)))
    wr_hi = wr.astype(BF16)
    wr_lo = (wr - wr_hi.astype(F32)).astype(BF16)
    br = jnp.pad(b_router[0], (0, ROUTER_LANES - N_EXPERTS), constant_values=NEG_INF).reshape(1, ROUTER_LANES)
    x1, h2, top_i, gate, rank, counts = _merge(
        y_ssm, o_attn, proj, xp, xs, w_glu[0].astype(BF16), b_glu[0].reshape(1, D_SSM), w_branch_ssm[0].astype(BF16),
        w_branch_attn[0].astype(BF16), w_out[0].astype(BF16), ffn_norm_g[0].reshape(1, D_MODEL), wr_hi, wr_lo, br)

    route = _route(top_i[:, :TOP_K], rank[:, :TOP_K], counts[0, :N_EXPERTS], m)
    x_sorted = _gather(h2, route, m)
    y_sorted = _moe(x_sorted, route, w_up[0], b_up[0].reshape(N_EXPERTS, 1, 2 * D_FF), w_down[0],
                    b_down[0].reshape(N_EXPERTS, 1, D_MODEL), m)
    y_p, y_s = _combine(y_sorted, route, x1, gate, final_norm_g.reshape(1, D_MODEL), mp)

    y_prompt = y_p.reshape(n_batch, seq, D_MODEL)
    y_sample = y_s.reshape(n_dec, t_new, D_MODEL)
    to_state = lambda s: s.reshape(1, s.shape[0], g, p)
    k_all = proj[:, COL_K:COL_K + D_KV]
    v_all = proj[:, COL_V:COL_V + D_KV]
    kv_p = lambda a: a[:mp].reshape(n_batch, seq, N_KV_HEADS, HEAD_DIM)[:, -WINDOW:][None]
    kv_s = lambda a, c: jnp.concatenate([c[0][:, t_new:], a[mp:].reshape(n_dec, t_new, N_KV_HEADS, HEAD_DIM)], axis=1)[None]
    return (y_prompt, y_sample, to_state(sp_re), to_state(sp_im), kv_p(k_all), kv_p(v_all),
            to_state(ss_re), to_state(ss_im), kv_s(k_all, cache_k), kv_s(v_all, cache_v))
```

```python
import functools

import jax
import jax.numpy as jnp
from jax import lax
from jax.experimental import pallas as pl
from jax.experimental.pallas import tpu as pltpu

F32 = jnp.float32
BF16 = jnp.bfloat16
I32 = jnp.int32
HIGHEST = lax.Precision.HIGHEST

D_MODEL = 2048
RMS_EPS = 1e-5
SSM_GROUP = 16
SSM_GROUPS = 64
SSM_STATE = 64
D_SSM = 1024
HEAD_DIM = 64
N_HEADS = 16
N_KV_HEADS = 4
Q_PER_KV = 4
D_ATTN = 1024
D_KV = 256
WINDOW = 128
ROPE_THETA = 10000.0
NEG_INF = -1e30
PAST_LEN = 16384
N_IN = D_SSM + D_ATTN + 2 * D_KV + 2 * D_MODEL
N_EXPERTS = 32
TOP_K = 4
D_FF = 2048
SWIGLU_ALPHA = 1.702
SWIGLU_LIMIT = 7.0

COL_U, COL_Q, COL_K, COL_V, COL_GS, COL_GA = 0, 1024, 2048, 2304, 2560, 4608

VMEM_LIMIT = 56 * 1024 * 1024


def _sigmoid(x):
    return 1.0 / (1.0 + jnp.exp(-x))


def _qk(q, k):
    return lax.dot_general(q, k, (((1,), (1,)), ((), ())), preferred_element_type=F32)


def _cparams(sem, vmem=VMEM_LIMIT):
    return pltpu.CompilerParams(dimension_semantics=sem, vmem_limit_bytes=vmem)


TM1 = 512
TN1 = 512
_ROPE_TILE0 = COL_Q // TN1
_KV_TILE = COL_K // TN1
_GATE_TILE0 = COL_GS // TN1


def _two_source(i, n_first, a_ref, b_ref):
    return jnp.where(i < n_first, a_ref[...], b_ref[...])


def _two_source_specs(tile, width, n_first):
    return [pl.BlockSpec((tile, width), lambda i, *_: (jnp.minimum(i, n_first - 1), 0)),
            pl.BlockSpec((tile, width), lambda i, *_: (jnp.maximum(i - n_first, 0), 0))]


def _inproj_kernel(xa_ref, xb_ref, g_ref, w_ref, b_ref, cos_ref, sin_ref, o_ref, h_scr, *, n_first):
    j = pl.program_id(1)

    @pl.when(j == 0)
    def _norm():
        x = _two_source(pl.program_id(0), n_first, xa_ref, xb_ref)
        ms = jnp.mean(x * x, axis=-1, keepdims=True)
        h_scr[...] = (x * lax.rsqrt(ms + RMS_EPS) * g_ref[...]).astype(BF16)

    acc = jnp.dot(h_scr[...], w_ref[...], preferred_element_type=F32) + b_ref[...]

    @pl.when(j >= _GATE_TILE0)
    def _gate():
        o_ref[...] = _sigmoid(acc)

    @pl.when(j < _ROPE_TILE0)
    def _plain():
        o_ref[...] = acc

    @pl.when(jnp.logical_and(j >= _ROPE_TILE0, j <= _KV_TILE))
    def _rope():
        c = jnp.concatenate([cos_ref[...]] * (TN1 // 128), axis=1)
        s = jnp.concatenate([sin_ref[...]] * (TN1 // 128), axis=1)
        lane = lax.broadcasted_iota(I32, acc.shape, 1)
        first_half = (lane & (HEAD_DIM // 2)) == 0
        partner = jnp.where(first_half, pltpu.roll(acc, TN1 - HEAD_DIM // 2, 1), pltpu.roll(acc, HEAD_DIM // 2, 1))
        roped = acc * c + partner * s
        n_rot = jnp.where(j == _KV_TILE, D_KV, TN1)
        o_ref[...] = jnp.where(lane < n_rot, roped, acc)


def _inproj(xa, xb, g, w_bf, b, cos_t, sin_t):
    assert xa.shape[0] % TM1 == 0 and xb.shape[0] % TM1 == 0
    m = xa.shape[0] + xb.shape[0]
    n_first = xa.shape[0] // TM1
    return pl.pallas_call(
        functools.partial(_inproj_kernel, n_first=n_first),
        grid=(m // TM1, N_IN // TN1),
        in_specs=_two_source_specs(TM1, D_MODEL, n_first) + [
            pl.BlockSpec((1, D_MODEL), lambda i, j: (0, 0)),
            pl.BlockSpec((D_MODEL, TN1), lambda i, j: (0, j)),
            pl.BlockSpec((1, TN1), lambda i, j: (0, j)),
            pl.BlockSpec((TM1, 128), lambda i, j: (i, 0)),
            pl.BlockSpec((TM1, 128), lambda i, j: (i, 0)),
        ],
        out_specs=pl.BlockSpec((TM1, TN1), lambda i, j: (i, j)),
        out_shape=jax.ShapeDtypeStruct((m, N_IN), F32),
        scratch_shapes=[pltpu.VMEM((TM1, D_MODEL), BF16)],
        compiler_params=_cparams(("parallel", "arbitrary")),
        name="inproj",
    )(xa, xb, g, w_bf, b, cos_t, sin_t)


SSM_GB = 8


def _ssm_operators(a_re, a_im, log_dt, b_re, b_im, c_re, c_im, d_skip, chunk, w_dtype):
    a_re = a_re.astype(F32)
    a_im = a_im.astype(F32)
    dt = jnp.exp(log_dt.astype(F32))[:, None]
    mag = jnp.exp(a_re * dt)
    lb_re = mag * jnp.cos(a_im * dt)
    lb_im = mag * jnp.sin(a_im * dt)
    inv = 1.0 / (a_re * a_re + a_im * a_im)
    f_re = ((lb_re - 1.0) * a_re + lb_im * a_im) * inv
    f_im = (lb_im * a_re - (lb_re - 1.0) * a_im) * inv
    g, p, h = b_re.shape
    nlb = g // SSM_GB
    sw = SSM_GB * p
    br = jnp.transpose(b_re.astype(F32), (0, 2, 1))
    bi = jnp.transpose(b_im.astype(F32), (0, 2, 1))
    bb_re = f_re[:, None, :] * br - f_im[:, None, :] * bi
    bb_im = f_re[:, None, :] * bi + f_im[:, None, :] * br
    k = jnp.arange(chunk + 1, dtype=F32)[:, None, None]
    pmag = jnp.exp(a_re * dt * k)
    pw_re = (pmag * jnp.cos(a_im * dt * k))[:, :, None, :]
    pw_im = (pmag * jnp.sin(a_im * dt * k))[:, :, None, :]
    pb_re = pw_re[:chunk] * bb_re[None] - pw_im[:chunk] * bb_im[None]
    pb_im = pw_re[:chunk] * bb_im[None] + pw_im[:chunk] * bb_re[None]
    cr = c_re.astype(F32)
    ci = c_im.astype(F32)
    cv_re = cr[None] * pw_re[1:] - ci[None] * pw_im[1:]
    cv_im = -(cr[None] * pw_im[1:] + ci[None] * pw_re[1:])

    def lane_block_rows(x):
        return jnp.transpose(x.reshape(x.shape[0], nlb, SSM_GB * h, x.shape[-1]), (1, 0, 2, 3))

    def same_group(rows_per, cols_per, n_rows, n_cols):
        return (jnp.arange(n_rows)[:, None] // rows_per == jnp.arange(n_cols)[None, :] // cols_per).astype(F32)
    spread = jnp.tile(jnp.eye(p, dtype=F32), (1, SSM_GB))

    def block_diag_states(x, dtype):
        y = jnp.einsum('blrp,pc->blrc', lane_block_rows(x), spread, precision=HIGHEST) * same_group(h, p, 128, sw)
        return y.reshape(nlb, chunk * 128, sw).astype(dtype)
    c_cat = jnp.concatenate([cr.reshape(nlb, 128, p), -ci.reshape(nlb, 128, p)], axis=-1)
    pb_cat = jnp.concatenate([lane_block_rows(pb_re), lane_block_rows(pb_im)], axis=-1)
    lag_blocks = jnp.einsum('blrq,bcq->blrc', pb_cat, c_cat, precision=HIGHEST) * same_group(h, h, 128, 128)
    d_row = jnp.tile(d_skip.astype(F32).reshape(nlb, 1, 128), (1, 1, chunk))
    return dict(lag_blocks=lag_blocks.astype(BF16),
                w_re=block_diag_states(pb_re[::-1], w_dtype), w_im=block_diag_states(pb_im[::-1], w_dtype),
                v_re=block_diag_states(cv_re, BF16), v_im=block_diag_states(cv_im, BF16),
                lbl_re=pw_re[chunk].reshape(1, g * p), lbl_im=pw_im[chunk].reshape(1, g * p), d_row=d_row)


def _ssm_kernel(*refs, chunk, n_chunks, n_seq, aliased):
    (u_ref, t_ref, wr_ref, wi_ref, vr_ref, vi_ref, lr_ref, li_ref, d_ref, s0r_ref, s0i_ref) = refs[:11]
    y_ref, sr_ref, si_ref, z_scr, t_scr, lor_scr, loi_scr, pr_scr, pi_scr = refs[11 + aliased:]
    rows = n_seq * n_chunks
    for j in range(chunk):
        for t in range(chunk):
            blk = t_ref[0, t - j] if t >= j else jnp.zeros((128, 128), BF16)
            t_scr[j * 128:(j + 1) * 128, t * 128:(t + 1) * 128] = blk
    for j in range(chunk):
        z_scr[:, j * 128:(j + 1) * 128] = u_ref[pl.ds(j, rows, stride=chunk), :]
    z = z_scr[...]
    zw = z.astype(wr_ref.dtype)
    prec = HIGHEST if wr_ref.dtype == F32 else None
    lor = jnp.dot(zw, wr_ref[0], preferred_element_type=F32, precision=prec)
    loi = jnp.dot(zw, wi_ref[0], preferred_element_type=F32, precision=prec)
    lr = lr_ref[...]
    li = li_ref[...]
    sr = s0r_ref[...]
    si = s0i_ref[...]
    if n_chunks == 1:
        pr, pi = sr, si
        sr, si = lr * sr - li * si + lor, lr * si + li * sr + loi
    else:
        pieces = range(lor.shape[1] // 128)
        cut = lambda a: tuple(a[:, k * 128:(k + 1) * 128] for k in pieces)
        for k in pieces:
            lor_scr[k] = lor[:, k * 128:(k + 1) * 128]
            loi_scr[k] = loi[:, k * 128:(k + 1) * 128]
        lrs, lis = cut(lr), cut(li)

        def step(c, carry):
            idx = pl.ds(c, n_seq, stride=n_chunks)
            nxt = []
            for k, (a, b) in enumerate(zip(*carry)):
                pr_scr[k, idx, :] = a
                pi_scr[k, idx, :] = b
                nxt.append((lrs[k] * a - lis[k] * b + lor_scr[k, idx, :], lrs[k] * b + lis[k] * a + loi_scr[k, idx, :]))
            return tuple(n[0] for n in nxt), tuple(n[1] for n in nxt)
        srs, sis = lax.fori_loop(0, n_chunks, step, (cut(sr), cut(si)), unroll=8)
        sr, si = jnp.concatenate(srs, axis=1), jnp.concatenate(sis, axis=1)
        pr = jnp.concatenate([pr_scr[k] for k in pieces], axis=1)
        pi = jnp.concatenate([pi_scr[k] for k in pieces], axis=1)
    sr_ref[...] = sr
    si_ref[...] = si
    y = jnp.dot(z.astype(BF16), t_scr[...], preferred_element_type=F32)
    y += _qk(pr.astype(BF16), vr_ref[0])
    y += _qk(pi.astype(BF16), vi_ref[0])
    y += z * d_ref[0]
    for j in range(chunk):
        y_ref[pl.ds(j, rows, stride=chunk), :] = y[:, j * 128:(j + 1) * 128]


def _ssm(proj, row0, m_out, n_seq, n_chunks, chunk, ops, s0_re, s0_im, y_prev=None):
    rows = n_seq * n_chunks
    tok = rows * chunk
    nlb = SSM_GROUPS // SSM_GB
    width = chunk * 128
    sw = SSM_GB * SSM_STATE
    blk0 = row0 // tok
    per_lb = lambda a, b: pl.BlockSpec((1, a, b), lambda i: (i, 0, 0))
    lanes = lambda r, w: pl.BlockSpec((r, w), lambda i: (0, i))
    in_specs = [pl.BlockSpec((tok, 128), lambda i: (blk0, COL_U // 128 + i)),
                pl.BlockSpec((1, chunk, 128, 128), lambda i: (i, 0, 0, 0)),
                per_lb(width, sw), per_lb(width, sw), per_lb(width, sw), per_lb(width, sw),
                lanes(1, sw), lanes(1, sw), per_lb(1, width), lanes(n_seq, sw), lanes(n_seq, sw)]
    args = [proj, ops['lag_blocks'], ops['w_re'], ops['w_im'], ops['v_re'], ops['v_im'],
            ops['lbl_re'], ops['lbl_im'], ops['d_row'], s0_re, s0_im]
    aliases = {}
    if y_prev is not None:
        in_specs.append(pl.BlockSpec(memory_space=pl.ANY))
        args.append(y_prev)
        aliases = {len(args) - 1: 0}
    state = jax.ShapeDtypeStruct((n_seq, SSM_GROUPS * SSM_STATE), F32)
    return pl.pallas_call(
        functools.partial(_ssm_kernel, chunk=chunk, n_chunks=n_chunks, n_seq=n_seq, aliased=int(y_prev is not None)),
        grid=(nlb,),
        in_specs=in_specs,
        out_specs=[pl.BlockSpec((tok, 128), lambda i: (blk0, i)), lanes(n_seq, sw), lanes(n_seq, sw)],
        out_shape=[jax.ShapeDtypeStruct((m_out, D_SSM), F32), state, state],
        scratch_shapes=[pltpu.VMEM((rows, width), F32), pltpu.VMEM((width, width), BF16)]
        + [pltpu.VMEM((sw // 128, rows, 128), F32)] * 4,
        input_output_aliases=aliases,
        compiler_params=_cparams(("parallel",)),
        name=f"ssm_c{n_chunks}",
    )(*args)


def _softmax_sink_pv(parts, sink):
    masked = [jnp.where(valid, s, NEG_INF) for s, valid, _ in parts]
    m = sink
    for s in masked:
        m = jnp.maximum(m, jnp.max(s, axis=-1, keepdims=True))
    denom = jnp.exp(sink - m)
    out = None
    for s, (_, _, v) in zip(masked, parts):
        e = jnp.exp(s - m)
        denom = denom + jnp.sum(e, axis=-1, keepdims=True)
        pv = jnp.dot(e.astype(BF16), v, preferred_element_type=F32)
        out = pv if out is None else out + pv
    return out / denom


ATTN_STACK = 4


def _attn_prompt_kernel(q_ref, kp_ref, kc_ref, vp_ref, vc_ref, sink_ref, o_ref):
    qb = pl.program_id(1)
    rows = ATTN_STACK * WINDOW
    t = lax.broadcasted_iota(I32, (rows, 2 * WINDOW), 0) & (WINDOW - 1)
    col = lax.broadcasted_iota(I32, (rows, 2 * WINDOW), 1)
    valid = jnp.logical_or(jnp.logical_and(jnp.logical_and(col < WINDOW, col > t), qb > 0),
                           jnp.logical_and(col >= WINDOW, (col - WINDOW) <= t))
    scale = HEAD_DIM ** -0.5
    for kv in range(N_KV_HEADS):
        lanes = slice(kv * HEAD_DIM, (kv + 1) * HEAD_DIM)
        k = jnp.concatenate([kp_ref[:, lanes], kc_ref[:, lanes]], axis=0).astype(BF16)
        v = jnp.concatenate([vp_ref[:, lanes], vc_ref[:, lanes]], axis=0).astype(BF16)
        for h0 in range(kv * Q_PER_KV, (kv + 1) * Q_PER_KV, ATTN_STACK):
            heads = range(h0, h0 + ATTN_STACK)
            q = jnp.concatenate([q_ref[:, h * HEAD_DIM:(h + 1) * HEAD_DIM] for h in heads], axis=0).astype(BF16)
            sink = jnp.concatenate(
                [jnp.broadcast_to(sink_ref[0:1, h:h + 1], (WINDOW, 1)) for h in heads], axis=0)
            s = _qk(q, k) * scale
            o = _softmax_sink_pv([(s, valid, v)], sink)
            for i, h in enumerate(heads):
                o_ref[:, h * HEAD_DIM:(h + 1) * HEAD_DIM] = o[i * WINDOW:(i + 1) * WINDOW, :]


def _attn_prompt(proj, sinks, n_batch, seq, m_out):
    nb = seq // WINDOW
    cq, ck, cv = COL_Q // D_ATTN, COL_K // D_KV, COL_V // D_KV
    cur = lambda col: (lambda n, b: (n * nb + b, col))
    prev = lambda col: (lambda n, b: (n * nb + jnp.maximum(b - 1, 0), col))
    return pl.pallas_call(
        _attn_prompt_kernel,
        grid=(n_batch, nb),
        in_specs=[
            pl.BlockSpec((WINDOW, D_ATTN), cur(cq)),
            pl.BlockSpec((WINDOW, D_KV), prev(ck)),
            pl.BlockSpec((WINDOW, D_KV), cur(ck)),
            pl.BlockSpec((WINDOW, D_KV), prev(cv)),
            pl.BlockSpec((WINDOW, D_KV), cur(cv)),
            pl.BlockSpec((1, N_HEADS), lambda n, b: (0, 0)),
        ],
        out_specs=pl.BlockSpec((WINDOW, D_ATTN), lambda n, b: (n * nb + b, 0)),
        out_shape=jax.ShapeDtypeStruct((m_out, D_ATTN), F32),
        compiler_params=_cparams(("parallel", "arbitrary")),
        name="attn_prompt",
    )(proj, proj, proj, proj, proj, sinks)


SAMPLE_GB = 8


def _attn_sample_kernel(q_ref, kn_ref, vn_ref, kc_ref, vc_ref, sink_ref, o_ref, *, t_new):
    tok = SAMPLE_GB * t_new
    rows = Q_PER_KV * tok
    ncache = SAMPLE_GB * WINDOW
    r_c = lax.broadcasted_iota(I32, (rows, ncache), 0)
    c_c = lax.broadcasted_iota(I32, (rows, ncache), 1)
    rb_c = (r_c % tok) // t_new
    rt_c = r_c % t_new
    valid_c = jnp.logical_and(rb_c == c_c // WINDOW, (c_c % WINDOW) > rt_c)
    r_n = lax.broadcasted_iota(I32, (rows, tok), 0)
    c_n = lax.broadcasted_iota(I32, (rows, tok), 1)
    valid_n = jnp.logical_and((r_n % tok) // t_new == c_n // t_new, (c_n % t_new) <= (r_n % t_new))
    scale = HEAD_DIM ** -0.5
    for kv in range(N_KV_HEADS):
        lanes = slice(kv * HEAD_DIM, (kv + 1) * HEAD_DIM)
        kc = kc_ref[:, lanes].astype(BF16)
        vc = vc_ref[:, lanes].astype(BF16)
        kn = kn_ref[:, lanes].astype(BF16)
        vn = vn_ref[:, lanes].astype(BF16)
        heads = [kv * Q_PER_KV + i for i in range(Q_PER_KV)]
        q = jnp.concatenate([q_ref[:, h * HEAD_DIM:(h + 1) * HEAD_DIM] for h in heads], axis=0).astype(BF16)
        sink = jnp.concatenate(
            [jnp.broadcast_to(sink_ref[0:1, h:h + 1], (tok, 1)) for h in heads], axis=0)
        o = _softmax_sink_pv([(_qk(q, kc) * scale, valid_c, vc), (_qk(q, kn) * scale, valid_n, vn)], sink)
        for i, h in enumerate(heads):
            o_ref[:, h * HEAD_DIM:(h + 1) * HEAD_DIM] = o[i * tok:(i + 1) * tok, :]


def _attn_sample_aliased(q_ref, kn_ref, vn_ref, kc_ref, vc_ref, sink_ref, o_prev, o_ref, *, t_new):
    del o_prev
    _attn_sample_kernel(q_ref, kn_ref, vn_ref, kc_ref, vc_ref, sink_ref, o_ref, t_new=t_new)


def _attn_sample(proj, row0, cache_k, cache_v, sinks, n_batch, t_new, o_prev):
    tok = SAMPLE_GB * t_new
    blk0 = row0 // tok
    cq, ck, cv = COL_Q // D_ATTN, COL_K // D_KV, COL_V // D_KV
    new = lambda col: (lambda g: (blk0 + g, col))
    return pl.pallas_call(
        functools.partial(_attn_sample_aliased, t_new=t_new),
        grid=(n_batch // SAMPLE_GB,),
        in_specs=[
            pl.BlockSpec((tok, D_ATTN), new(cq)),
            pl.BlockSpec((tok, D_KV), new(ck)),
            pl.BlockSpec((tok, D_KV), new(cv)),
            pl.BlockSpec((SAMPLE_GB * WINDOW, D_KV), lambda g: (g, 0)),
            pl.BlockSpec((SAMPLE_GB * WINDOW, D_KV), lambda g: (g, 0)),
            pl.BlockSpec((1, N_HEADS), lambda g: (0, 0)),
            pl.BlockSpec(memory_space=pl.ANY),
        ],
        out_specs=pl.BlockSpec((tok, D_ATTN), lambda g: (blk0 + g, 0)),
        out_shape=jax.ShapeDtypeStruct(o_prev.shape, F32),
        input_output_aliases={6: 0},
        compiler_params=_cparams(("parallel",)),
        name="attn_sample",
    )(proj, proj, proj, cache_k, cache_v, sinks, o_prev)


TM4 = 256
ROUTER_LANES = 128
GATE_W = 512
GATE_BLOCKS = D_MODEL // GATE_W
ROW_TILES = D_MODEL // 128


def _gelu_tanh(x):
    return 0.5 * x * (1.0 + jnp.tanh(0.7978845608028654 * (x + 0.044715 * (x * x * x))))


def _merge_kernel(y_ref, o_ref, *refs, n_first):
    gs_refs, ga_refs = refs[:GATE_BLOCKS], refs[GATE_BLOCKS:2 * GATE_BLOCKS]
    (xa_ref, xb_ref, wglu_ref, bglu_ref, wbs_ref, wba_ref, wout_ref, g2_ref, wr_ref, wrl_ref, br_ref,
     x1_ref, h2_ref, ti_ref, tg_ref, rank_ref, cnt_ref, seen_scr) = refs[2 * GATE_BLOCKS:]

    @pl.when(pl.program_id(0) == 0)
    def _reset():
        seen_scr[...] = jnp.zeros(seen_scr.shape, F32)

    y1 = _gelu_tanh(y_ref[...])
    z = jnp.dot(y1.astype(BF16), wglu_ref[...], preferred_element_type=F32) + bglu_ref[...]
    y2 = y1 * _sigmoid(z)
    bs = jnp.dot(y2.astype(BF16), wbs_ref[...], preferred_element_type=F32)
    ba = jnp.dot(o_ref[...].astype(BF16), wba_ref[...], preferred_element_type=F32)
    merged = jnp.concatenate(
        [gs[...] * bs[:, c * GATE_W:(c + 1) * GATE_W] + ga[...] * ba[:, c * GATE_W:(c + 1) * GATE_W]
         for c, (gs, ga) in enumerate(zip(gs_refs, ga_refs))], axis=1)
    x = _two_source(pl.program_id(0), n_first, xa_ref, xb_ref)
    x1 = x + jnp.dot(merged.astype(BF16), wout_ref[...], preferred_element_type=F32)
    x1_ref[...] = x1
    ms = jnp.mean(x1 * x1, axis=-1, keepdims=True)
    h2 = x1 * lax.rsqrt(ms + RMS_EPS) * g2_ref[...]
    for s in range(ROW_TILES):
        h2_ref[pl.ds(s, TM4, stride=ROW_TILES), :] = h2[:, s * 128:(s + 1) * 128]
    h_hi = h2.astype(BF16)
    h_lo = (h2 - h_hi.astype(F32)).astype(BF16)
    logits = (jnp.dot(h_hi, wr_ref[...], preferred_element_type=F32)
              + (jnp.dot(h_hi, wrl_ref[...], preferred_element_type=F32)
                 + jnp.dot(h_lo, wr_ref[...], preferred_element_type=F32))) + br_ref[...]
    lane = lax.broadcasted_iota(I32, logits.shape, 1)
    cur = logits
    top_i = jnp.zeros(logits.shape, I32)
    top_v = jnp.zeros(logits.shape, F32)
    chosen = []
    for k in range(TOP_K):
        mx = jnp.max(cur, axis=-1, keepdims=True)
        idx = jnp.min(jnp.where(cur == mx, lane, ROUTER_LANES), axis=-1, keepdims=True)
        top_i = jnp.where(lane == k, idx, top_i)
        top_v = jnp.where(lane == k, mx, top_v)
        chosen.append(lane == idx)
        cur = jnp.where(chosen[-1], -jnp.inf, cur)
    v0 = jnp.max(jnp.where(lane < TOP_K, top_v, -jnp.inf), axis=-1, keepdims=True)
    e = jnp.where(lane < TOP_K, jnp.exp(top_v - v0), 0.0)
    ti_ref[...] = top_i
    tg_ref[...] = e / jnp.sum(e, axis=-1, keepdims=True)
    hits = sum(jnp.where(c, 1.0, 0.0) for c in chosen)
    r = lax.broadcasted_iota(I32, (TM4, TM4), 0)
    c = lax.broadcasted_iota(I32, (TM4, TM4), 1)
    before = jnp.dot(jnp.where(c < r, 1.0, 0.0).astype(BF16), hits.astype(BF16), preferred_element_type=F32)
    before = before + seen_scr[...]
    rank = jnp.zeros(logits.shape, F32)
    for k in range(TOP_K):
        rank = jnp.where(lane == k, jnp.sum(jnp.where(chosen[k], before, 0.0), axis=-1, keepdims=True), rank)
    rank_ref[...] = rank.astype(I32)
    seen_scr[...] = seen_scr[...] + jnp.sum(hits, axis=0, keepdims=True)
    cnt_ref[...] = seen_scr[...].astype(I32)


def _merge(y_ssm, o_attn, proj, xa, xb, wglu, bglu, wbs, wba, wout, g2, wr_hi, wr_lo, br):
    assert xa.shape[0] % TM4 == 0 and xb.shape[0] % TM4 == 0
    m = xa.shape[0] + xb.shape[0]
    n_first = xa.shape[0] // TM4
    row = lambda w: pl.BlockSpec((TM4, w), lambda i: (i, 0))
    const = lambda a, b: pl.BlockSpec((a, b), lambda i: (0, 0), pipeline_mode=pl.Buffered(1))
    return pl.pallas_call(
        functools.partial(_merge_kernel, n_first=n_first),
        grid=(m // TM4,),
        in_specs=[
            row(D_SSM), row(D_ATTN),
            *[pl.BlockSpec((TM4, GATE_W), lambda i, c=col // GATE_W + k: (i, c))
              for col in (COL_GS, COL_GA) for k in range(GATE_BLOCKS)],
            *_two_source_specs(TM4, D_MODEL, n_first),
            const(D_SSM, D_SSM), const(1, D_SSM), const(D_SSM, D_MODEL), const(D_ATTN, D_MODEL),
            const(D_MODEL, D_MODEL), const(1, D_MODEL), const(D_MODEL, ROUTER_LANES), const(D_MODEL, ROUTER_LANES),
            const(1, ROUTER_LANES),
        ],
        out_specs=[row(D_MODEL), pl.BlockSpec((TM4 * ROW_TILES, 128), lambda i: (i, 0)),
                   row(ROUTER_LANES), row(ROUTER_LANES), row(ROUTER_LANES),
                   pl.BlockSpec((1, ROUTER_LANES), lambda i: (0, 0))],
        out_shape=[jax.ShapeDtypeStruct((m, D_MODEL), F32), jax.ShapeDtypeStruct((m * ROW_TILES, 128), F32),
                   jax.ShapeDtypeStruct((m, ROUTER_LANES), I32), jax.ShapeDtypeStruct((m, ROUTER_LANES), F32),
                   jax.ShapeDtypeStruct((m, ROUTER_LANES), I32), jax.ShapeDtypeStruct((1, ROUTER_LANES), I32)],
        scratch_shapes=[pltpu.VMEM((1, ROUTER_LANES), F32)],
        compiler_params=_cparams(("arbitrary",)),
        name="merge",
    )(y_ssm, o_attn, *([proj] * (2 * GATE_BLOCKS)), xa, xb, wglu, bglu, wbs, wba, wout, g2, wr_hi, wr_lo, br)


SUB = 128
SUPER = 1280
TF = 256
NF = D_FF // TF
MOE_WIDE = 8


def _moe_sizes(m):
    a = m * TOP_K
    ns_max = N_EXPERTS + a // SUPER
    nb_max = N_EXPERTS + a // SUB
    return a, ns_max, nb_max


def _route(top_i, rank, counts, m):
    a, ns_max, nb_max = _moe_sizes(m)
    experts = jnp.arange(N_EXPERTS, dtype=I32)
    nsb = (counts + SUPER - 1) // SUPER
    sb_end = jnp.cumsum(nsb)
    sb_start = sb_end - nsb
    nb = (counts + SUB - 1) // SUB
    b_end = jnp.cumsum(nb)
    n_c = nb_max * SUB
    cslot = jnp.sum(jnp.where(top_i[:, :, None] == experts, (b_end - nb) * SUB, 0), axis=-1) + rank
    row_of = jnp.arange(TOP_K, dtype=I32)[None, :] * m + jnp.arange(m, dtype=I32)[:, None]
    dest = (a + jnp.arange(n_c, dtype=I32)).at[cslot.reshape(a)].set(row_of.reshape(a), unique_indices=True)
    tok = jnp.minimum(dest, a - 1) % m
    s_idx = jnp.arange(ns_max, dtype=I32)
    ns_used = sb_end[-1]
    s_eff = jnp.maximum(jnp.minimum(s_idx, ns_used - 1), 0)
    se = jnp.minimum(jnp.sum((s_eff[:, None] >= sb_end[None, :]).astype(I32), axis=1), N_EXPERTS - 1)
    of_se = lambda v: jnp.sum(jnp.where(se[:, None] == experts, v, 0), axis=-1)
    rows_in = jnp.clip(of_se(counts) - (s_eff - of_se(sb_start)) * SUPER, 0, SUPER)
    nsub = jnp.where(s_idx < ns_used, (rows_in + SUB - 1) // SUB, 0).astype(I32)
    cstart = (of_se(b_end - nb) * SUB + (s_eff - of_se(sb_start)) * SUPER).astype(I32)
    b_idx = jnp.arange(nb_max, dtype=I32)
    eb = jnp.minimum(jnp.sum((b_idx[:, None] >= b_end[None, :]).astype(I32), axis=1), N_EXPERTS - 1)
    of_eb = lambda v: jnp.sum(jnp.where(eb[:, None] == experts, v, 0), axis=-1)
    ib = b_idx - of_eb(b_end - nb)
    dst = jnp.where(b_idx < b_end[-1], of_eb(sb_start) * (SUPER // SUB) + ib, ns_max * (SUPER // SUB)).astype(I32)
    return dict(se=se.astype(I32), nsub=nsub, xblk=s_eff.astype(I32), cstart=cstart, ns_used=ns_used.reshape(1).astype(I32),
                dst=dst, tok=tok.astype(I32), dest=dest)


def _row_copy(src_hbm, row, dst_buf, slot, r, sem):
    return pltpu.make_async_copy(src_hbm.at[pl.ds(pl.multiple_of(row * ROW_TILES, ROW_TILES), ROW_TILES), :],
                                 dst_buf.at[slot, pl.ds(pl.multiple_of(r * ROW_TILES, ROW_TILES), ROW_TILES), :],
                                 sem.at[slot])


def _gather_kernel(dst_ref, tok_ref, h_hbm, o_ref, buf, sem):
    b = pl.program_id(0)
    nb = pl.num_programs(0)

    def issue(blk, slot):
        def body(r, c):
            _row_copy(h_hbm, tok_ref[blk * SUB + r], buf, slot, r, sem).start()
            return c
        lax.fori_loop(0, SUB, body, 0, unroll=16)

    @pl.when(b == 0)
    def _first():
        issue(0, 0)

    @pl.when(b + 1 < nb)
    def _next():
        issue(b + 1, (b + 1) % 2)

    slot = b % 2

    def wait_body(r, c):
        _row_copy(h_hbm, 0, buf, slot, r, sem).wait()
        return c
    lax.fori_loop(0, SUB, wait_body, 0, unroll=16)
    for s in range(ROW_TILES):
        o_ref[:, s * 128:(s + 1) * 128] = buf[slot, pl.ds(s, SUB, stride=ROW_TILES), :].astype(BF16)


def _gather(h2, route, m):
    _, ns_max, nb_max = _moe_sizes(m)
    rows = ns_max * SUPER + SUB
    return pl.pallas_call(
        _gather_kernel,
        grid_spec=pltpu.PrefetchScalarGridSpec(
            num_scalar_prefetch=2,
            grid=(nb_max,),
            in_specs=[pl.BlockSpec(memory_space=pl.ANY)],
            out_specs=pl.BlockSpec((SUB, D_MODEL), lambda b, dst, tok: (dst[b], 0)),
            scratch_shapes=[pltpu.VMEM((2, SUB * ROW_TILES, 128), F32), pltpu.SemaphoreType.DMA((2,))],
        ),
        out_shape=jax.ShapeDtypeStruct((rows, D_MODEL), BF16),
        compiler_params=_cparams(("arbitrary",)),
        name="moe_gather",
    )(route['dst'], route['tok'], h2)


_COPY_UNROLL = 16


def _moe_kernel(se_ref, nsub_ref, xb_ref, cstart_ref, nused_ref, dest_ref,
                x_ref, wg_ref, wl_ref, bg_ref, bl_ref, wd_ref, bd_ref, y_hbm, wg_s, wl_s, wd_s, acc, sem):
    s = pl.program_id(0)
    f = pl.program_id(1)
    n = nsub_ref[s]

    def result_rows(sb, wait):
        sl = sb % 2
        base = cstart_ref[sb]

        def body(g, c):
            for u in range(_COPY_UNROLL):
                r = g * _COPY_UNROLL + u
                row = 0 if wait else dest_ref[base + r]
                cp = pltpu.make_async_copy(acc.at[sl, pl.ds(r, 1), :], y_hbm.at[pl.ds(row, 1), :], sem.at[sl])
                cp.wait() if wait else cp.start()
            return c
        lax.fori_loop(0, nsub_ref[sb] * (SUB // _COPY_UNROLL), body, 0)

    @pl.when(n > 0)
    def _work():
        o_ref = acc.at[s % 2]
        wg_s[...] = wg_ref[0].astype(BF16)
        wl_s[...] = wl_ref[0].astype(BF16)
        wd_s[...] = wd_ref[0].astype(BF16)

        @pl.when(f == 0)
        def _init():
            @pl.when(s >= 2)
            def _free_slot():
                result_rows(s - 2, True)
            o_ref[...] = jnp.broadcast_to(bd_ref[0], o_ref.shape)

        bg = bg_ref[0]
        bl = bl_ref[0]

        def expert_rows(r0, rows):
            xs = x_ref[pl.ds(r0, rows), :]
            hg = jnp.dot(xs, wg_s[...], preferred_element_type=F32) + bg
            hl = jnp.dot(xs, wl_s[...], preferred_element_type=F32) + bl
            hg = jnp.minimum(hg, SWIGLU_LIMIT)
            hl = jnp.clip(hl, -SWIGLU_LIMIT, SWIGLU_LIMIT)
            act = hg * _sigmoid(SWIGLU_ALPHA * hg) * (hl + 1.0)
            o_ref[pl.ds(r0, rows), :] += jnp.dot(act.astype(BF16), wd_s[...], preferred_element_type=F32)

        def wide_body(i, c):
            expert_rows(pl.multiple_of(i * (MOE_WIDE * SUB), MOE_WIDE * SUB), MOE_WIDE * SUB)
            return c
        n_wide = n // MOE_WIDE
        lax.fori_loop(0, n_wide, wide_body, 0)

        def single_body(i, c):
            expert_rows(pl.multiple_of(i * SUB, SUB), SUB)
            return c
        lax.fori_loop(n_wide * MOE_WIDE, n, single_body, 0)

        @pl.when(f == NF - 1)
        def _send():
            result_rows(s, False)

    @pl.when(jnp.logical_and(s == pl.num_programs(0) - 1, f == NF - 1))
    def _drain():
        n_used = nused_ref[0]

        @pl.when(n_used >= 2)
        def _():
            result_rows(n_used - 2, True)
        result_rows(n_used - 1, True)


def _moe(x_sorted, route, w_up, b_up, w_down, b_down, m):
    a, ns_max, nb_max = _moe_sizes(m)
    sp = lambda shape, imap: pl.BlockSpec(shape, lambda s, f, se, ns, xb, cs, nu, de: imap(s, f, se, xb))
    return pl.pallas_call(
        _moe_kernel,
        grid_spec=pltpu.PrefetchScalarGridSpec(
            num_scalar_prefetch=6,
            grid=(ns_max, NF),
            in_specs=[
                sp((SUPER, D_MODEL), lambda s, f, se, xb: (xb[s], 0)),
                sp((1, D_MODEL, TF), lambda s, f, se, xb: (se[s], 0, f)),
                sp((1, D_MODEL, TF), lambda s, f, se, xb: (se[s], 0, NF + f)),
                sp((1, 1, TF), lambda s, f, se, xb: (se[s], 0, f)),
                sp((1, 1, TF), lambda s, f, se, xb: (se[s], 0, NF + f)),
                sp((1, TF, D_MODEL), lambda s, f, se, xb: (se[s], f, 0)),
                sp((1, 1, D_MODEL), lambda s, f, se, xb: (se[s], 0, 0)),
            ],
            out_specs=pl.BlockSpec(memory_space=pl.ANY),
            scratch_shapes=[pltpu.VMEM((D_MODEL, TF), BF16), pltpu.VMEM((D_MODEL, TF), BF16),
                            pltpu.VMEM((TF, D_MODEL), BF16), pltpu.VMEM((2, SUPER, D_MODEL), F32),
                            pltpu.SemaphoreType.DMA((2,))],
        ),
        out_shape=jax.ShapeDtypeStruct((a + nb_max * SUB, D_MODEL), F32),
        compiler_params=_cparams(("arbitrary", "arbitrary")),
        name="moe_experts",
    )(route['se'], route['nsub'], route['xblk'], route['cstart'], route['ns_used'], route['dest'],
      x_sorted, w_up, w_up, b_up, b_up, w_down, b_down)


TC8 = 64


def _combine_kernel(*refs, n_first):
    y_refs = refs[:TOP_K]
    x1_ref, gate_ref, g_ref, o1_ref, o2_ref = refs[TOP_K:]
    b = pl.program_id(0)
    acc = x1_ref[...]
    gate = gate_ref[...]
    for k in range(TOP_K):
        acc = acc + gate[:, k:k + 1] * y_refs[k][...]
    ms = jnp.mean(acc * acc, axis=-1, keepdims=True)
    res = acc * lax.rsqrt(ms + RMS_EPS) * g_ref[...]

    @pl.when(b < n_first)
    def _to_first():
        o1_ref[...] = res

    @pl.when(b >= n_first)
    def _to_second():
        o2_ref[...] = res


def _combine(y_tok, x1, gate, g_final, m_first):
    m = x1.shape[0]
    n_first = m_first // TC8
    return pl.pallas_call(
        functools.partial(_combine_kernel, n_first=n_first),
        grid=(m // TC8,),
        in_specs=[
            *[pl.BlockSpec((TC8, D_MODEL), lambda b, k=k: (k * (m // TC8) + b, 0)) for k in range(TOP_K)],
            pl.BlockSpec((TC8, D_MODEL), lambda b: (b, 0)),
            pl.BlockSpec((TC8, ROUTER_LANES), lambda b: (b, 0)),
            pl.BlockSpec((1, D_MODEL), lambda b: (0, 0)),
        ],
        out_specs=[pl.BlockSpec((TC8, D_MODEL), lambda b: (jnp.minimum(b, n_first - 1), 0)),
                   pl.BlockSpec((TC8, D_MODEL), lambda b: (jnp.maximum(b - n_first, 0), 0))],
        out_shape=[jax.ShapeDtypeStruct((m_first, D_MODEL), F32), jax.ShapeDtypeStruct((m - m_first, D_MODEL), F32)],
        compiler_params=_cparams(("arbitrary",)),
        name="moe_combine",
    )(*([y_tok] * TOP_K), x1, gate, g_final)


SSM_CHUNK = 8


def _rope_tables(seq, t_new, n_batch, n_dec):
    half = HEAD_DIM // 2
    inv = ROPE_THETA ** (-jnp.arange(half, dtype=F32) / half)
    pos = jnp.concatenate([jnp.tile(jnp.arange(seq), n_batch), jnp.tile(PAST_LEN + jnp.arange(t_new), n_dec)])
    ang = pos.astype(F32)[:, None] * inv[None, :]
    cos = jnp.cos(ang)
    sin = jnp.sin(ang)
    return jnp.tile(cos, (1, 4)), jnp.tile(jnp.concatenate([-sin, sin], axis=1), (1, 2))


def kernel(x_prompt, x_sample, state_ssm_re, state_ssm_im, cache_k, cache_v, attn_norm_g, w_in, b_in, ssm_a_re, ssm_a_im, ssm_log_dt, ssm_b_re, ssm_b_im, ssm_c_re, ssm_c_im, ssm_d, w_glu, b_glu, attn_sinks, w_branch_ssm, w_branch_attn, w_out, ffn_norm_g, w_router, b_router, w_up, b_up, w_down, b_down, final_norm_g):
    n_batch, seq, _ = x_prompt.shape
    n_dec, t_new, _ = x_sample.shape
    mp = n_batch * seq
    ms = n_dec * t_new
    m = mp + ms
    g, p, h = SSM_GROUPS, SSM_STATE, SSM_GROUP
    xp = x_prompt.reshape(mp, D_MODEL)
    xs = x_sample.reshape(ms, D_MODEL)

    cos_t, sin_t = _rope_tables(seq, t_new, n_batch, n_dec)
    proj = _inproj(xp, xs, attn_norm_g[0].reshape(1, D_MODEL), w_in[0].astype(BF16), b_in[0].reshape(1, N_IN),
                   cos_t, sin_t)

    ssm_params = (ssm_a_re[0], ssm_a_im[0], ssm_log_dt[0], ssm_b_re[0], ssm_b_im[0], ssm_c_re[0], ssm_c_im[0], ssm_d[0])
    zeros = jnp.zeros((n_batch, g * p), F32)
    y_ssm, sp_re, sp_im = _ssm(proj, 0, m, n_batch, seq // SSM_CHUNK, SSM_CHUNK,
                               _ssm_operators(*ssm_params, SSM_CHUNK, BF16), zeros, zeros)
    y_ssm, ss_re, ss_im = _ssm(proj, mp, m, n_dec, 1, t_new, _ssm_operators(*ssm_params, t_new, F32),
                               state_ssm_re[0].reshape(n_dec, g * p), state_ssm_im[0].reshape(n_dec, g * p), y_prev=y_ssm)

    sinks = attn_sinks[0].reshape(1, N_HEADS)
    o_attn = _attn_prompt(proj, sinks, n_batch, seq, m)
    o_attn = _attn_sample(proj, mp, cache_k[0].reshape(n_dec * WINDOW, D_KV), cache_v[0].reshape(n_dec * WINDOW, D_KV),
                          sinks, n_dec, t_new, o_attn)

    wr = jnp.pad(w_router[0], ((0, 0), (0, ROUTER_LANES - N_EXPERTS)))
    wr_hi = wr.astype(BF16)
    wr_lo = (wr - wr_hi.astype(F32)).astype(BF16)
    br = jnp.pad(b_router[0], (0, ROUTER_LANES - N_EXPERTS), constant_values=NEG_INF).reshape(1, ROUTER_LANES)
    x1, h2, top_i, gate, rank, counts = _merge(
        y_ssm, o_attn, proj, xp, xs, w_glu[0].astype(BF16), b_glu[0].reshape(1, D_SSM), w_branch_ssm[0].astype(BF16),
        w_branch_attn[0].astype(BF16), w_out[0].astype(BF16), ffn_norm_g[0].reshape(1, D_MODEL), wr_hi, wr_lo, br)

    route = _route(top_i[:, :TOP_K], rank[:, :TOP_K], counts[0, :N_EXPERTS], m)
    x_sorted = _gather(h2, route, m)
    y_tok = _moe(x_sorted, route, w_up[0], b_up[0].reshape(N_EXPERTS, 1, 2 * D_FF), w_down[0],
                    b_down[0].reshape(N_EXPERTS, 1, D_MODEL), m)
    y_p, y_s = _combine(y_tok, x1, gate, final_norm_g.reshape(1, D_MODEL), mp)

    y_prompt = y_p.reshape(n_batch, seq, D_MODEL)
    y_sample = y_s.reshape(n_dec, t_new, D_MODEL)
    to_state = lambda s: s.reshape(1, s.shape[0], g, p)
    k_all = proj[:, COL_K:COL_K + D_KV]
    v_all = proj[:, COL_V:COL_V + D_KV]
    kv_p = lambda a: a[:mp].reshape(n_batch, seq, N_KV_HEADS, HEAD_DIM)[:, -WINDOW:][None]
    kv_s = lambda a, c: jnp.concatenate([c[0][:, t_new:], a[mp:].reshape(n_dec, t_new, N_KV_HEADS, HEAD_DIM)], axis=1)[None]
    return (y_prompt, y_sample, to_state(sp_re), to_state(sp_im), kv_p(k_all), kv_p(v_all),
            to_state(ss_re), to_state(ss_im), kv_s(k_all, cache_k), kv_s(v_all, cache_v))
```

```python
import functools

import jax
import jax.numpy as jnp
from jax import lax
from jax.experimental import pallas as pl
from jax.experimental.pallas import tpu as pltpu

F32 = jnp.float32
BF16 = jnp.bfloat16
I32 = jnp.int32
HIGHEST = lax.Precision.HIGHEST

D_MODEL = 2048
RMS_EPS = 1e-5
SSM_GROUP = 16
SSM_GROUPS = 64
SSM_STATE = 64
D_SSM = 1024
HEAD_DIM = 64
N_HEADS = 16
N_KV_HEADS = 4
Q_PER_KV = 4
D_ATTN = 1024
D_KV = 256
WINDOW = 128
ROPE_THETA = 10000.0
NEG_INF = -1e30
PAST_LEN = 16384
N_IN = D_SSM + D_ATTN + 2 * D_KV + 2 * D_MODEL
N_EXPERTS = 32
TOP_K = 4
D_FF = 2048
SWIGLU_ALPHA = 1.702
SWIGLU_LIMIT = 7.0

COL_U, COL_Q, COL_K, COL_V, COL_GS, COL_GA = 0, 1024, 2048, 2304, 2560, 4608

VMEM_LIMIT = 56 * 1024 * 1024


def _sigmoid(x):
    return 1.0 / (1.0 + jnp.exp(-x))


def _qk(q, k):
    return lax.dot_general(q, k, (((1,), (1,)), ((), ())), preferred_element_type=F32)


def _cparams(sem, vmem=VMEM_LIMIT):
    return pltpu.CompilerParams(dimension_semantics=sem, vmem_limit_bytes=vmem)


TM1 = 512
TN1 = 512
_ROPE_TILE0 = COL_Q // TN1
_KV_TILE = COL_K // TN1
_GATE_TILE0 = COL_GS // TN1


def _two_source(i, n_first, a_ref, b_ref):
    return jnp.where(i < n_first, a_ref[...], b_ref[...])


def _two_source_specs(tile, width, n_first):
    return [pl.BlockSpec((tile, width), lambda i, *_: (jnp.minimum(i, n_first - 1), 0)),
            pl.BlockSpec((tile, width), lambda i, *_: (jnp.maximum(i - n_first, 0), 0))]


def _inproj_kernel(xa_ref, xb_ref, g_ref, w_ref, b_ref, cos_ref, sin_ref, o_ref, h_scr, *, n_first):
    j = pl.program_id(1)

    @pl.when(j == 0)
    def _norm():
        x = _two_source(pl.program_id(0), n_first, xa_ref, xb_ref)
        ms = jnp.mean(x * x, axis=-1, keepdims=True)
        h_scr[...] = (x * lax.rsqrt(ms + RMS_EPS) * g_ref[...]).astype(BF16)

    acc = jnp.dot(h_scr[...], w_ref[...], preferred_element_type=F32) + b_ref[...]

    @pl.when(j >= _GATE_TILE0)
    def _gate():
        o_ref[...] = _sigmoid(acc)

    @pl.when(j < _ROPE_TILE0)
    def _plain():
        o_ref[...] = acc

    @pl.when(jnp.logical_and(j >= _ROPE_TILE0, j <= _KV_TILE))
    def _rope():
        c = jnp.concatenate([cos_ref[...]] * (TN1 // 128), axis=1)
        s = jnp.concatenate([sin_ref[...]] * (TN1 // 128), axis=1)
        lane = lax.broadcasted_iota(I32, acc.shape, 1)
        first_half = (lane & (HEAD_DIM // 2)) == 0
        partner = jnp.where(first_half, pltpu.roll(acc, TN1 - HEAD_DIM // 2, 1), pltpu.roll(acc, HEAD_DIM // 2, 1))
        roped = acc * c + partner * s
        n_rot = jnp.where(j == _KV_TILE, D_KV, TN1)
        o_ref[...] = jnp.where(lane < n_rot, roped, acc)


def _inproj(xa, xb, g, w_bf, b, cos_t, sin_t):
    assert xa.shape[0] % TM1 == 0 and xb.shape[0] % TM1 == 0
    m = xa.shape[0] + xb.shape[0]
    n_first = xa.shape[0] // TM1
    return pl.pallas_call(
        functools.partial(_inproj_kernel, n_first=n_first),
        grid=(m // TM1, N_IN // TN1),
        in_specs=_two_source_specs(TM1, D_MODEL, n_first) + [
            pl.BlockSpec((1, D_MODEL), lambda i, j: (0, 0)),
            pl.BlockSpec((D_MODEL, TN1), lambda i, j: (0, j)),
            pl.BlockSpec((1, TN1), lambda i, j: (0, j)),
            pl.BlockSpec((TM1, 128), lambda i, j: (i, 0)),
            pl.BlockSpec((TM1, 128), lambda i, j: (i, 0)),
        ],
        out_specs=pl.BlockSpec((TM1, TN1), lambda i, j: (i, j)),
        out_shape=jax.ShapeDtypeStruct((m, N_IN), F32),
        scratch_shapes=[pltpu.VMEM((TM1, D_MODEL), BF16)],
        compiler_params=_cparams(("parallel", "arbitrary")),
        name="inproj",
    )(xa, xb, g, w_bf, b, cos_t, sin_t)


SSM_GB = 8


def _ssm_operators(a_re, a_im, log_dt, b_re, b_im, c_re, c_im, d_skip, chunk, w_dtype):
    a_re = a_re.astype(F32)
    a_im = a_im.astype(F32)
    dt = jnp.exp(log_dt.astype(F32))[:, None]
    mag = jnp.exp(a_re * dt)
    lb_re = mag * jnp.cos(a_im * dt)
    lb_im = mag * jnp.sin(a_im * dt)
    inv = 1.0 / (a_re * a_re + a_im * a_im)
    f_re = ((lb_re - 1.0) * a_re + lb_im * a_im) * inv
    f_im = (lb_im * a_re - (lb_re - 1.0) * a_im) * inv
    g, p, h = b_re.shape
    nlb = g // SSM_GB
    sw = SSM_GB * p
    br = jnp.transpose(b_re.astype(F32), (0, 2, 1))
    bi = jnp.transpose(b_im.astype(F32), (0, 2, 1))
    bb_re = f_re[:, None, :] * br - f_im[:, None, :] * bi
    bb_im = f_re[:, None, :] * bi + f_im[:, None, :] * br
    k = jnp.arange(chunk + 1, dtype=F32)[:, None, None]
    pmag = jnp.exp(a_re * dt * k)
    pw_re = (pmag * jnp.cos(a_im * dt * k))[:, :, None, :]
    pw_im = (pmag * jnp.sin(a_im * dt * k))[:, :, None, :]
    pb_re = pw_re[:chunk] * bb_re[None] - pw_im[:chunk] * bb_im[None]
    pb_im = pw_re[:chunk] * bb_im[None] + pw_im[:chunk] * bb_re[None]
    cr = c_re.astype(F32)
    ci = c_im.astype(F32)
    cv_re = cr[None] * pw_re[1:] - ci[None] * pw_im[1:]
    cv_im = -(cr[None] * pw_im[1:] + ci[None] * pw_re[1:])

    def lane_block_rows(x):
        return jnp.transpose(x.reshape(x.shape[0], nlb, SSM_GB * h, x.shape[-1]), (1, 0, 2, 3))

    def same_group(rows_per, cols_per, n_rows, n_cols):
        return (jnp.arange(n_rows)[:, None] // rows_per == jnp.arange(n_cols)[None, :] // cols_per).astype(F32)
    spread = jnp.tile(jnp.eye(p, dtype=F32), (1, SSM_GB))

    def block_diag_states(x, dtype):
        y = jnp.einsum('blrp,pc->blrc', lane_block_rows(x), spread, precision=HIGHEST) * same_group(h, p, 128, sw)
        return y.reshape(nlb, chunk * 128, sw).astype(dtype)
    c_cat = jnp.concatenate([cr.reshape(nlb, 128, p), -ci.reshape(nlb, 128, p)], axis=-1)
    pb_cat = jnp.concatenate([lane_block_rows(pb_re), lane_block_rows(pb_im)], axis=-1)
    lag_blocks = jnp.einsum('blrq,bcq->blrc', pb_cat, c_cat, precision=HIGHEST) * same_group(h, h, 128, 128)
    d_row = jnp.tile(d_skip.astype(F32).reshape(nlb, 1, 128), (1, 1, chunk))
    return dict(lag_blocks=lag_blocks.astype(BF16),
                w_re=block_diag_states(pb_re[::-1], w_dtype), w_im=block_diag_states(pb_im[::-1], w_dtype),
                v_re=block_diag_states(cv_re, BF16), v_im=block_diag_states(cv_im, BF16),
                lbl_re=pw_re[chunk].reshape(1, g * p), lbl_im=pw_im[chunk].reshape(1, g * p), d_row=d_row)


def _ssm_kernel(*refs, chunk, n_chunks, n_seq, aliased):
    (u_ref, t_ref, wr_ref, wi_ref, vr_ref, vi_ref, lr_ref, li_ref, d_ref, s0r_ref, s0i_ref) = refs[:11]
    y_ref, sr_ref, si_ref, z_scr, t_scr, lor_scr, loi_scr, pr_scr, pi_scr = refs[11 + aliased:]
    rows = n_seq * n_chunks
    for j in range(chunk):
        for t in range(chunk):
            blk = t_ref[0, t - j] if t >= j else jnp.zeros((128, 128), BF16)
            t_scr[j * 128:(j + 1) * 128, t * 128:(t + 1) * 128] = blk
    for j in range(chunk):
        z_scr[:, j * 128:(j + 1) * 128] = u_ref[pl.ds(j, rows, stride=chunk), :]
    z = z_scr[...]
    zw = z.astype(wr_ref.dtype)
    prec = HIGHEST if wr_ref.dtype == F32 else None
    lor = jnp.dot(zw, wr_ref[0], preferred_element_type=F32, precision=prec)
    loi = jnp.dot(zw, wi_ref[0], preferred_element_type=F32, precision=prec)
    lr = lr_ref[...]
    li = li_ref[...]
    sr = s0r_ref[...]
    si = s0i_ref[...]
    if n_chunks == 1:
        pr, pi = sr, si
        sr, si = lr * sr - li * si + lor, lr * si + li * sr + loi
    else:
        pieces = range(lor.shape[1] // 128)
        cut = lambda a: tuple(a[:, k * 128:(k + 1) * 128] for k in pieces)
        for k in pieces:
            lor_scr[k] = lor[:, k * 128:(k + 1) * 128]
            loi_scr[k] = loi[:, k * 128:(k + 1) * 128]
        lrs, lis = cut(lr), cut(li)

        def step(c, carry):
            idx = pl.ds(c, n_seq, stride=n_chunks)
            nxt = []
            for k, (a, b) in enumerate(zip(*carry)):
                pr_scr[k, idx, :] = a
                pi_scr[k, idx, :] = b
                nxt.append((lrs[k] * a - lis[k] * b + lor_scr[k, idx, :], lrs[k] * b + lis[k] * a + loi_scr[k, idx, :]))
            return tuple(n[0] for n in nxt), tuple(n[1] for n in nxt)
        srs, sis = lax.fori_loop(0, n_chunks, step, (cut(sr), cut(si)), unroll=8)
        sr, si = jnp.concatenate(srs, axis=1), jnp.concatenate(sis, axis=1)
        pr = jnp.concatenate([pr_scr[k] for k in pieces], axis=1)
        pi = jnp.concatenate([pi_scr[k] for k in pieces], axis=1)
    sr_ref[...] = sr
    si_ref[...] = si
    y = jnp.dot(z.astype(BF16), t_scr[...], preferred_element_type=F32)
    y += _qk(pr.astype(BF16), vr_ref[0])
    y += _qk(pi.astype(BF16), vi_ref[0])
    y += z * d_ref[0]
    for j in range(chunk):
        y_ref[pl.ds(j, rows, stride=chunk), :] = y[:, j * 128:(j + 1) * 128]


def _ssm(proj, row0, m_out, n_seq, n_chunks, chunk, ops, s0_re, s0_im, y_prev=None):
    rows = n_seq * n_chunks
    tok = rows * chunk
    nlb = SSM_GROUPS // SSM_GB
    width = chunk * 128
    sw = SSM_GB * SSM_STATE
    blk0 = row0 // tok
    per_lb = lambda a, b: pl.BlockSpec((1, a, b), lambda i: (i, 0, 0))
    lanes = lambda r, w: pl.BlockSpec((r, w), lambda i: (0, i))
    in_specs = [pl.BlockSpec((tok, 128), lambda i: (blk0, COL_U // 128 + i)),
                pl.BlockSpec((1, chunk, 128, 128), lambda i: (i, 0, 0, 0)),
                per_lb(width, sw), per_lb(width, sw), per_lb(width, sw), per_lb(width, sw),
                lanes(1, sw), lanes(1, sw), per_lb(1, width), lanes(n_seq, sw), lanes(n_seq, sw)]
    args = [proj, ops['lag_blocks'], ops['w_re'], ops['w_im'], ops['v_re'], ops['v_im'],
            ops['lbl_re'], ops['lbl_im'], ops['d_row'], s0_re, s0_im]
    aliases = {}
    if y_prev is not None:
        in_specs.append(pl.BlockSpec(memory_space=pl.ANY))
        args.append(y_prev)
        aliases = {len(args) - 1: 0}
    state = jax.ShapeDtypeStruct((n_seq, SSM_GROUPS * SSM_STATE), F32)
    return pl.pallas_call(
        functools.partial(_ssm_kernel, chunk=chunk, n_chunks=n_chunks, n_seq=n_seq, aliased=int(y_prev is not None)),
        grid=(nlb,),
        in_specs=in_specs,
        out_specs=[pl.BlockSpec((tok, 128), lambda i: (blk0, i)), lanes(n_seq, sw), lanes(n_seq, sw)],
        out_shape=[jax.ShapeDtypeStruct((m_out, D_SSM), F32), state, state],
        scratch_shapes=[pltpu.VMEM((rows, width), F32), pltpu.VMEM((width, width), BF16)]
        + [pltpu.VMEM((sw // 128, rows, 128), F32)] * 4,
        input_output_aliases=aliases,
        compiler_params=_cparams(("parallel",)),
        name=f"ssm_c{n_chunks}",
    )(*args)


def _softmax_sink_pv(parts, sink):
    masked = [jnp.where(valid, s, NEG_INF) for s, valid, _ in parts]
    m = sink
    for s in masked:
        m = jnp.maximum(m, jnp.max(s, axis=-1, keepdims=True))
    denom = jnp.exp(sink - m)
    out = None
    for s, (_, _, v) in zip(masked, parts):
        e = jnp.exp(s - m)
        denom = denom + jnp.sum(e, axis=-1, keepdims=True)
        pv = jnp.dot(e.astype(BF16), v, preferred_element_type=F32)
        out = pv if out is None else out + pv
    return out / denom


ATTN_STACK = 4


def _attn_prompt_kernel(q_ref, kp_ref, kc_ref, vp_ref, vc_ref, sink_ref, o_ref):
    qb = pl.program_id(1)
    rows = ATTN_STACK * WINDOW
    t = lax.broadcasted_iota(I32, (rows, 2 * WINDOW), 0) & (WINDOW - 1)
    col = lax.broadcasted_iota(I32, (rows, 2 * WINDOW), 1)
    valid = jnp.logical_or(jnp.logical_and(jnp.logical_and(col < WINDOW, col > t), qb > 0),
                           jnp.logical_and(col >= WINDOW, (col - WINDOW) <= t))
    scale = HEAD_DIM ** -0.5
    for kv in range(N_KV_HEADS):
        lanes = slice(kv * HEAD_DIM, (kv + 1) * HEAD_DIM)
        k = jnp.concatenate([kp_ref[:, lanes], kc_ref[:, lanes]], axis=0).astype(BF16)
        v = jnp.concatenate([vp_ref[:, lanes], vc_ref[:, lanes]], axis=0).astype(BF16)
        for h0 in range(kv * Q_PER_KV, (kv + 1) * Q_PER_KV, ATTN_STACK):
            heads = range(h0, h0 + ATTN_STACK)
            q = jnp.concatenate([q_ref[:, h * HEAD_DIM:(h + 1) * HEAD_DIM] for h in heads], axis=0).astype(BF16)
            sink = jnp.concatenate(
                [jnp.broadcast_to(sink_ref[0:1, h:h + 1], (WINDOW, 1)) for h in heads], axis=0)
            s = _qk(q, k) * scale
            o = _softmax_sink_pv([(s, valid, v)], sink)
            for i, h in enumerate(heads):
                o_ref[:, h * HEAD_DIM:(h + 1) * HEAD_DIM] = o[i * WINDOW:(i + 1) * WINDOW, :]


def _attn_prompt(proj, sinks, n_batch, seq, m_out):
    nb = seq // WINDOW
    cq, ck, cv = COL_Q // D_ATTN, COL_K // D_KV, COL_V // D_KV
    cur = lambda col: (lambda n, b: (n * nb + b, col))
    prev = lambda col: (lambda n, b: (n * nb + jnp.maximum(b - 1, 0), col))
    return pl.pallas_call(
        _attn_prompt_kernel,
        grid=(n_batch, nb),
        in_specs=[
            pl.BlockSpec((WINDOW, D_ATTN), cur(cq)),
            pl.BlockSpec((WINDOW, D_KV), prev(ck)),
            pl.BlockSpec((WINDOW, D_KV), cur(ck)),
            pl.BlockSpec((WINDOW, D_KV), prev(cv)),
            pl.BlockSpec((WINDOW, D_KV), cur(cv)),
            pl.BlockSpec((1, N_HEADS), lambda n, b: (0, 0)),
        ],
        out_specs=pl.BlockSpec((WINDOW, D_ATTN), lambda n, b: (n * nb + b, 0)),
        out_shape=jax.ShapeDtypeStruct((m_out, D_ATTN), F32),
        compiler_params=_cparams(("parallel", "arbitrary")),
        name="attn_prompt",
    )(proj, proj, proj, proj, proj, sinks)


SAMPLE_GB = 8


def _attn_sample_kernel(q_ref, kn_ref, vn_ref, kc_ref, vc_ref, sink_ref, o_ref, *, t_new):
    tok = SAMPLE_GB * t_new
    rows = Q_PER_KV * tok
    ncache = SAMPLE_GB * WINDOW
    r_c = lax.broadcasted_iota(I32, (rows, ncache), 0)
    c_c = lax.broadcasted_iota(I32, (rows, ncache), 1)
    rb_c = (r_c % tok) // t_new
    rt_c = r_c % t_new
    valid_c = jnp.logical_and(rb_c == c_c // WINDOW, (c_c % WINDOW) > rt_c)
    r_n = lax.broadcasted_iota(I32, (rows, tok), 0)
    c_n = lax.broadcasted_iota(I32, (rows, tok), 1)
    valid_n = jnp.logical_and((r_n % tok) // t_new == c_n // t_new, (c_n % t_new) <= (r_n % t_new))
    scale = HEAD_DIM ** -0.5
    for kv in range(N_KV_HEADS):
        lanes = slice(kv * HEAD_DIM, (kv + 1) * HEAD_DIM)
        kc = kc_ref[:, lanes].astype(BF16)
        vc = vc_ref[:, lanes].astype(BF16)
        kn = kn_ref[:, lanes].astype(BF16)
        vn = vn_ref[:, lanes].astype(BF16)
        heads = [kv * Q_PER_KV + i for i in range(Q_PER_KV)]
        q = jnp.concatenate([q_ref[:, h * HEAD_DIM:(h + 1) * HEAD_DIM] for h in heads], axis=0).astype(BF16)
        sink = jnp.concatenate(
            [jnp.broadcast_to(sink_ref[0:1, h:h + 1], (tok, 1)) for h in heads], axis=0)
        o = _softmax_sink_pv([(_qk(q, kc) * scale, valid_c, vc), (_qk(q, kn) * scale, valid_n, vn)], sink)
        for i, h in enumerate(heads):
            o_ref[:, h * HEAD_DIM:(h + 1) * HEAD_DIM] = o[i * tok:(i + 1) * tok, :]


def _attn_sample_aliased(q_ref, kn_ref, vn_ref, kc_ref, vc_ref, sink_ref, o_prev, o_ref, *, t_new):
    del o_prev
    _attn_sample_kernel(q_ref, kn_ref, vn_ref, kc_ref, vc_ref, sink_ref, o_ref, t_new=t_new)


def _attn_sample(proj, row0, cache_k, cache_v, sinks, n_batch, t_new, o_prev):
    tok = SAMPLE_GB * t_new
    blk0 = row0 // tok
    cq, ck, cv = COL_Q // D_ATTN, COL_K // D_KV, COL_V // D_KV
    new = lambda col: (lambda g: (blk0 + g, col))
    return pl.pallas_call(
        functools.partial(_attn_sample_aliased, t_new=t_new),
        grid=(n_batch // SAMPLE_GB,),
        in_specs=[
            pl.BlockSpec((tok, D_ATTN), new(cq)),
            pl.BlockSpec((tok, D_KV), new(ck)),
            pl.BlockSpec((tok, D_KV), new(cv)),
            pl.BlockSpec((SAMPLE_GB * WINDOW, D_KV), lambda g: (g, 0)),
            pl.BlockSpec((SAMPLE_GB * WINDOW, D_KV), lambda g: (g, 0)),
            pl.BlockSpec((1, N_HEADS), lambda g: (0, 0)),
            pl.BlockSpec(memory_space=pl.ANY),
        ],
        out_specs=pl.BlockSpec((tok, D_ATTN), lambda g: (blk0 + g, 0)),
        out_shape=jax.ShapeDtypeStruct(o_prev.shape, F32),
        input_output_aliases={6: 0},
        compiler_params=_cparams(("parallel",)),
        name="attn_sample",
    )(proj, proj, proj, cache_k, cache_v, sinks, o_prev)


TM4 = 256
ROUTER_LANES = 128
GATE_W = 512
GATE_BLOCKS = D_MODEL // GATE_W
ROW_TILES = D_MODEL // 128


def _gelu_tanh(x):
    return 0.5 * x * (1.0 + jnp.tanh(0.7978845608028654 * (x + 0.044715 * (x * x * x))))


def _merge_kernel(y_ref, o_ref, *refs, n_first):
    gs_refs, ga_refs = refs[:GATE_BLOCKS], refs[GATE_BLOCKS:2 * GATE_BLOCKS]
    (xa_ref, xb_ref, wglu_ref, bglu_ref, wbs_ref, wba_ref, wout_ref, g2_ref, wr_ref, wrl_ref, br_ref,
     x1_ref, h2_ref, ti_ref, tg_ref, rank_ref, cnt_ref, seen_scr) = refs[2 * GATE_BLOCKS:]

    @pl.when(pl.program_id(0) == 0)
    def _reset():
        seen_scr[...] = jnp.zeros(seen_scr.shape, F32)

    y1 = _gelu_tanh(y_ref[...])
    z = jnp.dot(y1.astype(BF16), wglu_ref[...], preferred_element_type=F32) + bglu_ref[...]
    y2 = y1 * _sigmoid(z)
    bs = jnp.dot(y2.astype(BF16), wbs_ref[...], preferred_element_type=F32)
    ba = jnp.dot(o_ref[...].astype(BF16), wba_ref[...], preferred_element_type=F32)
    merged = jnp.concatenate(
        [gs[...] * bs[:, c * GATE_W:(c + 1) * GATE_W] + ga[...] * ba[:, c * GATE_W:(c + 1) * GATE_W]
         for c, (gs, ga) in enumerate(zip(gs_refs, ga_refs))], axis=1)
    x = _two_source(pl.program_id(0), n_first, xa_ref, xb_ref)
    x1 = x + jnp.dot(merged.astype(BF16), wout_ref[...], preferred_element_type=F32)
    x1_ref[...] = x1
    ms = jnp.mean(x1 * x1, axis=-1, keepdims=True)
    h2 = x1 * lax.rsqrt(ms + RMS_EPS) * g2_ref[...]
    for s in range(ROW_TILES):
        h2_ref[pl.ds(s, TM4, stride=ROW_TILES), :] = h2[:, s * 128:(s + 1) * 128]
    h_hi = h2.astype(BF16)
    h_lo = (h2 - h_hi.astype(F32)).astype(BF16)
    logits = (jnp.dot(h_hi, wr_ref[...], preferred_element_type=F32)
              + (jnp.dot(h_hi, wrl_ref[...], preferred_element_type=F32)
                 + jnp.dot(h_lo, wr_ref[...], preferred_element_type=F32))) + br_ref[...]
    lane = lax.broadcasted_iota(I32, logits.shape, 1)
    cur = logits
    top_i = jnp.zeros(logits.shape, I32)
    top_v = jnp.zeros(logits.shape, F32)
    chosen = []
    for k in range(TOP_K):
        mx = jnp.max(cur, axis=-1, keepdims=True)
        idx = jnp.min(jnp.where(cur == mx, lane, ROUTER_LANES), axis=-1, keepdims=True)
        top_i = jnp.where(lane == k, idx, top_i)
        top_v = jnp.where(lane == k, mx, top_v)
        chosen.append(lane == idx)
        cur = jnp.where(chosen[-1], -jnp.inf, cur)
    v0 = jnp.max(jnp.where(lane < TOP_K, top_v, -jnp.inf), axis=-1, keepdims=True)
    e = jnp.where(lane < TOP_K, jnp.exp(top_v - v0), 0.0)
    ti_ref[...] = top_i
    tg_ref[...] = e / jnp.sum(e, axis=-1, keepdims=True)
    hits = sum(jnp.where(c, 1.0, 0.0) for c in chosen)
    r = lax.broadcasted_iota(I32, (TM4, TM4), 0)
    c = lax.broadcasted_iota(I32, (TM4, TM4), 1)
    before = jnp.dot(jnp.where(c < r, 1.0, 0.0).astype(BF16), hits.astype(BF16), preferred_element_type=F32)
    before = before + seen_scr[...]
    rank = jnp.zeros(logits.shape, F32)
    for k in range(TOP_K):
        rank = jnp.where(lane == k, jnp.sum(jnp.where(chosen[k], before, 0.0), axis=-1, keepdims=True), rank)
    rank_ref[...] = rank.astype(I32)
    seen_scr[...] = seen_scr[...] + jnp.sum(hits, axis=0, keepdims=True)
    cnt_ref[...] = seen_scr[...].astype(I32)


def _merge(y_ssm, o_attn, proj, xa, xb, wglu, bglu, wbs, wba, wout, g2, wr_hi, wr_lo, br):
    assert xa.shape[0] % TM4 == 0 and xb.shape[0] % TM4 == 0
    m = xa.shape[0] + xb.shape[0]
    n_first = xa.shape[0] // TM4
    row = lambda w: pl.BlockSpec((TM4, w), lambda i: (i, 0))
    const = lambda a, b: pl.BlockSpec((a, b), lambda i: (0, 0), pipeline_mode=pl.Buffered(1))
    return pl.pallas_call(
        functools.partial(_merge_kernel, n_first=n_first),
        grid=(m // TM4,),
        in_specs=[
            row(D_SSM), row(D_ATTN),
            *[pl.BlockSpec((TM4, GATE_W), lambda i, c=col // GATE_W + k: (i, c))
              for col in (COL_GS, COL_GA) for k in range(GATE_BLOCKS)],
            *_two_source_specs(TM4, D_MODEL, n_first),
            const(D_SSM, D_SSM), const(1, D_SSM), const(D_SSM, D_MODEL), const(D_ATTN, D_MODEL),
            const(D_MODEL, D_MODEL), const(1, D_MODEL), const(D_MODEL, ROUTER_LANES), const(D_MODEL, ROUTER_LANES),
            const(1, ROUTER_LANES),
        ],
        out_specs=[row(D_MODEL), pl.BlockSpec((TM4 * ROW_TILES, 128), lambda i: (i, 0)),
                   row(ROUTER_LANES), row(ROUTER_LANES), row(ROUTER_LANES),
                   pl.BlockSpec((1, ROUTER_LANES), lambda i: (0, 0))],
        out_shape=[jax.ShapeDtypeStruct((m, D_MODEL), F32), jax.ShapeDtypeStruct((m * ROW_TILES, 128), F32),
                   jax.ShapeDtypeStruct((m, ROUTER_LANES), I32), jax.ShapeDtypeStruct((m, ROUTER_LANES), F32),
                   jax.ShapeDtypeStruct((m, ROUTER_LANES), I32), jax.ShapeDtypeStruct((1, ROUTER_LANES), I32)],
        scratch_shapes=[pltpu.VMEM((1, ROUTER_LANES), F32)],
        compiler_params=_cparams(("arbitrary",)),
        name="merge",
    )(y_ssm, o_attn, *([proj] * (2 * GATE_BLOCKS)), xa, xb, wglu, bglu, wbs, wba, wout, g2, wr_hi, wr_lo, br)


SUB = 128
SUPER = 1280
TF = 256
NF = D_FF // TF
MOE_WIDE = 8


def _moe_sizes(m):
    a = m * TOP_K
    ns_max = N_EXPERTS + a // SUPER
    nb_max = N_EXPERTS + a // SUB
    return a, ns_max, nb_max


def _route(top_i, rank, counts, m):
    a, ns_max, nb_max = _moe_sizes(m)
    experts = jnp.arange(N_EXPERTS, dtype=I32)
    nsb = (counts + SUPER - 1) // SUPER
    sb_end = jnp.cumsum(nsb)
    sb_start = sb_end - nsb
    nb = (counts + SUB - 1) // SUB
    b_end = jnp.cumsum(nb)
    n_c = nb_max * SUB
    cslot = jnp.sum(jnp.where(top_i[:, :, None] == experts, (b_end - nb) * SUB, 0), axis=-1) + rank
    row_of = jnp.arange(TOP_K, dtype=I32)[None, :] * m + jnp.arange(m, dtype=I32)[:, None]
    dest = (a + jnp.arange(n_c, dtype=I32)).at[cslot.reshape(a)].set(row_of.reshape(a), unique_indices=True)
    tok = jnp.minimum(dest, a - 1) % m
    s_idx = jnp.arange(ns_max, dtype=I32)
    ns_used = sb_end[-1]
    s_eff = jnp.maximum(jnp.minimum(s_idx, ns_used - 1), 0)
    se = jnp.minimum(jnp.sum((s_eff[:, None] >= sb_end[None, :]).astype(I32), axis=1), N_EXPERTS - 1)
    of_se = lambda v: jnp.sum(jnp.where(se[:, None] == experts, v, 0), axis=-1)
    rows_in = jnp.clip(of_se(counts) - (s_eff - of_se(sb_start)) * SUPER, 0, SUPER)
    nsub = jnp.where(s_idx < ns_used, (rows_in + SUB - 1) // SUB, 0).astype(I32)
    cstart = (of_se(b_end - nb) * SUB + (s_eff - of_se(sb_start)) * SUPER).astype(I32)
    return dict(se=se.astype(I32), nsub=nsub, cstart=cstart, ns_used=ns_used.reshape(1).astype(I32),
                tok=tok.astype(I32), dest=dest)


_COPY_UNROLL = 16


def _moe_kernel(se_ref, nsub_ref, cstart_ref, nused_ref, tok_ref, dest_ref,
                h_hbm, wg_ref, wl_ref, bg_ref, bl_ref, wd_ref, bd_ref, y_hbm,
                wg_s, wl_s, wd_s, x_ref, xstage, acc, sem, sem_x):
    s = pl.program_id(0)
    f = pl.program_id(1)
    n = nsub_ref[s]

    def token_rows(sb, wait):
        base = cstart_ref[sb]

        def body(g, c):
            for u in range(_COPY_UNROLL):
                r = g * _COPY_UNROLL + u
                tok = 0 if wait else tok_ref[base + r]
                cp = pltpu.make_async_copy(
                    h_hbm.at[pl.ds(pl.multiple_of(tok * ROW_TILES, ROW_TILES), ROW_TILES), :],
                    xstage.at[pl.ds(pl.multiple_of(r * ROW_TILES, ROW_TILES), ROW_TILES), :], sem_x.at[0])
                cp.wait() if wait else cp.start()
            return c
        lax.fori_loop(0, nsub_ref[sb] * (SUB // _COPY_UNROLL), body, 0)

    def result_rows(sb, wait):
        sl = sb % 2
        base = cstart_ref[sb]

        def body(g, c):
            for u in range(_COPY_UNROLL):
                r = g * _COPY_UNROLL + u
                row = 0 if wait else dest_ref[base + r]
                cp = pltpu.make_async_copy(acc.at[sl, pl.ds(r, 1), :], y_hbm.at[pl.ds(row, 1), :], sem.at[sl])
                cp.wait() if wait else cp.start()
            return c
        lax.fori_loop(0, nsub_ref[sb] * (SUB // _COPY_UNROLL), body, 0)

    @pl.when(n > 0)
    def _work():
        o_ref = acc.at[s % 2]
        wg_s[...] = wg_ref[0].astype(BF16)
        wl_s[...] = wl_ref[0].astype(BF16)
        wd_s[...] = wd_ref[0].astype(BF16)

        @pl.when(f == 0)
        def _init():
            @pl.when(s == 0)
            def _first_rows():
                token_rows(0, False)
            token_rows(s, True)

            def to_bf16(i, c):
                for t in range(ROW_TILES):
                    src = pl.ds(i * (SUB * ROW_TILES) + t, SUB, stride=ROW_TILES)
                    x_ref[pl.ds(pl.multiple_of(i * SUB, SUB), SUB), t * 128:(t + 1) * 128] = xstage[src, :].astype(BF16)
                return c
            lax.fori_loop(0, n, to_bf16, 0)

            @pl.when(s + 1 < nused_ref[0])
            def _next_rows():
                token_rows(s + 1, False)

            @pl.when(s >= 2)
            def _free_slot():
                result_rows(s - 2, True)
            o_ref[...] = jnp.broadcast_to(bd_ref[0], o_ref.shape)

        bg = bg_ref[0]
        bl = bl_ref[0]

        def expert_rows(r0, rows):
            xs = x_ref[pl.ds(r0, rows), :]
            hg = jnp.dot(xs, wg_s[...], preferred_element_type=F32) + bg
            hl = jnp.dot(xs, wl_s[...], preferred_element_type=F32) + bl
            hg = jnp.minimum(hg, SWIGLU_LIMIT)
            hl = jnp.clip(hl, -SWIGLU_LIMIT, SWIGLU_LIMIT)
            act = hg * _sigmoid(SWIGLU_ALPHA * hg) * (hl + 1.0)
            o_ref[pl.ds(r0, rows), :] += jnp.dot(act.astype(BF16), wd_s[...], preferred_element_type=F32)

        def wide_body(i, c):
            expert_rows(pl.multiple_of(i * (MOE_WIDE * SUB), MOE_WIDE * SUB), MOE_WIDE * SUB)
            return c
        n_wide = n // MOE_WIDE
        lax.fori_loop(0, n_wide, wide_body, 0)

        def single_body(i, c):
            expert_rows(pl.multiple_of(i * SUB, SUB), SUB)
            return c
        lax.fori_loop(n_wide * MOE_WIDE, n, single_body, 0)

        @pl.when(f == NF - 1)
        def _send():
            result_rows(s, False)

    @pl.when(jnp.logical_and(s == pl.num_programs(0) - 1, f == NF - 1))
    def _drain():
        n_used = nused_ref[0]

        @pl.when(n_used >= 2)
        def _():
            result_rows(n_used - 2, True)
        result_rows(n_used - 1, True)


def _moe(h2, route, w_up, b_up, w_down, b_down, m):
    a, ns_max, nb_max = _moe_sizes(m)
    sp = lambda shape, imap: pl.BlockSpec(shape, lambda s, f, se, *_: imap(s, f, se))
    return pl.pallas_call(
        _moe_kernel,
        grid_spec=pltpu.PrefetchScalarGridSpec(
            num_scalar_prefetch=6,
            grid=(ns_max, NF),
            in_specs=[
                pl.BlockSpec(memory_space=pl.ANY),
                sp((1, D_MODEL, TF), lambda s, f, se: (se[s], 0, f)),
                sp((1, D_MODEL, TF), lambda s, f, se: (se[s], 0, NF + f)),
                sp((1, 1, TF), lambda s, f, se: (se[s], 0, f)),
                sp((1, 1, TF), lambda s, f, se: (se[s], 0, NF + f)),
                sp((1, TF, D_MODEL), lambda s, f, se: (se[s], f, 0)),
                sp((1, 1, D_MODEL), lambda s, f, se: (se[s], 0, 0)),
            ],
            out_specs=pl.BlockSpec(memory_space=pl.ANY),
            scratch_shapes=[pltpu.VMEM((D_MODEL, TF), BF16), pltpu.VMEM((D_MODEL, TF), BF16),
                            pltpu.VMEM((TF, D_MODEL), BF16), pltpu.VMEM((SUPER, D_MODEL), BF16),
                            pltpu.VMEM((SUPER * ROW_TILES, 128), F32), pltpu.VMEM((2, SUPER, D_MODEL), F32),
                            pltpu.SemaphoreType.DMA((2,)), pltpu.SemaphoreType.DMA((1,))],
        ),
        out_shape=jax.ShapeDtypeStruct((a + nb_max * SUB, D_MODEL), F32),
        compiler_params=_cparams(("arbitrary", "arbitrary")),
        name="moe_experts",
    )(route['se'], route['nsub'], route['cstart'], route['ns_used'], route['tok'], route['dest'],
      h2, w_up, w_up, b_up, b_up, w_down, b_down)


TC8 = 64


def _combine_kernel(*refs, n_first):
    y_refs = refs[:TOP_K]
    x1_ref, gate_ref, g_ref, o1_ref, o2_ref = refs[TOP_K:]
    b = pl.program_id(0)
    acc = x1_ref[...]
    gate = gate_ref[...]
    for k in range(TOP_K):
        acc = acc + gate[:, k:k + 1] * y_refs[k][...]
    ms = jnp.mean(acc * acc, axis=-1, keepdims=True)
    res = acc * lax.rsqrt(ms + RMS_EPS) * g_ref[...]

    @pl.when(b < n_first)
    def _to_first():
        o1_ref[...] = res

    @pl.when(b >= n_first)
    def _to_second():
        o2_ref[...] = res


def _combine(y_tok, x1, gate, g_final, m_first):
    m = x1.shape[0]
    n_first = m_first // TC8
    return pl.pallas_call(
        functools.partial(_combine_kernel, n_first=n_first),
        grid=(m // TC8,),
        in_specs=[
            *[pl.BlockSpec((TC8, D_MODEL), lambda b, k=k: (k * (m // TC8) + b, 0)) for k in range(TOP_K)],
            pl.BlockSpec((TC8, D_MODEL), lambda b: (b, 0)),
            pl.BlockSpec((TC8, ROUTER_LANES), lambda b: (b, 0)),
            pl.BlockSpec((1, D_MODEL), lambda b: (0, 0)),
        ],
        out_specs=[pl.BlockSpec((TC8, D_MODEL), lambda b: (jnp.minimum(b, n_first - 1), 0)),
                   pl.BlockSpec((TC8, D_MODEL), lambda b: (jnp.maximum(b - n_first, 0), 0))],
        out_shape=[jax.ShapeDtypeStruct((m_first, D_MODEL), F32), jax.ShapeDtypeStruct((m - m_first, D_MODEL), F32)],
        compiler_params=_cparams(("arbitrary",)),
        name="moe_combine",
    )(*([y_tok] * TOP_K), x1, gate, g_final)


SSM_CHUNK = 8


def _rope_tables(seq, t_new, n_batch, n_dec):
    half = HEAD_DIM // 2
    inv = ROPE_THETA ** (-jnp.arange(half, dtype=F32) / half)
    pos = jnp.concatenate([jnp.tile(jnp.arange(seq), n_batch), jnp.tile(PAST_LEN + jnp.arange(t_new), n_dec)])
    ang = pos.astype(F32)[:, None] * inv[None, :]
    cos = jnp.cos(ang)
    sin = jnp.sin(ang)
    return jnp.tile(cos, (1, 4)), jnp.tile(jnp.concatenate([-sin, sin], axis=1), (1, 2))


def kernel(x_prompt, x_sample, state_ssm_re, state_ssm_im, cache_k, cache_v, attn_norm_g, w_in, b_in, ssm_a_re, ssm_a_im, ssm_log_dt, ssm_b_re, ssm_b_im, ssm_c_re, ssm_c_im, ssm_d, w_glu, b_glu, attn_sinks, w_branch_ssm, w_branch_attn, w_out, ffn_norm_g, w_router, b_router, w_up, b_up, w_down, b_down, final_norm_g):
    n_batch, seq, _ = x_prompt.shape
    n_dec, t_new, _ = x_sample.shape
    mp = n_batch * seq
    ms = n_dec * t_new
    m = mp + ms
    g, p, h = SSM_GROUPS, SSM_STATE, SSM_GROUP
    xp = x_prompt.reshape(mp, D_MODEL)
    xs = x_sample.reshape(ms, D_MODEL)

    cos_t, sin_t = _rope_tables(seq, t_new, n_batch, n_dec)
    proj = _inproj(xp, xs, attn_norm_g[0].reshape(1, D_MODEL), w_in[0].astype(BF16), b_in[0].reshape(1, N_IN),
                   cos_t, sin_t)

    ssm_params = (ssm_a_re[0], ssm_a_im[0], ssm_log_dt[0], ssm_b_re[0], ssm_b_im[0], ssm_c_re[0], ssm_c_im[0], ssm_d[0])
    zeros = jnp.zeros((n_batch, g * p), F32)
    y_ssm, sp_re, sp_im = _ssm(proj, 0, m, n_batch, seq // SSM_CHUNK, SSM_CHUNK,
                               _ssm_operators(*ssm_params, SSM_CHUNK, BF16), zeros, zeros)
    y_ssm, ss_re, ss_im = _ssm(proj, mp, m, n_dec, 1, t_new, _ssm_operators(*ssm_params, t_new, F32),
                               state_ssm_re[0].reshape(n_dec, g * p), state_ssm_im[0].reshape(n_dec, g * p), y_prev=y_ssm)

    sinks = attn_sinks[0].reshape(1, N_HEADS)
    o_attn = _attn_prompt(proj, sinks, n_batch, seq, m)
    o_attn = _attn_sample(proj, mp, cache_k[0].reshape(n_dec * WINDOW, D_KV), cache_v[0].reshape(n_dec * WINDOW, D_KV),
                          sinks, n_dec, t_new, o_attn)

    wr = jnp.pad(w_router[0], ((0, 0), (0, ROUTER_LANES - N_EXPERTS)))
    wr_hi = wr.astype(BF16)
    wr_lo = (wr - wr_hi.astype(F32)).astype(BF16)
    br = jnp.pad(b_router[0], (0, ROUTER_LANES - N_EXPERTS), constant_values=NEG_INF).reshape(1, ROUTER_LANES)
    x1, h2, top_i, gate, rank, counts = _merge(
        y_ssm, o_attn, proj, xp, xs, w_glu[0].astype(BF16), b_glu[0].reshape(1, D_SSM), w_branch_ssm[0].astype(BF16),
        w_branch_attn[0].astype(BF16), w_out[0].astype(BF16), ffn_norm_g[0].reshape(1, D_MODEL), wr_hi, wr_lo, br)

    route = _route(top_i[:, :TOP_K], rank[:, :TOP_K], counts[0, :N_EXPERTS], m)
    y_tok = _moe(h2, route, w_up[0], b_up[0].reshape(N_EXPERTS, 1, 2 * D_FF), w_down[0],
                    b_down[0].reshape(N_EXPERTS, 1, D_MODEL), m)
    y_p, y_s = _combine(y_tok, x1, gate, final_norm_g.reshape(1, D_MODEL), mp)

    y_prompt = y_p.reshape(n_batch, seq, D_MODEL)
    y_sample = y_s.reshape(n_dec, t_new, D_MODEL)
    to_state = lambda s: s.reshape(1, s.shape[0], g, p)
    k_all = proj[:, COL_K:COL_K + D_KV]
    v_all = proj[:, COL_V:COL_V + D_KV]
    kv_p = lambda a: a[:mp].reshape(n_batch, seq, N_KV_HEADS, HEAD_DIM)[:, -WINDOW:][None]
    kv_s = lambda a, c: jnp.concatenate([c[0][:, t_new:], a[mp:].reshape(n_dec, t_new, N_KV_HEADS, HEAD_DIM)], axis=1)[None]
    return (y_prompt, y_sample, to_state(sp_re), to_state(sp_im), kv_p(k_all), kv_p(v_all),
            to_state(ss_re), to_state(ss_im), kv_s(k_all, cache_k), kv_s(v_all, cache_v))
```

```python
import functools

import jax
import jax.numpy as jnp
from jax import lax
from jax.experimental import pallas as pl
from jax.experimental.pallas import tpu as pltpu

F32 = jnp.float32
BF16 = jnp.bfloat16
I32 = jnp.int32
HIGHEST = lax.Precision.HIGHEST

D_MODEL = 2048
RMS_EPS = 1e-5
SSM_GROUP = 16
SSM_GROUPS = 64
SSM_STATE = 64
D_SSM = 1024
HEAD_DIM = 64
N_HEADS = 16
N_KV_HEADS = 4
Q_PER_KV = 4
D_ATTN = 1024
D_KV = 256
WINDOW = 128
ROPE_THETA = 10000.0
NEG_INF = -1e30
PAST_LEN = 16384
N_IN = D_SSM + D_ATTN + 2 * D_KV + 2 * D_MODEL
N_EXPERTS = 32
TOP_K = 4
D_FF = 2048
SWIGLU_ALPHA = 1.702
SWIGLU_LIMIT = 7.0

COL_U, COL_Q, COL_K, COL_V, COL_GS, COL_GA = 0, 1024, 2048, 2304, 2560, 4608

VMEM_LIMIT = 56 * 1024 * 1024


def _sigmoid(x):
    return 0.5 * jnp.tanh(0.5 * x) + 0.5


def _qk(q, k):
    return lax.dot_general(q, k, (((1,), (1,)), ((), ())), preferred_element_type=F32)


def _cparams(sem, vmem=VMEM_LIMIT):
    return pltpu.CompilerParams(dimension_semantics=sem, vmem_limit_bytes=vmem)


TM1 = 512
TN1 = 512
_ROPE_TILE0 = COL_Q // TN1
_KV_TILE = COL_K // TN1
_GATE_TILE0 = COL_GS // TN1


def _two_source(i, n_first, a_ref, b_ref):
    return jnp.where(i < n_first, a_ref[...], b_ref[...])


def _two_source_specs(tile, width, n_first):
    return [pl.BlockSpec((tile, width), lambda i, *_: (jnp.minimum(i, n_first - 1), 0)),
            pl.BlockSpec((tile, width), lambda i, *_: (jnp.maximum(i - n_first, 0), 0))]


def _inproj_kernel(xa_ref, xb_ref, g_ref, w_ref, b_ref, cos_ref, sin_ref, o_ref, h_scr, *, n_first):
    j = pl.program_id(1)

    def norm_from(x_ref):
        x = x_ref[...]
        ms = jnp.mean(x * x, axis=-1, keepdims=True)
        h_scr[...] = (x * lax.rsqrt(ms + RMS_EPS) * g_ref[...]).astype(BF16)

    @pl.when(jnp.logical_and(j == 0, pl.program_id(0) < n_first))
    def _norm_a():
        norm_from(xa_ref)

    @pl.when(jnp.logical_and(j == 0, pl.program_id(0) >= n_first))
    def _norm_b():
        norm_from(xb_ref)

    acc = jnp.dot(h_scr[...], w_ref[...], preferred_element_type=F32) + b_ref[...]

    @pl.when(j >= _GATE_TILE0)
    def _gate():
        o_ref[...] = _sigmoid(acc)

    @pl.when(j < _ROPE_TILE0)
    def _plain():
        o_ref[...] = acc

    @pl.when(jnp.logical_and(j >= _ROPE_TILE0, j <= _KV_TILE))
    def _rope():
        c = jnp.concatenate([cos_ref[...]] * (TN1 // 128), axis=1)
        s = jnp.concatenate([sin_ref[...]] * (TN1 // 128), axis=1)
        lane = lax.broadcasted_iota(I32, acc.shape, 1)
        first_half = (lane & (HEAD_DIM // 2)) == 0
        partner = jnp.where(first_half, pltpu.roll(acc, TN1 - HEAD_DIM // 2, 1), pltpu.roll(acc, HEAD_DIM // 2, 1))
        roped = acc * c + partner * s
        n_rot = jnp.where(j == _KV_TILE, D_KV, TN1)
        o_ref[...] = jnp.where(lane < n_rot, roped, acc)


def _inproj(xa, xb, g, w_bf, b, cos_t, sin_t):
    assert xa.shape[0] % TM1 == 0 and xb.shape[0] % TM1 == 0
    m = xa.shape[0] + xb.shape[0]
    n_first = xa.shape[0] // TM1
    return pl.pallas_call(
        functools.partial(_inproj_kernel, n_first=n_first),
        grid=(m // TM1, N_IN // TN1),
        in_specs=_two_source_specs(TM1, D_MODEL, n_first) + [
            pl.BlockSpec((1, D_MODEL), lambda i, j: (0, 0)),
            pl.BlockSpec((D_MODEL, TN1), lambda i, j: (0, j)),
            pl.BlockSpec((1, TN1), lambda i, j: (0, j)),
            pl.BlockSpec((TM1, 128), lambda i, j: (i, 0)),
            pl.BlockSpec((TM1, 128), lambda i, j: (i, 0)),
        ],
        out_specs=pl.BlockSpec((TM1, TN1), lambda i, j: (i, j)),
        out_shape=jax.ShapeDtypeStruct((m, N_IN), F32),
        scratch_shapes=[pltpu.VMEM((TM1, D_MODEL), BF16)],
        compiler_params=_cparams(("parallel", "arbitrary")),
        name="inproj",
    )(xa, xb, g, w_bf, b, cos_t, sin_t)


SSM_GB = 8


def _ssm_operators(a_re, a_im, log_dt, b_re, b_im, c_re, c_im, d_skip, chunk, w_dtype):
    a_re = a_re.astype(F32)
    a_im = a_im.astype(F32)
    dt = jnp.exp(log_dt.astype(F32))[:, None]
    mag = jnp.exp(a_re * dt)
    lb_re = mag * jnp.cos(a_im * dt)
    lb_im = mag * jnp.sin(a_im * dt)
    inv = 1.0 / (a_re * a_re + a_im * a_im)
    f_re = ((lb_re - 1.0) * a_re + lb_im * a_im) * inv
    f_im = (lb_im * a_re - (lb_re - 1.0) * a_im) * inv
    g, p, h = b_re.shape
    nlb = g // SSM_GB
    sw = SSM_GB * p
    br = jnp.transpose(b_re.astype(F32), (0, 2, 1))
    bi = jnp.transpose(b_im.astype(F32), (0, 2, 1))
    bb_re = f_re[:, None, :] * br - f_im[:, None, :] * bi
    bb_im = f_re[:, None, :] * bi + f_im[:, None, :] * br
    k = jnp.arange(chunk + 1, dtype=F32)[:, None, None]
    pmag = jnp.exp(a_re * dt * k)
    pw_re = (pmag * jnp.cos(a_im * dt * k))[:, :, None, :]
    pw_im = (pmag * jnp.sin(a_im * dt * k))[:, :, None, :]
    pb_re = pw_re[:chunk] * bb_re[None] - pw_im[:chunk] * bb_im[None]
    pb_im = pw_re[:chunk] * bb_im[None] + pw_im[:chunk] * bb_re[None]
    cr = c_re.astype(F32)
    ci = c_im.astype(F32)
    cv_re = cr[None] * pw_re[1:] - ci[None] * pw_im[1:]
    cv_im = -(cr[None] * pw_im[1:] + ci[None] * pw_re[1:])

    def lane_block_rows(x):
        return jnp.transpose(x.reshape(x.shape[0], nlb, SSM_GB * h, x.shape[-1]), (1, 0, 2, 3))

    def same_group(rows_per, cols_per, n_rows, n_cols):
        return (jnp.arange(n_rows)[:, None] // rows_per == jnp.arange(n_cols)[None, :] // cols_per).astype(F32)
    spread = jnp.tile(jnp.eye(p, dtype=F32), (1, SSM_GB))

    def block_diag_states(x, dtype):
        y = jnp.einsum('blrp,pc->blrc', lane_block_rows(x), spread, precision=HIGHEST) * same_group(h, p, 128, sw)
        return y.reshape(nlb, chunk * 128, sw).astype(dtype)
    c_cat = jnp.concatenate([cr.reshape(nlb, 128, p), -ci.reshape(nlb, 128, p)], axis=-1)
    pb_cat = jnp.concatenate([lane_block_rows(pb_re), lane_block_rows(pb_im)], axis=-1)
    lag_blocks = jnp.einsum('blrq,bcq->blrc', pb_cat, c_cat, precision=HIGHEST) * same_group(h, h, 128, 128)
    d_row = jnp.tile(d_skip.astype(F32).reshape(nlb, 1, 128), (1, 1, chunk))
    return dict(lag_blocks=lag_blocks.astype(BF16),
                w_re=block_diag_states(pb_re[::-1], w_dtype), w_im=block_diag_states(pb_im[::-1], w_dtype),
                v_re=block_diag_states(cv_re, BF16), v_im=block_diag_states(cv_im, BF16),
                lbl_re=pw_re[chunk].reshape(1, g * p), lbl_im=pw_im[chunk].reshape(1, g * p), d_row=d_row)


def _ssm_kernel(*refs, chunk, n_chunks, n_seq, aliased):
    (u_ref, t_ref, wr_ref, wi_ref, vr_ref, vi_ref, lr_ref, li_ref, d_ref, s0r_ref, s0i_ref) = refs[:11]
    y_ref, sr_ref, si_ref, z_scr, t_scr, lor_scr, loi_scr, pr_scr, pi_scr = refs[11 + aliased:]
    rows = n_seq * n_chunks
    for j in range(chunk):
        for t in range(chunk):
            blk = t_ref[0, t - j] if t >= j else jnp.zeros((128, 128), BF16)
            t_scr[j * 128:(j + 1) * 128, t * 128:(t + 1) * 128] = blk
    for j in range(chunk):
        z_scr[:, j * 128:(j + 1) * 128] = u_ref[pl.ds(j, rows, stride=chunk), :]
    z = z_scr[...]
    zw = z.astype(wr_ref.dtype)
    prec = HIGHEST if wr_ref.dtype == F32 else None
    lor = jnp.dot(zw, wr_ref[0], preferred_element_type=F32, precision=prec)
    loi = jnp.dot(zw, wi_ref[0], preferred_element_type=F32, precision=prec)
    lr = lr_ref[...]
    li = li_ref[...]
    sr = s0r_ref[...]
    si = s0i_ref[...]
    if n_chunks == 1:
        pr, pi = sr, si
        sr, si = lr * sr - li * si + lor, lr * si + li * sr + loi
    else:
        pieces = range(lor.shape[1] // 128)
        cut = lambda a: tuple(a[:, k * 128:(k + 1) * 128] for k in pieces)
        for k in pieces:
            lor_scr[k] = lor[:, k * 128:(k + 1) * 128]
            loi_scr[k] = loi[:, k * 128:(k + 1) * 128]
        lrs, lis = cut(lr), cut(li)

        def step(c, carry):
            idx = pl.ds(c, n_seq, stride=n_chunks)
            nxt = []
            for k, (a, b) in enumerate(zip(*carry)):
                pr_scr[k, idx, :] = a
                pi_scr[k, idx, :] = b
                nxt.append((lrs[k] * a - lis[k] * b + lor_scr[k, idx, :], lrs[k] * b + lis[k] * a + loi_scr[k, idx, :]))
            return tuple(n[0] for n in nxt), tuple(n[1] for n in nxt)
        srs, sis = lax.fori_loop(0, n_chunks, step, (cut(sr), cut(si)), unroll=8)
        sr, si = jnp.concatenate(srs, axis=1), jnp.concatenate(sis, axis=1)
        pr = jnp.concatenate([pr_scr[k] for k in pieces], axis=1)
        pi = jnp.concatenate([pi_scr[k] for k in pieces], axis=1)
    sr_ref[...] = sr
    si_ref[...] = si
    y = jnp.dot(z.astype(BF16), t_scr[...], preferred_element_type=F32)
    y += _qk(pr.astype(BF16), vr_ref[0])
    y += _qk(pi.astype(BF16), vi_ref[0])
    y += z * d_ref[0]
    for j in range(chunk):
        y_ref[pl.ds(j, rows, stride=chunk), :] = y[:, j * 128:(j + 1) * 128]


def _ssm(proj, row0, m_out, n_seq, n_chunks, chunk, ops, s0_re, s0_im, y_prev=None):
    rows = n_seq * n_chunks
    tok = rows * chunk
    nlb = SSM_GROUPS // SSM_GB
    width = chunk * 128
    sw = SSM_GB * SSM_STATE
    blk0 = row0 // tok
    per_lb = lambda a, b: pl.BlockSpec((1, a, b), lambda i: (i, 0, 0))
    lanes = lambda r, w: pl.BlockSpec((r, w), lambda i: (0, i))
    in_specs = [pl.BlockSpec((tok, 128), lambda i: (blk0, COL_U // 128 + i)),
                pl.BlockSpec((1, chunk, 128, 128), lambda i: (i, 0, 0, 0)),
                per_lb(width, sw), per_lb(width, sw), per_lb(width, sw), per_lb(width, sw),
                lanes(1, sw), lanes(1, sw), per_lb(1, width), lanes(n_seq, sw), lanes(n_seq, sw)]
    args = [proj, ops['lag_blocks'], ops['w_re'], ops['w_im'], ops['v_re'], ops['v_im'],
            ops['lbl_re'], ops['lbl_im'], ops['d_row'], s0_re, s0_im]
    aliases = {}
    if y_prev is not None:
        in_specs.append(pl.BlockSpec(memory_space=pl.ANY))
        args.append(y_prev)
        aliases = {len(args) - 1: 0}
    state = jax.ShapeDtypeStruct((n_seq, SSM_GROUPS * SSM_STATE), F32)
    return pl.pallas_call(
        functools.partial(_ssm_kernel, chunk=chunk, n_chunks=n_chunks, n_seq=n_seq, aliased=int(y_prev is not None)),
        grid=(nlb,),
        in_specs=in_specs,
        out_specs=[pl.BlockSpec((tok, 128), lambda i: (blk0, i)), lanes(n_seq, sw), lanes(n_seq, sw)],
        out_shape=[jax.ShapeDtypeStruct((m_out, D_SSM), F32), state, state],
        scratch_shapes=[pltpu.VMEM((rows, width), F32), pltpu.VMEM((width, width), BF16)]
        + [pltpu.VMEM((sw // 128, rows, 128), F32)] * 4,
        input_output_aliases=aliases,
        compiler_params=_cparams(("parallel",)),
        name=f"ssm_c{n_chunks}",
    )(*args)


def _softmax_sink_pv(parts, sink):
    masked = [jnp.where(valid, s, NEG_INF) for s, valid, _ in parts]
    m = sink
    for s in masked:
        m = jnp.maximum(m, jnp.max(s, axis=-1, keepdims=True))
    denom = jnp.exp(sink - m)
    out = None
    for s, (_, _, v) in zip(masked, parts):
        e = jnp.exp(s - m)
        denom = denom + jnp.sum(e, axis=-1, keepdims=True)
        pv = jnp.dot(e.astype(BF16), v, preferred_element_type=F32)
        out = pv if out is None else out + pv
    return out / denom


ATTN_STACK = 4


def _attn_prompt_kernel(q_ref, kp_ref, kc_ref, vp_ref, vc_ref, sink_ref, o_ref):
    qb = pl.program_id(1)
    rows = ATTN_STACK * WINDOW
    t = lax.broadcasted_iota(I32, (rows, 2 * WINDOW), 0) & (WINDOW - 1)
    col = lax.broadcasted_iota(I32, (rows, 2 * WINDOW), 1)
    valid = jnp.logical_or(jnp.logical_and(jnp.logical_and(col < WINDOW, col > t), qb > 0),
                           jnp.logical_and(col >= WINDOW, (col - WINDOW) <= t))
    scale = HEAD_DIM ** -0.5
    for kv in range(N_KV_HEADS):
        lanes = slice(kv * HEAD_DIM, (kv + 1) * HEAD_DIM)
        k = jnp.concatenate([kp_ref[:, lanes], kc_ref[:, lanes]], axis=0).astype(BF16)
        v = jnp.concatenate([vp_ref[:, lanes], vc_ref[:, lanes]], axis=0).astype(BF16)
        for h0 in range(kv * Q_PER_KV, (kv + 1) * Q_PER_KV, ATTN_STACK):
            heads = range(h0, h0 + ATTN_STACK)
            q = jnp.concatenate([q_ref[:, h * HEAD_DIM:(h + 1) * HEAD_DIM] for h in heads], axis=0).astype(BF16)
            sink = jnp.concatenate(
                [jnp.broadcast_to(sink_ref[0:1, h:h + 1], (WINDOW, 1)) for h in heads], axis=0)
            s = _qk(q, k) * scale
            o = _softmax_sink_pv([(s, valid, v)], sink)
            for i, h in enumerate(heads):
                o_ref[:, h * HEAD_DIM:(h + 1) * HEAD_DIM] = o[i * WINDOW:(i + 1) * WINDOW, :]


def _attn_prompt(proj, sinks, n_batch, seq, m_out):
    nb = seq // WINDOW
    cq, ck, cv = COL_Q // D_ATTN, COL_K // D_KV, COL_V // D_KV
    cur = lambda col: (lambda n, b: (n * nb + b, col))
    prev = lambda col: (lambda n, b: (n * nb + jnp.maximum(b - 1, 0), col))
    return pl.pallas_call(
        _attn_prompt_kernel,
        grid=(n_batch, nb),
        in_specs=[
            pl.BlockSpec((WINDOW, D_ATTN), cur(cq)),
            pl.BlockSpec((WINDOW, D_KV), prev(ck)),
            pl.BlockSpec((WINDOW, D_KV), cur(ck)),
            pl.BlockSpec((WINDOW, D_KV), prev(cv)),
            pl.BlockSpec((WINDOW, D_KV), cur(cv)),
            pl.BlockSpec((1, N_HEADS), lambda n, b: (0, 0)),
        ],
        out_specs=pl.BlockSpec((WINDOW, D_ATTN), lambda n, b: (n * nb + b, 0)),
        out_shape=jax.ShapeDtypeStruct((m_out, D_ATTN), F32),
        compiler_params=_cparams(("parallel", "arbitrary")),
        name="attn_prompt",
    )(proj, proj, proj, proj, proj, sinks)


SAMPLE_GB = 8


def _attn_sample_kernel(q_ref, kn_ref, vn_ref, kc_ref, vc_ref, sink_ref, o_ref, *, t_new):
    tok = SAMPLE_GB * t_new
    rows = Q_PER_KV * tok
    ncache = SAMPLE_GB * WINDOW
    r_c = lax.broadcasted_iota(I32, (rows, ncache), 0)
    c_c = lax.broadcasted_iota(I32, (rows, ncache), 1)
    rb_c = (r_c % tok) // t_new
    rt_c = r_c % t_new
    valid_c = jnp.logical_and(rb_c == c_c // WINDOW, (c_c % WINDOW) > rt_c)
    r_n = lax.broadcasted_iota(I32, (rows, tok), 0)
    c_n = lax.broadcasted_iota(I32, (rows, tok), 1)
    valid_n = jnp.logical_and((r_n % tok) // t_new == c_n // t_new, (c_n % t_new) <= (r_n % t_new))
    scale = HEAD_DIM ** -0.5
    for kv in range(N_KV_HEADS):
        lanes = slice(kv * HEAD_DIM, (kv + 1) * HEAD_DIM)
        kc = kc_ref[:, lanes].astype(BF16)
        vc = vc_ref[:, lanes].astype(BF16)
        kn = kn_ref[:, lanes].astype(BF16)
        vn = vn_ref[:, lanes].astype(BF16)
        heads = [kv * Q_PER_KV + i for i in range(Q_PER_KV)]
        q = jnp.concatenate([q_ref[:, h * HEAD_DIM:(h + 1) * HEAD_DIM] for h in heads], axis=0).astype(BF16)
        sink = jnp.concatenate(
            [jnp.broadcast_to(sink_ref[0:1, h:h + 1], (tok, 1)) for h in heads], axis=0)
        o = _softmax_sink_pv([(_qk(q, kc) * scale, valid_c, vc), (_qk(q, kn) * scale, valid_n, vn)], sink)
        for i, h in enumerate(heads):
            o_ref[:, h * HEAD_DIM:(h + 1) * HEAD_DIM] = o[i * tok:(i + 1) * tok, :]


def _attn_sample_aliased(q_ref, kn_ref, vn_ref, kc_ref, vc_ref, sink_ref, o_prev, o_ref, *, t_new):
    del o_prev
    _attn_sample_kernel(q_ref, kn_ref, vn_ref, kc_ref, vc_ref, sink_ref, o_ref, t_new=t_new)


def _attn_sample(proj, row0, cache_k, cache_v, sinks, n_batch, t_new, o_prev):
    tok = SAMPLE_GB * t_new
    blk0 = row0 // tok
    cq, ck, cv = COL_Q // D_ATTN, COL_K // D_KV, COL_V // D_KV
    new = lambda col: (lambda g: (blk0 + g, col))
    return pl.pallas_call(
        functools.partial(_attn_sample_aliased, t_new=t_new),
        grid=(n_batch // SAMPLE_GB,),
        in_specs=[
            pl.BlockSpec((tok, D_ATTN), new(cq)),
            pl.BlockSpec((tok, D_KV), new(ck)),
            pl.BlockSpec((tok, D_KV), new(cv)),
            pl.BlockSpec((SAMPLE_GB * WINDOW, D_KV), lambda g: (g, 0)),
            pl.BlockSpec((SAMPLE_GB * WINDOW, D_KV), lambda g: (g, 0)),
            pl.BlockSpec((1, N_HEADS), lambda g: (0, 0)),
            pl.BlockSpec(memory_space=pl.ANY),
        ],
        out_specs=pl.BlockSpec((tok, D_ATTN), lambda g: (blk0 + g, 0)),
        out_shape=jax.ShapeDtypeStruct(o_prev.shape, F32),
        input_output_aliases={6: 0},
        compiler_params=_cparams(("parallel",)),
        name="attn_sample",
    )(proj, proj, proj, cache_k, cache_v, sinks, o_prev)


TM4 = 256
ROUTER_LANES = 128
GATE_W = 512
GATE_BLOCKS = D_MODEL // GATE_W
ROW_TILES = D_MODEL // 128


def _gelu_tanh(x):
    return 0.5 * x * (1.0 + jnp.tanh(0.7978845608028654 * (x + 0.044715 * (x * x * x))))


def _merge_kernel(y_ref, o_ref, *refs, n_first):
    gs_refs, ga_refs = refs[:GATE_BLOCKS], refs[GATE_BLOCKS:2 * GATE_BLOCKS]
    (xa_ref, xb_ref, wglu_ref, bglu_ref, wbs_ref, wba_ref, wout_ref, g2_ref, wr_ref, wrl_ref, br_ref,
     x1_ref, h2_ref, ti_ref, tg_ref, rank_ref, cnt_ref, seen_scr) = refs[2 * GATE_BLOCKS:]

    @pl.when(pl.program_id(0) == 0)
    def _reset():
        seen_scr[...] = jnp.zeros(seen_scr.shape, F32)

    y1 = _gelu_tanh(y_ref[...])
    z = jnp.dot(y1.astype(BF16), wglu_ref[...], preferred_element_type=F32) + bglu_ref[...]
    y2 = y1 * _sigmoid(z)
    bs = jnp.dot(y2.astype(BF16), wbs_ref[...], preferred_element_type=F32)
    ba = jnp.dot(o_ref[...].astype(BF16), wba_ref[...], preferred_element_type=F32)
    merged = jnp.concatenate(
        [gs[...] * bs[:, c * GATE_W:(c + 1) * GATE_W] + ga[...] * ba[:, c * GATE_W:(c + 1) * GATE_W]
         for c, (gs, ga) in enumerate(zip(gs_refs, ga_refs))], axis=1)
    x = _two_source(pl.program_id(0), n_first, xa_ref, xb_ref)
    x1 = x + jnp.dot(merged.astype(BF16), wout_ref[...], preferred_element_type=F32)
    x1_ref[...] = x1
    ms = jnp.mean(x1 * x1, axis=-1, keepdims=True)
    h2 = x1 * lax.rsqrt(ms + RMS_EPS) * g2_ref[...]
    h2_ref[...] = h2
    h_hi = h2.astype(BF16)
    h_lo = (h2 - h_hi.astype(F32)).astype(BF16)
    logits = (jnp.dot(h_hi, wr_ref[...], preferred_element_type=F32)
              + (jnp.dot(h_hi, wrl_ref[...], preferred_element_type=F32)
                 + jnp.dot(h_lo, wr_ref[...], preferred_element_type=F32))) + br_ref[...]
    lane = lax.broadcasted_iota(I32, logits.shape, 1)
    cur = logits
    top_i = jnp.zeros(logits.shape, I32)
    top_v = jnp.zeros(logits.shape, F32)
    chosen = []
    for k in range(TOP_K):
        mx = jnp.max(cur, axis=-1, keepdims=True)
        idx = jnp.min(jnp.where(cur == mx, lane, ROUTER_LANES), axis=-1, keepdims=True)
        top_i = jnp.where(lane == k, idx, top_i)
        top_v = jnp.where(lane == k, mx, top_v)
        chosen.append(lane == idx)
        cur = jnp.where(chosen[-1], -jnp.inf, cur)
    v0 = jnp.max(jnp.where(lane < TOP_K, top_v, -jnp.inf), axis=-1, keepdims=True)
    e = jnp.where(lane < TOP_K, jnp.exp(top_v - v0), 0.0)
    ti_ref[...] = top_i
    tg_ref[...] = e / jnp.sum(e, axis=-1, keepdims=True)
    hits = sum(jnp.where(c, 1.0, 0.0) for c in chosen)
    r = lax.broadcasted_iota(I32, (TM4, TM4), 0)
    c = lax.broadcasted_iota(I32, (TM4, TM4), 1)
    before = jnp.dot(jnp.where(c < r, 1.0, 0.0).astype(BF16), hits.astype(BF16), preferred_element_type=F32)
    before = before + seen_scr[...]
    rank = jnp.zeros(logits.shape, F32)
    for k in range(TOP_K):
        rank = jnp.where(lane == k, jnp.sum(jnp.where(chosen[k], before, 0.0), axis=-1, keepdims=True), rank)
    rank_ref[...] = rank.astype(I32)
    seen_scr[...] = seen_scr[...] + jnp.sum(hits, axis=0, keepdims=True)
    cnt_ref[...] = seen_scr[...].astype(I32)


def _merge(y_ssm, o_attn, proj, xa, xb, wglu, bglu, wbs, wba, wout, g2, wr_hi, wr_lo, br):
    assert xa.shape[0] % TM4 == 0 and xb.shape[0] % TM4 == 0
    m = xa.shape[0] + xb.shape[0]
    n_first = xa.shape[0] // TM4
    row = lambda w: pl.BlockSpec((TM4, w), lambda i: (i, 0))
    const = lambda a, b: pl.BlockSpec((a, b), lambda i: (0, 0), pipeline_mode=pl.Buffered(1))
    return pl.pallas_call(
        functools.partial(_merge_kernel, n_first=n_first),
        grid=(m // TM4,),
        in_specs=[
            row(D_SSM), row(D_ATTN),
            *[pl.BlockSpec((TM4, GATE_W), lambda i, c=col // GATE_W + k: (i, c))
              for col in (COL_GS, COL_GA) for k in range(GATE_BLOCKS)],
            *_two_source_specs(TM4, D_MODEL, n_first),
            const(D_SSM, D_SSM), const(1, D_SSM), const(D_SSM, D_MODEL), const(D_ATTN, D_MODEL),
            const(D_MODEL, D_MODEL), const(1, D_MODEL), const(D_MODEL, ROUTER_LANES), const(D_MODEL, ROUTER_LANES),
            const(1, ROUTER_LANES),
        ],
        out_specs=[row(D_MODEL), row(D_MODEL),
                   row(ROUTER_LANES), row(ROUTER_LANES), row(ROUTER_LANES),
                   pl.BlockSpec((1, ROUTER_LANES), lambda i: (0, 0))],
        out_shape=[jax.ShapeDtypeStruct((m, D_MODEL), F32), jax.ShapeDtypeStruct((m, D_MODEL), F32),
                   jax.ShapeDtypeStruct((m, ROUTER_LANES), I32), jax.ShapeDtypeStruct((m, ROUTER_LANES), F32),
                   jax.ShapeDtypeStruct((m, ROUTER_LANES), I32), jax.ShapeDtypeStruct((1, ROUTER_LANES), I32)],
        scratch_shapes=[pltpu.VMEM((1, ROUTER_LANES), F32)],
        compiler_params=_cparams(("arbitrary",)),
        name="merge",
    )(y_ssm, o_attn, *([proj] * (2 * GATE_BLOCKS)), xa, xb, wglu, bglu, wbs, wba, wout, g2, wr_hi, wr_lo, br)


SUB = 128
SUPER = 1152
TF = 256
NF = D_FF // TF
MOE_WIDE = SUPER // SUB
ROWS_PER_STEP = SUPER // NF


def _moe_sizes(m):
    a = m * TOP_K
    ns_max = N_EXPERTS + a // SUPER
    nb_max = N_EXPERTS + a // SUB
    return a, ns_max, nb_max


def _route(top_i, rank, counts, m):
    a, ns_max, nb_max = _moe_sizes(m)
    experts = jnp.arange(N_EXPERTS, dtype=I32)
    nsb = (counts + SUPER - 1) // SUPER
    sb_end = jnp.cumsum(nsb)
    sb_start = sb_end - nsb
    nb = (counts + SUB - 1) // SUB
    b_end = jnp.cumsum(nb)
    n_c = nb_max * SUB
    cslot = jnp.sum(jnp.where(top_i[:, :, None] == experts, (b_end - nb) * SUB, 0), axis=-1) + rank
    row_of = jnp.arange(TOP_K, dtype=I32)[None, :] * m + jnp.arange(m, dtype=I32)[:, None]
    dest = (a + jnp.arange(n_c, dtype=I32)).at[cslot.reshape(a)].set(row_of.reshape(a), unique_indices=True)
    tok = jnp.pad(jnp.minimum(dest, a - 1) % m, (0, SUPER))
    s_idx = jnp.arange(ns_max, dtype=I32)
    ns_used = sb_end[-1]
    s_eff = jnp.maximum(jnp.minimum(s_idx, ns_used - 1), 0)
    se = jnp.minimum(jnp.sum((s_eff[:, None] >= sb_end[None, :]).astype(I32), axis=1), N_EXPERTS - 1)
    of_se = lambda v: jnp.sum(jnp.where(se[:, None] == experts, v, 0), axis=-1)
    rows_in = jnp.clip(of_se(counts) - (s_eff - of_se(sb_start)) * SUPER, 0, SUPER)
    nsub = jnp.where(s_idx < ns_used, (rows_in + SUB - 1) // SUB, 0).astype(I32)
    cstart = (of_se(b_end - nb) * SUB + (s_eff - of_se(sb_start)) * SUPER).astype(I32)
    return dict(se=se.astype(I32), nsub=nsub, cstart=cstart, ns_used=ns_used.reshape(1).astype(I32),
                tok=tok.astype(I32), dest=dest)


_COPY_UNROLL = 16


def _moe_kernel(se_ref, nsub_ref, cstart_ref, nused_ref, tok_ref, dest_ref,
                h_hbm, wg_ref, wl_ref, bg_ref, bl_ref, wd_ref, bd_ref, y_hbm,
                wg_s, wl_s, wd_s, x_ref, xf32, acc, sem, sem_x):
    s = pl.program_id(0)
    f = pl.program_id(1)
    n = nsub_ref[s]

    def token_row(tok, r, wait):
        cp = pltpu.make_async_copy(h_hbm.at[pl.ds(tok, 1), :], xf32.at[pl.ds(r, 1), :], sem_x.at[0])
        cp.wait() if wait else cp.start()

    def all_token_rows(sb, wait):
        base = cstart_ref[sb]

        def body(g, c):
            for u in range(_COPY_UNROLL):
                r = g * _COPY_UNROLL + u
                token_row(0 if wait else tok_ref[base + r], r, wait)
            return c
        lax.fori_loop(0, SUPER // _COPY_UNROLL, body, 0)

    def next_token_rows():
        base = cstart_ref[jnp.minimum(s + 1, pl.num_programs(0) - 1)] + f * ROWS_PER_STEP
        r0 = pl.multiple_of(f * ROWS_PER_STEP, 16)
        for u in range(ROWS_PER_STEP):
            token_row(tok_ref[base + u], r0 + u, False)

    def result_rows(sb, wait):
        sl = sb % 2
        base = cstart_ref[sb]

        def body(g, c):
            for u in range(_COPY_UNROLL):
                r = pl.multiple_of(g * _COPY_UNROLL, _COPY_UNROLL) + u
                row = 0 if wait else dest_ref[base + r]
                cp = pltpu.make_async_copy(acc.at[sl, pl.ds(r, 1), :], y_hbm.at[pl.ds(row, 1), :], sem.at[sl])
                cp.wait() if wait else cp.start()
            return c
        lax.fori_loop(0, nsub_ref[sb] * (SUB // _COPY_UNROLL), body, 0)

    @pl.when(n > 0)
    def _work():
        o_ref = acc.at[s % 2]
        wg_s[...] = wg_ref[0].astype(BF16)
        wl_s[...] = wl_ref[0].astype(BF16)
        wd_s[...] = wd_ref[0].astype(BF16)

        @pl.when(f == 0)
        def _init():
            @pl.when(s == 0)
            def _first_rows():
                all_token_rows(0, False)
            all_token_rows(s, True)
            x_ref[...] = xf32[...].astype(BF16)

            @pl.when(s >= 2)
            def _free_slot():
                result_rows(s - 2, True)
            o_ref[...] = jnp.broadcast_to(bd_ref[0], o_ref.shape)

        bg = bg_ref[0]
        bl = bl_ref[0]

        def expert_rows(r0, rows):
            xs = x_ref[pl.ds(r0, rows), :]
            hg = jnp.dot(xs, wg_s[...], preferred_element_type=F32) + bg
            hl = jnp.dot(xs, wl_s[...], preferred_element_type=F32) + bl
            hg = jnp.minimum(hg, SWIGLU_LIMIT)
            hl = jnp.clip(hl, -SWIGLU_LIMIT, SWIGLU_LIMIT)
            act = hg * _sigmoid(SWIGLU_ALPHA * hg) * (hl + 1.0)
            o_ref[pl.ds(r0, rows), :] += jnp.dot(act.astype(BF16), wd_s[...], preferred_element_type=F32)

        @pl.when(n == MOE_WIDE)
        def _full():
            next_token_rows()
            expert_rows(0, SUPER)

        @pl.when(n < MOE_WIDE)
        def _partial():
            next_token_rows()

            def single_body(i, c):
                expert_rows(pl.multiple_of(i * SUB, SUB), SUB)
                return c
            lax.fori_loop(0, n, single_body, 0)

        @pl.when(f == NF - 1)
        def _send():
            result_rows(s, False)

    @pl.when(jnp.logical_and(s == pl.num_programs(0) - 1, f == NF - 1))
    def _drain():
        n_used = nused_ref[0]
        all_token_rows(0, True)

        @pl.when(n_used >= 2)
        def _():
            result_rows(n_used - 2, True)
        result_rows(n_used - 1, True)


def _moe(h2, route, w_up, b_up, w_down, b_down, m):
    a, ns_max, nb_max = _moe_sizes(m)
    sp = lambda shape, imap: pl.BlockSpec(shape, lambda s, f, se, *_: imap(s, f, se))
    return pl.pallas_call(
        _moe_kernel,
        grid_spec=pltpu.PrefetchScalarGridSpec(
            num_scalar_prefetch=6,
            grid=(ns_max, NF),
            in_specs=[
                pl.BlockSpec(memory_space=pl.ANY),
                sp((1, D_MODEL, TF), lambda s, f, se: (se[s], 0, f)),
                sp((1, D_MODEL, TF), lambda s, f, se: (se[s], 0, NF + f)),
                sp((1, 1, TF), lambda s, f, se: (se[s], 0, f)),
                sp((1, 1, TF), lambda s, f, se: (se[s], 0, NF + f)),
                sp((1, TF, D_MODEL), lambda s, f, se: (se[s], f, 0)),
                sp((1, 1, D_MODEL), lambda s, f, se: (se[s], 0, 0)),
            ],
            out_specs=pl.BlockSpec(memory_space=pl.ANY),
            scratch_shapes=[pltpu.VMEM((D_MODEL, TF), BF16), pltpu.VMEM((D_MODEL, TF), BF16),
                            pltpu.VMEM((TF, D_MODEL), BF16), pltpu.VMEM((SUPER, D_MODEL), BF16),
                            pltpu.VMEM((SUPER, D_MODEL), F32), pltpu.VMEM((2, SUPER, D_MODEL), F32),
                            pltpu.SemaphoreType.DMA((2,)), pltpu.SemaphoreType.DMA((1,))],
        ),
        out_shape=jax.ShapeDtypeStruct((a + nb_max * SUB, D_MODEL), F32),
        compiler_params=_cparams(("arbitrary", "arbitrary")),
        name="moe_experts",
    )(route['se'], route['nsub'], route['cstart'], route['ns_used'], route['tok'], route['dest'],
      h2, w_up, w_up, b_up, b_up, w_down, b_down)


TC8 = 64


def _combine_kernel(*refs, n_first):
    y_refs = refs[:TOP_K]
    x1_ref, gate_ref, g_ref, o1_ref, o2_ref = refs[TOP_K:]
    b = pl.program_id(0)
    acc = x1_ref[...]
    gate = gate_ref[...]
    for k in range(TOP_K):
        acc = acc + gate[:, k:k + 1] * y_refs[k][...]
    ms = jnp.mean(acc * acc, axis=-1, keepdims=True)
    res = acc * lax.rsqrt(ms + RMS_EPS) * g_ref[...]

    @pl.when(b < n_first)
    def _to_first():
        o1_ref[...] = res

    @pl.when(b >= n_first)
    def _to_second():
        o2_ref[...] = res


def _combine(y_tok, x1, gate, g_final, m_first):
    m = x1.shape[0]
    n_first = m_first // TC8
    return pl.pallas_call(
        functools.partial(_combine_kernel, n_first=n_first),
        grid=(m // TC8,),
        in_specs=[
            *[pl.BlockSpec((TC8, D_MODEL), lambda b, k=k: (k * (m // TC8) + b, 0)) for k in range(TOP_K)],
            pl.BlockSpec((TC8, D_MODEL), lambda b: (b, 0)),
            pl.BlockSpec((TC8, ROUTER_LANES), lambda b: (b, 0)),
            pl.BlockSpec((1, D_MODEL), lambda b: (0, 0)),
        ],
        out_specs=[pl.BlockSpec((TC8, D_MODEL), lambda b: (jnp.minimum(b, n_first - 1), 0)),
                   pl.BlockSpec((TC8, D_MODEL), lambda b: (jnp.maximum(b - n_first, 0), 0))],
        out_shape=[jax.ShapeDtypeStruct((m_first, D_MODEL), F32), jax.ShapeDtypeStruct((m - m_first, D_MODEL), F32)],
        compiler_params=_cparams(("arbitrary",)),
        name="moe_combine",
    )(*([y_tok] * TOP_K), x1, gate, g_final)


SSM_CHUNK = 8


def _rope_tables(seq, t_new, n_batch, n_dec):
    half = HEAD_DIM // 2
    inv = ROPE_THETA ** (-jnp.arange(half, dtype=F32) / half)
    pos = jnp.concatenate([jnp.tile(jnp.arange(seq), n_batch), jnp.tile(PAST_LEN + jnp.arange(t_new), n_dec)])
    ang = pos.astype(F32)[:, None] * inv[None, :]
    cos = jnp.cos(ang)
    sin = jnp.sin(ang)
    return jnp.tile(cos, (1, 4)), jnp.tile(jnp.concatenate([-sin, sin], axis=1), (1, 2))


def kernel(x_prompt, x_sample, state_ssm_re, state_ssm_im, cache_k, cache_v, attn_norm_g, w_in, b_in, ssm_a_re, ssm_a_im, ssm_log_dt, ssm_b_re, ssm_b_im, ssm_c_re, ssm_c_im, ssm_d, w_glu, b_glu, attn_sinks, w_branch_ssm, w_branch_attn, w_out, ffn_norm_g, w_router, b_router, w_up, b_up, w_down, b_down, final_norm_g):
    n_batch, seq, _ = x_prompt.shape
    n_dec, t_new, _ = x_sample.shape
    mp = n_batch * seq
    ms = n_dec * t_new
    m = mp + ms
    g, p, h = SSM_GROUPS, SSM_STATE, SSM_GROUP
    xp = x_prompt.reshape(mp, D_MODEL)
    xs = x_sample.reshape(ms, D_MODEL)

    cos_t, sin_t = _rope_tables(seq, t_new, n_batch, n_dec)
    proj = _inproj(xp, xs, attn_norm_g[0].reshape(1, D_MODEL), w_in[0].astype(BF16), b_in[0].reshape(1, N_IN),
                   cos_t, sin_t)

    ssm_params = (ssm_a_re[0], ssm_a_im[0], ssm_log_dt[0], ssm_b_re[0], ssm_b_im[0], ssm_c_re[0], ssm_c_im[0], ssm_d[0])
    zeros = jnp.zeros((n_batch, g * p), F32)
    y_ssm, sp_re, sp_im = _ssm(proj, 0, m, n_batch, seq // SSM_CHUNK, SSM_CHUNK,
                               _ssm_operators(*ssm_params, SSM_CHUNK, BF16), zeros, zeros)
    y_ssm, ss_re, ss_im = _ssm(proj, mp, m, n_dec, 1, t_new, _ssm_operators(*ssm_params, t_new, F32),
                               state_ssm_re[0].reshape(n_dec, g * p), state_ssm_im[0].reshape(n_dec, g * p), y_prev=y_ssm)

    sinks = attn_sinks[0].reshape(1, N_HEADS)
    o_attn = _attn_prompt(proj, sinks, n_batch, seq, m)
    o_attn = _attn_sample(proj, mp, cache_k[0].reshape(n_dec * WINDOW, D_KV), cache_v[0].reshape(n_dec * WINDOW, D_KV),
                          sinks, n_dec, t_new, o_attn)

    wr = jnp.pad(w_router[0], ((0, 0), (0, ROUTER_LANES - N_EXPERTS)))
    wr_hi = wr.astype(BF16)
    wr_lo = (wr - wr_hi.astype(F32)).astype(BF16)
    br = jnp.pad(b_router[0], (0, ROUTER_LANES - N_EXPERTS), constant_values=NEG_INF).reshape(1, ROUTER_LANES)
    x1, h2, top_i, gate, rank, counts = _merge(
        y_ssm, o_attn, proj, xp, xs, w_glu[0].astype(BF16), b_glu[0].reshape(1, D_SSM), w_branch_ssm[0].astype(BF16),
        w_branch_attn[0].astype(BF16), w_out[0].astype(BF16), ffn_norm_g[0].reshape(1, D_MODEL), wr_hi, wr_lo, br)

    route = _route(top_i[:, :TOP_K], rank[:, :TOP_K], counts[0, :N_EXPERTS], m)
    y_tok = _moe(h2, route, w_up[0], b_up[0].reshape(N_EXPERTS, 1, 2 * D_FF), w_down[0],
                    b_down[0].reshape(N_EXPERTS, 1, D_MODEL), m)
    y_p, y_s = _combine(y_tok, x1, gate, final_norm_g.reshape(1, D_MODEL), mp)

    y_prompt = y_p.reshape(n_batch, seq, D_MODEL)
    y_sample = y_s.reshape(n_dec, t_new, D_MODEL)
    to_state = lambda s: s.reshape(1, s.shape[0], g, p)
    k_all = proj[:, COL_K:COL_K + D_KV]
    v_all = proj[:, COL_V:COL_V + D_KV]
    kv_p = lambda a: a[:mp].reshape(n_batch, seq, N_KV_HEADS, HEAD_DIM)[:, -WINDOW:][None]
    kv_s = lambda a, c: jnp.concatenate([c[0][:, t_new:], a[mp:].reshape(n_dec, t_new, N_KV_HEADS, HEAD_DIM)], axis=1)[None]
    return (y_prompt, y_sample, to_state(sp_re), to_state(sp_im), kv_p(k_all), kv_p(v_all),
            to_state(ss_re), to_state(ss_im), kv_s(k_all, cache_k), kv_s(v_all, cache_v))
```

```python
import functools

import jax
import jax.numpy as jnp
from jax import lax
from jax.experimental import pallas as pl
from jax.experimental.pallas import tpu as pltpu

F32 = jnp.float32
BF16 = jnp.bfloat16
I32 = jnp.int32
HIGHEST = lax.Precision.HIGHEST

D_MODEL = 2048
RMS_EPS = 1e-5
SSM_GROUP = 16
SSM_GROUPS = 64
SSM_STATE = 64
D_SSM = 1024
HEAD_DIM = 64
N_HEADS = 16
N_KV_HEADS = 4
Q_PER_KV = 4
D_ATTN = 1024
D_KV = 256
WINDOW = 128
ROPE_THETA = 10000.0
NEG_INF = -1e30
PAST_LEN = 16384
N_IN = D_SSM + D_ATTN + 2 * D_KV + 2 * D_MODEL
N_EXPERTS = 32
TOP_K = 4
D_FF = 2048
SWIGLU_ALPHA = 1.702
SWIGLU_LIMIT = 7.0

COL_U, COL_Q, COL_K, COL_V, COL_GS, COL_GA = 0, 1024, 2048, 2304, 2560, 4608

VMEM_LIMIT = 56 * 1024 * 1024


def _sigmoid(x):
    return 0.5 * jnp.tanh(0.5 * x) + 0.5


def _qk(q, k):
    return lax.dot_general(q, k, (((1,), (1,)), ((), ())), preferred_element_type=F32)


def _cparams(sem, vmem=VMEM_LIMIT):
    return pltpu.CompilerParams(dimension_semantics=sem, vmem_limit_bytes=vmem)


TM1 = 512
TN1 = 512
_ROPE_TILE0 = COL_Q // TN1
_KV_TILE = COL_K // TN1
_GATE_TILE0 = COL_GS // TN1


def _two_source(i, n_first, a_ref, b_ref):
    return jnp.where(i < n_first, a_ref[...], b_ref[...])


def _two_source_specs(tile, width, n_first):
    return [pl.BlockSpec((tile, width), lambda i, *_: (jnp.minimum(i, n_first - 1), 0)),
            pl.BlockSpec((tile, width), lambda i, *_: (jnp.maximum(i - n_first, 0), 0))]


def _inproj_kernel(xa_ref, xb_ref, g_ref, w_ref, b_ref, cos_ref, sin_ref, o_ref, h_scr, *, n_first):
    j = pl.program_id(1)

    def norm_from(x_ref):
        x = x_ref[...]
        ms = jnp.mean(x * x, axis=-1, keepdims=True)
        h_scr[...] = (x * lax.rsqrt(ms + RMS_EPS) * g_ref[...]).astype(BF16)

    @pl.when(jnp.logical_and(j == 0, pl.program_id(0) < n_first))
    def _norm_a():
        norm_from(xa_ref)

    @pl.when(jnp.logical_and(j == 0, pl.program_id(0) >= n_first))
    def _norm_b():
        norm_from(xb_ref)

    acc = jnp.dot(h_scr[...], w_ref[...], preferred_element_type=F32) + b_ref[...]

    @pl.when(j >= _GATE_TILE0)
    def _gate():
        o_ref[...] = _sigmoid(acc)

    @pl.when(j < _ROPE_TILE0)
    def _plain():
        o_ref[...] = acc

    @pl.when(jnp.logical_and(j >= _ROPE_TILE0, j <= _KV_TILE))
    def _rope():
        c = jnp.concatenate([cos_ref[...]] * (TN1 // 128), axis=1)
        s = jnp.concatenate([sin_ref[...]] * (TN1 // 128), axis=1)
        lane = lax.broadcasted_iota(I32, acc.shape, 1)
        first_half = (lane & (HEAD_DIM // 2)) == 0
        partner = jnp.where(first_half, pltpu.roll(acc, TN1 - HEAD_DIM // 2, 1), pltpu.roll(acc, HEAD_DIM // 2, 1))
        roped = acc * c + partner * s
        n_rot = jnp.where(j == _KV_TILE, D_KV, TN1)
        o_ref[...] = jnp.where(lane < n_rot, roped, acc)


def _inproj(xa, xb, g, w_bf, b, cos_t, sin_t):
    assert xa.shape[0] % TM1 == 0 and xb.shape[0] % TM1 == 0
    m = xa.shape[0] + xb.shape[0]
    n_first = xa.shape[0] // TM1
    return pl.pallas_call(
        functools.partial(_inproj_kernel, n_first=n_first),
        grid=(m // TM1, N_IN // TN1),
        in_specs=_two_source_specs(TM1, D_MODEL, n_first) + [
            pl.BlockSpec((1, D_MODEL), lambda i, j: (0, 0)),
            pl.BlockSpec((D_MODEL, TN1), lambda i, j: (0, j)),
            pl.BlockSpec((1, TN1), lambda i, j: (0, j)),
            pl.BlockSpec((TM1, 128), lambda i, j: (i, 0)),
            pl.BlockSpec((TM1, 128), lambda i, j: (i, 0)),
        ],
        out_specs=pl.BlockSpec((TM1, TN1), lambda i, j: (i, j)),
        out_shape=jax.ShapeDtypeStruct((m, N_IN), F32),
        scratch_shapes=[pltpu.VMEM((TM1, D_MODEL), BF16)],
        compiler_params=_cparams(("parallel", "arbitrary")),
        name="inproj",
    )(xa, xb, g, w_bf, b, cos_t, sin_t)


SSM_GB = 8


def _ssm_operators(a_re, a_im, log_dt, b_re, b_im, c_re, c_im, d_skip, chunk, w_dtype):
    a_re = a_re.astype(F32)
    a_im = a_im.astype(F32)
    dt = jnp.exp(log_dt.astype(F32))[:, None]
    mag = jnp.exp(a_re * dt)
    lb_re = mag * jnp.cos(a_im * dt)
    lb_im = mag * jnp.sin(a_im * dt)
    inv = 1.0 / (a_re * a_re + a_im * a_im)
    f_re = ((lb_re - 1.0) * a_re + lb_im * a_im) * inv
    f_im = (lb_im * a_re - (lb_re - 1.0) * a_im) * inv
    g, p, h = b_re.shape
    nlb = g // SSM_GB
    sw = SSM_GB * p
    br = jnp.transpose(b_re.astype(F32), (0, 2, 1))
    bi = jnp.transpose(b_im.astype(F32), (0, 2, 1))
    bb_re = f_re[:, None, :] * br - f_im[:, None, :] * bi
    bb_im = f_re[:, None, :] * bi + f_im[:, None, :] * br
    k = jnp.arange(chunk + 1, dtype=F32)[:, None, None]
    pmag = jnp.exp(a_re * dt * k)
    pw_re = (pmag * jnp.cos(a_im * dt * k))[:, :, None, :]
    pw_im = (pmag * jnp.sin(a_im * dt * k))[:, :, None, :]
    pb_re = pw_re[:chunk] * bb_re[None] - pw_im[:chunk] * bb_im[None]
    pb_im = pw_re[:chunk] * bb_im[None] + pw_im[:chunk] * bb_re[None]
    cr = c_re.astype(F32)
    ci = c_im.astype(F32)
    cv_re = cr[None] * pw_re[1:] - ci[None] * pw_im[1:]
    cv_im = -(cr[None] * pw_im[1:] + ci[None] * pw_re[1:])

    def lane_block_rows(x):
        return jnp.transpose(x.reshape(x.shape[0], nlb, SSM_GB * h, x.shape[-1]), (1, 0, 2, 3))

    def same_group(rows_per, cols_per, n_rows, n_cols):
        return (jnp.arange(n_rows)[:, None] // rows_per == jnp.arange(n_cols)[None, :] // cols_per).astype(F32)
    spread = jnp.tile(jnp.eye(p, dtype=F32), (1, SSM_GB))

    def block_diag_states(x, dtype):
        y = jnp.einsum('blrp,pc->blrc', lane_block_rows(x), spread, precision=HIGHEST) * same_group(h, p, 128, sw)
        return y.reshape(nlb, chunk * 128, sw).astype(dtype)
    c_cat = jnp.concatenate([cr.reshape(nlb, 128, p), -ci.reshape(nlb, 128, p)], axis=-1)
    pb_cat = jnp.concatenate([lane_block_rows(pb_re), lane_block_rows(pb_im)], axis=-1)
    lag_blocks = jnp.einsum('blrq,bcq->blrc', pb_cat, c_cat, precision=HIGHEST) * same_group(h, h, 128, 128)
    d_row = jnp.tile(d_skip.astype(F32).reshape(nlb, 1, 128), (1, 1, chunk))
    return dict(lag_blocks=lag_blocks.astype(BF16),
                w_re=block_diag_states(pb_re[::-1], w_dtype), w_im=block_diag_states(pb_im[::-1], w_dtype),
                v_re=block_diag_states(cv_re, BF16), v_im=block_diag_states(cv_im, BF16),
                lbl_re=pw_re[chunk].reshape(1, g * p), lbl_im=pw_im[chunk].reshape(1, g * p), d_row=d_row)


def _ssm_kernel(*refs, chunk, n_chunks, n_seq, aliased):
    (u_ref, t_ref, wr_ref, wi_ref, vr_ref, vi_ref, lr_ref, li_ref, d_ref, s0r_ref, s0i_ref) = refs[:11]
    y_ref, sr_ref, si_ref, z_scr, t_scr, lor_scr, loi_scr, pr_scr, pi_scr = refs[11 + aliased:]
    rows = n_seq * n_chunks
    for j in range(chunk):
        for t in range(chunk):
            blk = t_ref[0, t - j] if t >= j else jnp.zeros((128, 128), BF16)
            t_scr[j * 128:(j + 1) * 128, t * 128:(t + 1) * 128] = blk
    for j in range(chunk):
        z_scr[:, j * 128:(j + 1) * 128] = u_ref[pl.ds(j, rows, stride=chunk), :]
    z = z_scr[...]
    zw = z.astype(wr_ref.dtype)
    prec = HIGHEST if wr_ref.dtype == F32 else None
    lor = jnp.dot(zw, wr_ref[0], preferred_element_type=F32, precision=prec)
    loi = jnp.dot(zw, wi_ref[0], preferred_element_type=F32, precision=prec)
    lr = lr_ref[...]
    li = li_ref[...]
    sr = s0r_ref[...]
    si = s0i_ref[...]
    if n_chunks == 1:
        pr, pi = sr, si
        sr, si = lr * sr - li * si + lor, lr * si + li * sr + loi
    else:
        pieces = range(lor.shape[1] // 128)
        cut = lambda a: tuple(a[:, k * 128:(k + 1) * 128] for k in pieces)
        for k in pieces:
            lor_scr[k] = lor[:, k * 128:(k + 1) * 128]
            loi_scr[k] = loi[:, k * 128:(k + 1) * 128]
        lrs, lis = cut(lr), cut(li)

        def step(c, carry):
            idx = pl.ds(c, n_seq, stride=n_chunks)
            nxt = []
            for k, (a, b) in enumerate(zip(*carry)):
                pr_scr[k, idx, :] = a
                pi_scr[k, idx, :] = b
                nxt.append((lrs[k] * a - lis[k] * b + lor_scr[k, idx, :], lrs[k] * b + lis[k] * a + loi_scr[k, idx, :]))
            return tuple(n[0] for n in nxt), tuple(n[1] for n in nxt)
        srs, sis = lax.fori_loop(0, n_chunks, step, (cut(sr), cut(si)), unroll=8)
        sr, si = jnp.concatenate(srs, axis=1), jnp.concatenate(sis, axis=1)
        pr = jnp.concatenate([pr_scr[k] for k in pieces], axis=1)
        pi = jnp.concatenate([pi_scr[k] for k in pieces], axis=1)
    sr_ref[...] = sr
    si_ref[...] = si
    y = jnp.dot(z.astype(BF16), t_scr[...], preferred_element_type=F32)
    y += _qk(pr.astype(BF16), vr_ref[0])
    y += _qk(pi.astype(BF16), vi_ref[0])
    y += z * d_ref[0]
    for j in range(chunk):
        y_ref[pl.ds(j, rows, stride=chunk), :] = y[:, j * 128:(j + 1) * 128]


def _ssm(proj, row0, m_out, n_seq, n_chunks, chunk, ops, s0_re, s0_im, y_prev=None):
    rows = n_seq * n_chunks
    tok = rows * chunk
    nlb = SSM_GROUPS // SSM_GB
    width = chunk * 128
    sw = SSM_GB * SSM_STATE
    blk0 = row0 // tok
    per_lb = lambda a, b: pl.BlockSpec((1, a, b), lambda i: (i, 0, 0))
    lanes = lambda r, w: pl.BlockSpec((r, w), lambda i: (0, i))
    in_specs = [pl.BlockSpec((tok, 128), lambda i: (blk0, COL_U // 128 + i)),
                pl.BlockSpec((1, chunk, 128, 128), lambda i: (i, 0, 0, 0)),
                per_lb(width, sw), per_lb(width, sw), per_lb(width, sw), per_lb(width, sw),
                lanes(1, sw), lanes(1, sw), per_lb(1, width), lanes(n_seq, sw), lanes(n_seq, sw)]
    args = [proj, ops['lag_blocks'], ops['w_re'], ops['w_im'], ops['v_re'], ops['v_im'],
            ops['lbl_re'], ops['lbl_im'], ops['d_row'], s0_re, s0_im]
    aliases = {}
    if y_prev is not None:
        in_specs.append(pl.BlockSpec(memory_space=pl.ANY))
        args.append(y_prev)
        aliases = {len(args) - 1: 0}
    state = jax.ShapeDtypeStruct((n_seq, SSM_GROUPS * SSM_STATE), F32)
    return pl.pallas_call(
        functools.partial(_ssm_kernel, chunk=chunk, n_chunks=n_chunks, n_seq=n_seq, aliased=int(y_prev is not None)),
        grid=(nlb,),
        in_specs=in_specs,
        out_specs=[pl.BlockSpec((tok, 128), lambda i: (blk0, i)), lanes(n_seq, sw), lanes(n_seq, sw)],
        out_shape=[jax.ShapeDtypeStruct((m_out, D_SSM), F32), state, state],
        scratch_shapes=[pltpu.VMEM((rows, width), F32), pltpu.VMEM((width, width), BF16)]
        + [pltpu.VMEM((sw // 128, rows, 128), F32)] * 4,
        input_output_aliases=aliases,
        compiler_params=_cparams(("parallel",)),
        name=f"ssm_c{n_chunks}",
    )(*args)


def _softmax_sink_pv(parts, sink):
    masked = [jnp.where(valid, s, NEG_INF) for s, valid, _ in parts]
    m = sink
    for s in masked:
        m = jnp.maximum(m, jnp.max(s, axis=-1, keepdims=True))
    denom = jnp.exp(sink - m)
    out = None
    for s, (_, _, v) in zip(masked, parts):
        e = jnp.exp(s - m)
        denom = denom + jnp.sum(e, axis=-1, keepdims=True)
        pv = jnp.dot(e.astype(BF16), v, preferred_element_type=F32)
        out = pv if out is None else out + pv
    return out / denom


ATTN_STACK = 4


def _attn_prompt_kernel(q_ref, kp_ref, kc_ref, vp_ref, vc_ref, sink_ref, o_ref):
    qb = pl.program_id(1)
    rows = ATTN_STACK * WINDOW
    t = lax.broadcasted_iota(I32, (rows, 2 * WINDOW), 0) & (WINDOW - 1)
    col = lax.broadcasted_iota(I32, (rows, 2 * WINDOW), 1)
    valid = jnp.logical_or(jnp.logical_and(jnp.logical_and(col < WINDOW, col > t), qb > 0),
                           jnp.logical_and(col >= WINDOW, (col - WINDOW) <= t))
    scale = HEAD_DIM ** -0.5
    for kv in range(N_KV_HEADS):
        lanes = slice(kv * HEAD_DIM, (kv + 1) * HEAD_DIM)
        k = jnp.concatenate([kp_ref[:, lanes], kc_ref[:, lanes]], axis=0).astype(BF16)
        v = jnp.concatenate([vp_ref[:, lanes], vc_ref[:, lanes]], axis=0).astype(BF16)
        for h0 in range(kv * Q_PER_KV, (kv + 1) * Q_PER_KV, ATTN_STACK):
            heads = range(h0, h0 + ATTN_STACK)
            q = jnp.concatenate([q_ref[:, h * HEAD_DIM:(h + 1) * HEAD_DIM] for h in heads], axis=0).astype(BF16)
            sink = jnp.concatenate(
                [jnp.broadcast_to(sink_ref[0:1, h:h + 1], (WINDOW, 1)) for h in heads], axis=0)
            s = _qk(q, k) * scale
            o = _softmax_sink_pv([(s, valid, v)], sink)
            for i, h in enumerate(heads):
                o_ref[:, h * HEAD_DIM:(h + 1) * HEAD_DIM] = o[i * WINDOW:(i + 1) * WINDOW, :]


def _attn_prompt(proj, sinks, n_batch, seq, m_out):
    nb = seq // WINDOW
    cq, ck, cv = COL_Q // D_ATTN, COL_K // D_KV, COL_V // D_KV
    cur = lambda col: (lambda n, b: (n * nb + b, col))
    prev = lambda col: (lambda n, b: (n * nb + jnp.maximum(b - 1, 0), col))
    return pl.pallas_call(
        _attn_prompt_kernel,
        grid=(n_batch, nb),
        in_specs=[
            pl.BlockSpec((WINDOW, D_ATTN), cur(cq)),
            pl.BlockSpec((WINDOW, D_KV), prev(ck)),
            pl.BlockSpec((WINDOW, D_KV), cur(ck)),
            pl.BlockSpec((WINDOW, D_KV), prev(cv)),
            pl.BlockSpec((WINDOW, D_KV), cur(cv)),
            pl.BlockSpec((1, N_HEADS), lambda n, b: (0, 0)),
        ],
        out_specs=pl.BlockSpec((WINDOW, D_ATTN), lambda n, b: (n * nb + b, 0)),
        out_shape=jax.ShapeDtypeStruct((m_out, D_ATTN), F32),
        compiler_params=_cparams(("parallel", "arbitrary")),
        name="attn_prompt",
    )(proj, proj, proj, proj, proj, sinks)


SAMPLE_GB = 8


def _attn_sample_kernel(q_ref, kn_ref, vn_ref, kc_ref, vc_ref, sink_ref, o_ref, *, t_new):
    tok = SAMPLE_GB * t_new
    rows = Q_PER_KV * tok
    ncache = SAMPLE_GB * WINDOW
    r_c = lax.broadcasted_iota(I32, (rows, ncache), 0)
    c_c = lax.broadcasted_iota(I32, (rows, ncache), 1)
    rb_c = (r_c % tok) // t_new
    rt_c = r_c % t_new
    valid_c = jnp.logical_and(rb_c == c_c // WINDOW, (c_c % WINDOW) > rt_c)
    r_n = lax.broadcasted_iota(I32, (rows, tok), 0)
    c_n = lax.broadcasted_iota(I32, (rows, tok), 1)
    valid_n = jnp.logical_and((r_n % tok) // t_new == c_n // t_new, (c_n % t_new) <= (r_n % t_new))
    scale = HEAD_DIM ** -0.5
    for kv in range(N_KV_HEADS):
        lanes = slice(kv * HEAD_DIM, (kv + 1) * HEAD_DIM)
        kc = kc_ref[:, lanes].astype(BF16)
        vc = vc_ref[:, lanes].astype(BF16)
        kn = kn_ref[:, lanes].astype(BF16)
        vn = vn_ref[:, lanes].astype(BF16)
        heads = [kv * Q_PER_KV + i for i in range(Q_PER_KV)]
        q = jnp.concatenate([q_ref[:, h * HEAD_DIM:(h + 1) * HEAD_DIM] for h in heads], axis=0).astype(BF16)
        sink = jnp.concatenate(
            [jnp.broadcast_to(sink_ref[0:1, h:h + 1], (tok, 1)) for h in heads], axis=0)
        o = _softmax_sink_pv([(_qk(q, kc) * scale, valid_c, vc), (_qk(q, kn) * scale, valid_n, vn)], sink)
        for i, h in enumerate(heads):
            o_ref[:, h * HEAD_DIM:(h + 1) * HEAD_DIM] = o[i * tok:(i + 1) * tok, :]


def _attn_sample_aliased(q_ref, kn_ref, vn_ref, kc_ref, vc_ref, sink_ref, o_prev, o_ref, *, t_new):
    del o_prev
    _attn_sample_kernel(q_ref, kn_ref, vn_ref, kc_ref, vc_ref, sink_ref, o_ref, t_new=t_new)


def _attn_sample(proj, row0, cache_k, cache_v, sinks, n_batch, t_new, o_prev):
    tok = SAMPLE_GB * t_new
    blk0 = row0 // tok
    cq, ck, cv = COL_Q // D_ATTN, COL_K // D_KV, COL_V // D_KV
    new = lambda col: (lambda g: (blk0 + g, col))
    return pl.pallas_call(
        functools.partial(_attn_sample_aliased, t_new=t_new),
        grid=(n_batch // SAMPLE_GB,),
        in_specs=[
            pl.BlockSpec((tok, D_ATTN), new(cq)),
            pl.BlockSpec((tok, D_KV), new(ck)),
            pl.BlockSpec((tok, D_KV), new(cv)),
            pl.BlockSpec((SAMPLE_GB * WINDOW, D_KV), lambda g: (g, 0)),
            pl.BlockSpec((SAMPLE_GB * WINDOW, D_KV), lambda g: (g, 0)),
            pl.BlockSpec((1, N_HEADS), lambda g: (0, 0)),
            pl.BlockSpec(memory_space=pl.ANY),
        ],
        out_specs=pl.BlockSpec((tok, D_ATTN), lambda g: (blk0 + g, 0)),
        out_shape=jax.ShapeDtypeStruct(o_prev.shape, F32),
        input_output_aliases={6: 0},
        compiler_params=_cparams(("parallel",)),
        name="attn_sample",
    )(proj, proj, proj, cache_k, cache_v, sinks, o_prev)


TM4 = 256
ROUTER_LANES = 128
GATE_W = 512
GATE_BLOCKS = D_MODEL // GATE_W
ROW_TILES = D_MODEL // 128


def _gelu_tanh(x):
    return 0.5 * x * (1.0 + jnp.tanh(0.7978845608028654 * (x + 0.044715 * (x * x * x))))


def _merge_kernel(y_ref, o_ref, *refs, n_first):
    gs_refs, ga_refs = refs[:GATE_BLOCKS], refs[GATE_BLOCKS:2 * GATE_BLOCKS]
    (xa_ref, xb_ref, wglu_ref, bglu_ref, wbs_ref, wba_ref, wout_ref, g2_ref, wr_ref, wrl_ref, br_ref,
     x1_ref, h2_ref, ti_ref, tg_ref, rank_ref, cnt_ref, seen_scr) = refs[2 * GATE_BLOCKS:]

    @pl.when(pl.program_id(0) == 0)
    def _reset():
        seen_scr[...] = jnp.zeros(seen_scr.shape, F32)

    y1 = _gelu_tanh(y_ref[...])
    z = jnp.dot(y1.astype(BF16), wglu_ref[...], preferred_element_type=F32) + bglu_ref[...]
    y2 = y1 * _sigmoid(z)
    bs = jnp.dot(y2.astype(BF16), wbs_ref[...], preferred_element_type=F32)
    ba = jnp.dot(o_ref[...].astype(BF16), wba_ref[...], preferred_element_type=F32)
    merged = jnp.concatenate(
        [gs[...] * bs[:, c * GATE_W:(c + 1) * GATE_W] + ga[...] * ba[:, c * GATE_W:(c + 1) * GATE_W]
         for c, (gs, ga) in enumerate(zip(gs_refs, ga_refs))], axis=1)
    x = _two_source(pl.program_id(0), n_first, xa_ref, xb_ref)
    x1 = x + jnp.dot(merged.astype(BF16), wout_ref[...], preferred_element_type=F32)
    x1_ref[...] = x1
    ms = jnp.mean(x1 * x1, axis=-1, keepdims=True)
    h2 = x1 * lax.rsqrt(ms + RMS_EPS) * g2_ref[...]
    for s in range(ROW_TILES):
        h2_ref[pl.ds(s, TM4, stride=ROW_TILES), :] = h2[:, s * 128:(s + 1) * 128]
    h_hi = h2.astype(BF16)
    h_lo = (h2 - h_hi.astype(F32)).astype(BF16)
    logits = (jnp.dot(h_hi, wr_ref[...], preferred_element_type=F32)
              + (jnp.dot(h_hi, wrl_ref[...], preferred_element_type=F32)
                 + jnp.dot(h_lo, wr_ref[...], preferred_element_type=F32))) + br_ref[...]
    lane = lax.broadcasted_iota(I32, logits.shape, 1)
    cur = logits
    top_i = jnp.zeros(logits.shape, I32)
    top_v = jnp.zeros(logits.shape, F32)
    chosen = []
    for k in range(TOP_K):
        mx = jnp.max(cur, axis=-1, keepdims=True)
        idx = jnp.min(jnp.where(cur == mx, lane, ROUTER_LANES), axis=-1, keepdims=True)
        top_i = jnp.where(lane == k, idx, top_i)
        top_v = jnp.where(lane == k, mx, top_v)
        chosen.append(lane == idx)
        cur = jnp.where(chosen[-1], -jnp.inf, cur)
    v0 = jnp.max(jnp.where(lane < TOP_K, top_v, -jnp.inf), axis=-1, keepdims=True)
    e = jnp.where(lane < TOP_K, jnp.exp(top_v - v0), 0.0)
    ti_ref[...] = top_i
    tg_ref[...] = e / jnp.sum(e, axis=-1, keepdims=True)
    hits = sum(jnp.where(c, 1.0, 0.0) for c in chosen)
    r = lax.broadcasted_iota(I32, (TM4, TM4), 0)
    c = lax.broadcasted_iota(I32, (TM4, TM4), 1)
    before = jnp.dot(jnp.where(c < r, 1.0, 0.0).astype(BF16), hits.astype(BF16), preferred_element_type=F32)
    before = before + seen_scr[...]
    rank = jnp.zeros(logits.shape, F32)
    for k in range(TOP_K):
        rank = jnp.where(lane == k, jnp.sum(jnp.where(chosen[k], before, 0.0), axis=-1, keepdims=True), rank)
    rank_ref[...] = rank.astype(I32)
    seen_scr[...] = seen_scr[...] + jnp.sum(hits, axis=0, keepdims=True)
    cnt_ref[...] = seen_scr[...].astype(I32)


def _merge(y_ssm, o_attn, proj, xa, xb, wglu, bglu, wbs, wba, wout, g2, wr_hi, wr_lo, br):
    assert xa.shape[0] % TM4 == 0 and xb.shape[0] % TM4 == 0
    m = xa.shape[0] + xb.shape[0]
    n_first = xa.shape[0] // TM4
    row = lambda w: pl.BlockSpec((TM4, w), lambda i: (i, 0))
    const = lambda a, b: pl.BlockSpec((a, b), lambda i: (0, 0), pipeline_mode=pl.Buffered(1))
    return pl.pallas_call(
        functools.partial(_merge_kernel, n_first=n_first),
        grid=(m // TM4,),
        in_specs=[
            row(D_SSM), row(D_ATTN),
            *[pl.BlockSpec((TM4, GATE_W), lambda i, c=col // GATE_W + k: (i, c))
              for col in (COL_GS, COL_GA) for k in range(GATE_BLOCKS)],
            *_two_source_specs(TM4, D_MODEL, n_first),
            const(D_SSM, D_SSM), const(1, D_SSM), const(D_SSM, D_MODEL), const(D_ATTN, D_MODEL),
            const(D_MODEL, D_MODEL), const(1, D_MODEL), const(D_MODEL, ROUTER_LANES), const(D_MODEL, ROUTER_LANES),
            const(1, ROUTER_LANES),
        ],
        out_specs=[row(D_MODEL), pl.BlockSpec((TM4 * ROW_TILES, 128), lambda i: (i, 0)),
                   row(ROUTER_LANES), row(ROUTER_LANES), row(ROUTER_LANES),
                   pl.BlockSpec((1, ROUTER_LANES), lambda i: (0, 0))],
        out_shape=[jax.ShapeDtypeStruct((m, D_MODEL), F32), jax.ShapeDtypeStruct((m * ROW_TILES, 128), F32),
                   jax.ShapeDtypeStruct((m, ROUTER_LANES), I32), jax.ShapeDtypeStruct((m, ROUTER_LANES), F32),
                   jax.ShapeDtypeStruct((m, ROUTER_LANES), I32), jax.ShapeDtypeStruct((1, ROUTER_LANES), I32)],
        scratch_shapes=[pltpu.VMEM((1, ROUTER_LANES), F32)],
        compiler_params=_cparams(("arbitrary",)),
        name="merge",
    )(y_ssm, o_attn, *([proj] * (2 * GATE_BLOCKS)), xa, xb, wglu, bglu, wbs, wba, wout, g2, wr_hi, wr_lo, br)


SUB = 128
SUPER = 1152
TF = 256
NF = D_FF // TF
MOE_WIDE = SUPER // SUB


def _moe_sizes(m):
    a = m * TOP_K
    ns_max = N_EXPERTS + a // SUPER
    nb_max = N_EXPERTS + a // SUB
    return a, ns_max, nb_max


def _route(top_i, rank, counts, m):
    a, ns_max, nb_max = _moe_sizes(m)
    experts = jnp.arange(N_EXPERTS, dtype=I32)
    nsb = (counts + SUPER - 1) // SUPER
    sb_end = jnp.cumsum(nsb)
    sb_start = sb_end - nsb
    nb = (counts + SUB - 1) // SUB
    b_end = jnp.cumsum(nb)
    n_c = nb_max * SUB
    cslot = jnp.sum(jnp.where(top_i[:, :, None] == experts, (b_end - nb) * SUB, 0), axis=-1) + rank
    row_of = jnp.arange(TOP_K, dtype=I32)[None, :] * m + jnp.arange(m, dtype=I32)[:, None]
    dest = (a + jnp.arange(n_c, dtype=I32)).at[cslot.reshape(a)].set(row_of.reshape(a), unique_indices=True)
    tok = jnp.minimum(dest, a - 1) % m
    s_idx = jnp.arange(ns_max, dtype=I32)
    ns_used = sb_end[-1]
    s_eff = jnp.maximum(jnp.minimum(s_idx, ns_used - 1), 0)
    se = jnp.minimum(jnp.sum((s_eff[:, None] >= sb_end[None, :]).astype(I32), axis=1), N_EXPERTS - 1)
    of_se = lambda v: jnp.sum(jnp.where(se[:, None] == experts, v, 0), axis=-1)
    rows_in = jnp.clip(of_se(counts) - (s_eff - of_se(sb_start)) * SUPER, 0, SUPER)
    nsub = jnp.where(s_idx < ns_used, (rows_in + SUB - 1) // SUB, 0).astype(I32)
    cstart = (of_se(b_end - nb) * SUB + (s_eff - of_se(sb_start)) * SUPER).astype(I32)
    return dict(se=se.astype(I32), nsub=nsub, cstart=cstart, ns_used=ns_used.reshape(1).astype(I32),
                tok=tok.astype(I32), dest=dest)


_COPY_UNROLL = 16


def _moe_kernel(se_ref, nsub_ref, cstart_ref, nused_ref, tok_ref, dest_ref,
                h_hbm, wg_ref, wl_ref, bg_ref, bl_ref, wd_ref, bd_ref, y_hbm,
                wg_s, wl_s, wd_s, x_ref, xstage, acc, sem, sem_x):
    s = pl.program_id(0)
    f = pl.program_id(1)
    n = nsub_ref[s]

    def token_rows(sb, wait):
        base = cstart_ref[sb]

        def body(g, c):
            for u in range(_COPY_UNROLL):
                r = g * _COPY_UNROLL + u
                tok = 0 if wait else tok_ref[base + r]
                cp = pltpu.make_async_copy(
                    h_hbm.at[pl.ds(pl.multiple_of(tok * ROW_TILES, ROW_TILES), ROW_TILES), :],
                    xstage.at[pl.ds(pl.multiple_of(r * ROW_TILES, ROW_TILES), ROW_TILES), :], sem_x.at[0])
                cp.wait() if wait else cp.start()
            return c
        lax.fori_loop(0, nsub_ref[sb] * (SUB // _COPY_UNROLL), body, 0)

    def result_rows(sb, wait):
        sl = sb % 2
        base = cstart_ref[sb]

        def body(g, c):
            for u in range(_COPY_UNROLL):
                r = pl.multiple_of(g * _COPY_UNROLL, _COPY_UNROLL) + u
                row = 0 if wait else dest_ref[base + r]
                cp = pltpu.make_async_copy(acc.at[sl, pl.ds(r, 1), :], y_hbm.at[pl.ds(row, 1), :], sem.at[sl])
                cp.wait() if wait else cp.start()
            return c
        lax.fori_loop(0, nsub_ref[sb] * (SUB // _COPY_UNROLL), body, 0)

    @pl.when(n > 0)
    def _work():
        o_ref = acc.at[s % 2]
        wg_s[...] = wg_ref[0].astype(BF16)
        wl_s[...] = wl_ref[0].astype(BF16)
        wd_s[...] = wd_ref[0].astype(BF16)

        @pl.when(f == 0)
        def _init():
            @pl.when(s == 0)
            def _first_rows():
                token_rows(0, False)
            token_rows(s, True)

            def to_bf16(i, c):
                for t in range(ROW_TILES):
                    src = pl.ds(i * (SUB * ROW_TILES) + t, SUB, stride=ROW_TILES)
                    x_ref[pl.ds(pl.multiple_of(i * SUB, SUB), SUB), t * 128:(t + 1) * 128] = xstage[src, :].astype(BF16)
                return c
            lax.fori_loop(0, n, to_bf16, 0)

            @pl.when(s + 1 < nused_ref[0])
            def _next_rows():
                token_rows(s + 1, False)

            @pl.when(s >= 2)
            def _free_slot():
                result_rows(s - 2, True)
            o_ref[...] = jnp.broadcast_to(bd_ref[0], o_ref.shape)

        bg = bg_ref[0]
        bl = bl_ref[0]

        def expert_rows(r0, rows):
            xs = x_ref[pl.ds(r0, rows), :]
            hg = jnp.dot(xs, wg_s[...], preferred_element_type=F32) + bg
            hl = jnp.dot(xs, wl_s[...], preferred_element_type=F32) + bl
            hg = jnp.minimum(hg, SWIGLU_LIMIT)
            hl = jnp.clip(hl, -SWIGLU_LIMIT, SWIGLU_LIMIT)
            act = hg * _sigmoid(SWIGLU_ALPHA * hg) * (hl + 1.0)
            o_ref[pl.ds(r0, rows), :] += jnp.dot(act.astype(BF16), wd_s[...], preferred_element_type=F32)

        @pl.when(n == MOE_WIDE)
        def _full():
            expert_rows(0, SUPER)

        @pl.when(n < MOE_WIDE)
        def _partial():
            def single_body(i, c):
                expert_rows(pl.multiple_of(i * SUB, SUB), SUB)
                return c
            lax.fori_loop(0, n, single_body, 0)

        @pl.when(f == NF - 1)
        def _send():
            result_rows(s, False)

    @pl.when(jnp.logical_and(s == pl.num_programs(0) - 1, f == NF - 1))
    def _drain():
        n_used = nused_ref[0]

        @pl.when(n_used >= 2)
        def _():
            result_rows(n_used - 2, True)
        result_rows(n_used - 1, True)


def _moe(h2, route, w_up, b_up, w_down, b_down, m):
    a, ns_max, nb_max = _moe_sizes(m)
    sp = lambda shape, imap: pl.BlockSpec(shape, lambda s, f, se, *_: imap(s, f, se))
    return pl.pallas_call(
        _moe_kernel,
        grid_spec=pltpu.PrefetchScalarGridSpec(
            num_scalar_prefetch=6,
            grid=(ns_max, NF),
            in_specs=[
                pl.BlockSpec(memory_space=pl.ANY),
                sp((1, D_MODEL, TF), lambda s, f, se: (se[s], 0, f)),
                sp((1, D_MODEL, TF), lambda s, f, se: (se[s], 0, NF + f)),
                sp((1, 1, TF), lambda s, f, se: (se[s], 0, f)),
                sp((1, 1, TF), lambda s, f, se: (se[s], 0, NF + f)),
                sp((1, TF, D_MODEL), lambda s, f, se: (se[s], f, 0)),
                sp((1, 1, D_MODEL), lambda s, f, se: (se[s], 0, 0)),
            ],
            out_specs=pl.BlockSpec(memory_space=pl.ANY),
            scratch_shapes=[pltpu.VMEM((D_MODEL, TF), BF16), pltpu.VMEM((D_MODEL, TF), BF16),
                            pltpu.VMEM((TF, D_MODEL), BF16), pltpu.VMEM((SUPER, D_MODEL), BF16),
                            pltpu.VMEM((SUPER * ROW_TILES, 128), F32), pltpu.VMEM((2, SUPER, D_MODEL), F32),
                            pltpu.SemaphoreType.DMA((2,)), pltpu.SemaphoreType.DMA((1,))],
        ),
        out_shape=jax.ShapeDtypeStruct((a + nb_max * SUB, D_MODEL), F32),
        compiler_params=_cparams(("arbitrary", "arbitrary")),
        name="moe_experts",
    )(route['se'], route['nsub'], route['cstart'], route['ns_used'], route['tok'], route['dest'],
      h2, w_up, w_up, b_up, b_up, w_down, b_down)


TC8 = 64


def _combine_kernel(*refs, n_first):
    y_refs = refs[:TOP_K]
    x1_ref, gate_ref, g_ref, o1_ref, o2_ref = refs[TOP_K:]
    b = pl.program_id(0)
    acc = x1_ref[...]
    gate = gate_ref[...]
    for k in range(TOP_K):
        acc = acc + gate[:, k:k + 1] * y_refs[k][...]
    ms = jnp.mean(acc * acc, axis=-1, keepdims=True)
    res = acc * lax.rsqrt(ms + RMS_EPS) * g_ref[...]

    @pl.when(b < n_first)
    def _to_first():
        o1_ref[...] = res

    @pl.when(b >= n_first)
    def _to_second():
        o2_ref[...] = res


def _combine(y_tok, x1, gate, g_final, m_first):
    m = x1.shape[0]
    n_first = m_first // TC8
    return pl.pallas_call(
        functools.partial(_combine_kernel, n_first=n_first),
        grid=(m // TC8,),
        in_specs=[
            *[pl.BlockSpec((TC8, D_MODEL), lambda b, k=k: (k * (m // TC8) + b, 0)) for k in range(TOP_K)],
            pl.BlockSpec((TC8, D_MODEL), lambda b: (b, 0)),
            pl.BlockSpec((TC8, ROUTER_LANES), lambda b: (b, 0)),
            pl.BlockSpec((1, D_MODEL), lambda b: (0, 0)),
        ],
        out_specs=[pl.BlockSpec((TC8, D_MODEL), lambda b: (jnp.minimum(b, n_first - 1), 0)),
                   pl.BlockSpec((TC8, D_MODEL), lambda b: (jnp.maximum(b - n_first, 0), 0))],
        out_shape=[jax.ShapeDtypeStruct((m_first, D_MODEL), F32), jax.ShapeDtypeStruct((m - m_first, D_MODEL), F32)],
        compiler_params=_cparams(("arbitrary",)),
        name="moe_combine",
    )(*([y_tok] * TOP_K), x1, gate, g_final)


SSM_CHUNK = 8


def _rope_tables(seq, t_new, n_batch, n_dec):
    half = HEAD_DIM // 2
    inv = ROPE_THETA ** (-jnp.arange(half, dtype=F32) / half)
    pos = jnp.concatenate([jnp.tile(jnp.arange(seq), n_batch), jnp.tile(PAST_LEN + jnp.arange(t_new), n_dec)])
    ang = pos.astype(F32)[:, None] * inv[None, :]
    cos = jnp.cos(ang)
    sin = jnp.sin(ang)
    return jnp.tile(cos, (1, 4)), jnp.tile(jnp.concatenate([-sin, sin], axis=1), (1, 2))


def kernel(x_prompt, x_sample, state_ssm_re, state_ssm_im, cache_k, cache_v, attn_norm_g, w_in, b_in, ssm_a_re, ssm_a_im, ssm_log_dt, ssm_b_re, ssm_b_im, ssm_c_re, ssm_c_im, ssm_d, w_glu, b_glu, attn_sinks, w_branch_ssm, w_branch_attn, w_out, ffn_norm_g, w_router, b_router, w_up, b_up, w_down, b_down, final_norm_g):
    n_batch, seq, _ = x_prompt.shape
    n_dec, t_new, _ = x_sample.shape
    mp = n_batch * seq
    ms = n_dec * t_new
    m = mp + ms
    g, p, h = SSM_GROUPS, SSM_STATE, SSM_GROUP
    xp = x_prompt.reshape(mp, D_MODEL)
    xs = x_sample.reshape(ms, D_MODEL)

    cos_t, sin_t = _rope_tables(seq, t_new, n_batch, n_dec)
    proj = _inproj(xp, xs, attn_norm_g[0].reshape(1, D_MODEL), w_in[0].astype(BF16), b_in[0].reshape(1, N_IN),
                   cos_t, sin_t)

    ssm_params = (ssm_a_re[0], ssm_a_im[0], ssm_log_dt[0], ssm_b_re[0], ssm_b_im[0], ssm_c_re[0], ssm_c_im[0], ssm_d[0])
    zeros = jnp.zeros((n_batch, g * p), F32)
    y_ssm, sp_re, sp_im = _ssm(proj, 0, m, n_batch, seq // SSM_CHUNK, SSM_CHUNK,
                               _ssm_operators(*ssm_params, SSM_CHUNK, BF16), zeros, zeros)
    y_ssm, ss_re, ss_im = _ssm(proj, mp, m, n_dec, 1, t_new, _ssm_operators(*ssm_params, t_new, F32),
                               state_ssm_re[0].reshape(n_dec, g * p), state_ssm_im[0].reshape(n_dec, g * p), y_prev=y_ssm)

    sinks = attn_sinks[0].reshape(1, N_HEADS)
    o_attn = _attn_prompt(proj, sinks, n_batch, seq, m)
    o_attn = _attn_sample(proj, mp, cache_k[0].reshape(n_dec * WINDOW, D_KV), cache_v[0].reshape(n_dec * WINDOW, D_KV),
                          sinks, n_dec, t_new, o_attn)

    wr = jnp.pad(w_router[0], ((0, 0), (0, ROUTER_LANES - N_EXPERTS)))
    wr_hi = wr.astype(BF16)
    wr_lo = (wr - wr_hi.astype(F32)).astype(BF16)
    br = jnp.pad(b_router[0], (0, ROUTER_LANES - N_EXPERTS), constant_values=NEG_INF).reshape(1, ROUTER_LANES)
    x1, h2, top_i, gate, rank, counts = _merge(
        y_ssm, o_attn, proj, xp, xs, w_glu[0].astype(BF16), b_glu[0].reshape(1, D_SSM), w_branch_ssm[0].astype(BF16),
        w_branch_attn[0].astype(BF16), w_out[0].astype(BF16), ffn_norm_g[0].reshape(1, D_MODEL), wr_hi, wr_lo, br)

    route = _route(top_i[:, :TOP_K], rank[:, :TOP_K], counts[0, :N_EXPERTS], m)
    y_tok = _moe(h2, route, w_up[0], b_up[0].reshape(N_EXPERTS, 1, 2 * D_FF), w_down[0],
                    b_down[0].reshape(N_EXPERTS, 1, D_MODEL), m)
    y_p, y_s = _combine(y_tok, x1, gate, final_norm_g.reshape(1, D_MODEL), mp)

    y_prompt = y_p.reshape(n_batch, seq, D_MODEL)
    y_sample = y_s.reshape(n_dec, t_new, D_MODEL)
    to_state = lambda s: s.reshape(1, s.shape[0], g, p)
    k_all = proj[:, COL_K:COL_K + D_KV]
    v_all = proj[:, COL_V:COL_V + D_KV]
    kv_p = lambda a: a[:mp].reshape(n_batch, seq, N_KV_HEADS, HEAD_DIM)[:, -WINDOW:][None]
    kv_s = lambda a, c: jnp.concatenate([c[0][:, t_new:], a[mp:].reshape(n_dec, t_new, N_KV_HEADS, HEAD_DIM)], axis=1)[None]
    return (y_prompt, y_sample, to_state(sp_re), to_state(sp_im), kv_p(k_all), kv_p(v_all),
            to_state(ss_re), to_state(ss_im), kv_s(k_all, cache_k), kv_s(v_all, cache_v))
```

```python
import functools

import jax
import jax.numpy as jnp
from jax import lax
from jax.experimental import pallas as pl
from jax.experimental.pallas import tpu as pltpu

F32 = jnp.float32
BF16 = jnp.bfloat16
I32 = jnp.int32
HIGHEST = lax.Precision.HIGHEST

D_MODEL = 2048
RMS_EPS = 1e-5
SSM_GROUP = 16
SSM_GROUPS = 64
SSM_STATE = 64
D_SSM = 1024
HEAD_DIM = 64
N_HEADS = 16
N_KV_HEADS = 4
Q_PER_KV = 4
D_ATTN = 1024
D_KV = 256
WINDOW = 128
ROPE_THETA = 10000.0
NEG_INF = -1e30
PAST_LEN = 16384
N_IN = D_SSM + D_ATTN + 2 * D_KV + 2 * D_MODEL
N_EXPERTS = 32
TOP_K = 4
D_FF = 2048
SWIGLU_ALPHA = 1.702
SWIGLU_LIMIT = 7.0

COL_U, COL_Q, COL_K, COL_V, COL_GS, COL_GA = 0, 1024, 2048, 2304, 2560, 4608

VMEM_LIMIT = 56 * 1024 * 1024


def _sigmoid(x):
    return 0.5 * jnp.tanh(0.5 * x) + 0.5


def _qk(q, k):
    return lax.dot_general(q, k, (((1,), (1,)), ((), ())), preferred_element_type=F32)


def _cparams(sem, vmem=VMEM_LIMIT):
    return pltpu.CompilerParams(dimension_semantics=sem, vmem_limit_bytes=vmem)


TM1 = 512
TN1 = 512
_ROPE_TILE0 = COL_Q // TN1
_KV_TILE = COL_K // TN1
_GATE_TILE0 = COL_GS // TN1


def _two_source(i, n_first, a_ref, b_ref):
    return jnp.where(i < n_first, a_ref[...], b_ref[...])


def _two_source_specs(tile, width, n_first):
    return [pl.BlockSpec((tile, width), lambda i, *_: (jnp.minimum(i, n_first - 1), 0)),
            pl.BlockSpec((tile, width), lambda i, *_: (jnp.maximum(i - n_first, 0), 0))]


def _inproj_kernel(xa_ref, xb_ref, g_ref, w_ref, b_ref, cos_ref, sin_ref, o_ref, h_scr, *, n_first):
    j = pl.program_id(1)

    def norm_from(x_ref):
        x = x_ref[...]
        ms = jnp.mean(x * x, axis=-1, keepdims=True)
        h_scr[...] = (x * lax.rsqrt(ms + RMS_EPS) * g_ref[...]).astype(BF16)

    @pl.when(jnp.logical_and(j == 0, pl.program_id(0) < n_first))
    def _norm_a():
        norm_from(xa_ref)

    @pl.when(jnp.logical_and(j == 0, pl.program_id(0) >= n_first))
    def _norm_b():
        norm_from(xb_ref)

    acc = jnp.dot(h_scr[...], w_ref[...], preferred_element_type=F32) + b_ref[...]

    @pl.when(j >= _GATE_TILE0)
    def _gate():
        o_ref[...] = _sigmoid(acc)

    @pl.when(j < _ROPE_TILE0)
    def _plain():
        o_ref[...] = acc

    @pl.when(jnp.logical_and(j >= _ROPE_TILE0, j <= _KV_TILE))
    def _rope():
        c = jnp.concatenate([cos_ref[...]] * (TN1 // 128), axis=1)
        s = jnp.concatenate([sin_ref[...]] * (TN1 // 128), axis=1)
        lane = lax.broadcasted_iota(I32, acc.shape, 1)
        first_half = (lane & (HEAD_DIM // 2)) == 0
        partner = jnp.where(first_half, pltpu.roll(acc, TN1 - HEAD_DIM // 2, 1), pltpu.roll(acc, HEAD_DIM // 2, 1))
        roped = acc * c + partner * s
        n_rot = jnp.where(j == _KV_TILE, D_KV, TN1)
        o_ref[...] = jnp.where(lane < n_rot, roped, acc)


def _inproj(xa, xb, g, w_bf, b, cos_t, sin_t):
    assert xa.shape[0] % TM1 == 0 and xb.shape[0] % TM1 == 0
    m = xa.shape[0] + xb.shape[0]
    n_first = xa.shape[0] // TM1
    return pl.pallas_call(
        functools.partial(_inproj_kernel, n_first=n_first),
        grid=(m // TM1, N_IN // TN1),
        in_specs=_two_source_specs(TM1, D_MODEL, n_first) + [
            pl.BlockSpec((1, D_MODEL), lambda i, j: (0, 0)),
            pl.BlockSpec((D_MODEL, TN1), lambda i, j: (0, j)),
            pl.BlockSpec((1, TN1), lambda i, j: (0, j)),
            pl.BlockSpec((TM1, 128), lambda i, j: (i, 0)),
            pl.BlockSpec((TM1, 128), lambda i, j: (i, 0)),
        ],
        out_specs=pl.BlockSpec((TM1, TN1), lambda i, j: (i, j)),
        out_shape=jax.ShapeDtypeStruct((m, N_IN), F32),
        scratch_shapes=[pltpu.VMEM((TM1, D_MODEL), BF16)],
        compiler_params=_cparams(("parallel", "arbitrary")),
        name="inproj",
    )(xa, xb, g, w_bf, b, cos_t, sin_t)


SSM_GB = 8


def _ssm_operators(a_re, a_im, log_dt, b_re, b_im, c_re, c_im, d_skip, chunk, w_dtype):
    a_re = a_re.astype(F32)
    a_im = a_im.astype(F32)
    dt = jnp.exp(log_dt.astype(F32))[:, None]
    mag = jnp.exp(a_re * dt)
    lb_re = mag * jnp.cos(a_im * dt)
    lb_im = mag * jnp.sin(a_im * dt)
    inv = 1.0 / (a_re * a_re + a_im * a_im)
    f_re = ((lb_re - 1.0) * a_re + lb_im * a_im) * inv
    f_im = (lb_im * a_re - (lb_re - 1.0) * a_im) * inv
    g, p, h = b_re.shape
    nlb = g // SSM_GB
    sw = SSM_GB * p
    br = jnp.transpose(b_re.astype(F32), (0, 2, 1))
    bi = jnp.transpose(b_im.astype(F32), (0, 2, 1))
    bb_re = f_re[:, None, :] * br - f_im[:, None, :] * bi
    bb_im = f_re[:, None, :] * bi + f_im[:, None, :] * br
    k = jnp.arange(chunk + 1, dtype=F32)[:, None, None]
    pmag = jnp.exp(a_re * dt * k)
    pw_re = (pmag * jnp.cos(a_im * dt * k))[:, :, None, :]
    pw_im = (pmag * jnp.sin(a_im * dt * k))[:, :, None, :]
    pb_re = pw_re[:chunk] * bb_re[None] - pw_im[:chunk] * bb_im[None]
    pb_im = pw_re[:chunk] * bb_im[None] + pw_im[:chunk] * bb_re[None]
    cr = c_re.astype(F32)
    ci = c_im.astype(F32)
    cv_re = cr[None] * pw_re[1:] - ci[None] * pw_im[1:]
    cv_im = -(cr[None] * pw_im[1:] + ci[None] * pw_re[1:])

    def lane_block_rows(x):
        return jnp.transpose(x.reshape(x.shape[0], nlb, SSM_GB * h, x.shape[-1]), (1, 0, 2, 3))

    def same_group(rows_per, cols_per, n_rows, n_cols):
        return (jnp.arange(n_rows)[:, None] // rows_per == jnp.arange(n_cols)[None, :] // cols_per).astype(F32)
    spread = jnp.tile(jnp.eye(p, dtype=F32), (1, SSM_GB))

    def block_diag_states(x, dtype):
        y = jnp.einsum('blrp,pc->blrc', lane_block_rows(x), spread, precision=HIGHEST) * same_group(h, p, 128, sw)
        return y.reshape(nlb, chunk * 128, sw).astype(dtype)
    c_cat = jnp.concatenate([cr.reshape(nlb, 128, p), -ci.reshape(nlb, 128, p)], axis=-1)
    pb_cat = jnp.concatenate([lane_block_rows(pb_re), lane_block_rows(pb_im)], axis=-1)
    lag_blocks = jnp.einsum('blrq,bcq->blrc', pb_cat, c_cat, precision=HIGHEST) * same_group(h, h, 128, 128)
    d_row = jnp.tile(d_skip.astype(F32).reshape(nlb, 1, 128), (1, 1, chunk))
    return dict(lag_blocks=lag_blocks.astype(BF16),
                w_re=block_diag_states(pb_re[::-1], w_dtype), w_im=block_diag_states(pb_im[::-1], w_dtype),
                v_re=block_diag_states(cv_re, BF16), v_im=block_diag_states(cv_im, BF16),
                lbl_re=pw_re[chunk].reshape(1, g * p), lbl_im=pw_im[chunk].reshape(1, g * p), d_row=d_row)


def _ssm_kernel(*refs, chunk, n_chunks, n_seq, aliased):
    (u_ref, t_ref, wr_ref, wi_ref, vr_ref, vi_ref, lr_ref, li_ref, d_ref, s0r_ref, s0i_ref) = refs[:11]
    y_ref, sr_ref, si_ref, z_scr, t_scr, lor_scr, loi_scr, pr_scr, pi_scr = refs[11 + aliased:]
    rows = n_seq * n_chunks
    for j in range(chunk):
        for t in range(chunk):
            blk = t_ref[0, t - j] if t >= j else jnp.zeros((128, 128), BF16)
            t_scr[j * 128:(j + 1) * 128, t * 128:(t + 1) * 128] = blk
    for j in range(chunk):
        z_scr[:, j * 128:(j + 1) * 128] = u_ref[pl.ds(j, rows, stride=chunk), :]
    z = z_scr[...]
    zw = z.astype(wr_ref.dtype)
    prec = HIGHEST if wr_ref.dtype == F32 else None
    lor = jnp.dot(zw, wr_ref[0], preferred_element_type=F32, precision=prec)
    loi = jnp.dot(zw, wi_ref[0], preferred_element_type=F32, precision=prec)
    lr = lr_ref[...]
    li = li_ref[...]
    sr = s0r_ref[...]
    si = s0i_ref[...]
    if n_chunks == 1:
        pr, pi = sr, si
        sr, si = lr * sr - li * si + lor, lr * si + li * sr + loi
    else:
        pieces = range(lor.shape[1] // 128)
        cut = lambda a: tuple(a[:, k * 128:(k + 1) * 128] for k in pieces)
        for k in pieces:
            lor_scr[k] = lor[:, k * 128:(k + 1) * 128]
            loi_scr[k] = loi[:, k * 128:(k + 1) * 128]
        lrs, lis = cut(lr), cut(li)

        def step(c, carry):
            idx = pl.ds(c, n_seq, stride=n_chunks)
            nxt = []
            for k, (a, b) in enumerate(zip(*carry)):
                pr_scr[k, idx, :] = a
                pi_scr[k, idx, :] = b
                nxt.append((lrs[k] * a - lis[k] * b + lor_scr[k, idx, :], lrs[k] * b + lis[k] * a + loi_scr[k, idx, :]))
            return tuple(n[0] for n in nxt), tuple(n[1] for n in nxt)
        srs, sis = lax.fori_loop(0, n_chunks, step, (cut(sr), cut(si)), unroll=8)
        sr, si = jnp.concatenate(srs, axis=1), jnp.concatenate(sis, axis=1)
        pr = jnp.concatenate([pr_scr[k] for k in pieces], axis=1)
        pi = jnp.concatenate([pi_scr[k] for k in pieces], axis=1)
    sr_ref[...] = sr
    si_ref[...] = si
    y = jnp.dot(z.astype(BF16), t_scr[...], preferred_element_type=F32)
    y += _qk(pr.astype(BF16), vr_ref[0])
    y += _qk(pi.astype(BF16), vi_ref[0])
    y += z * d_ref[0]
    for j in range(chunk):
        y_ref[pl.ds(j, rows, stride=chunk), :] = y[:, j * 128:(j + 1) * 128]


def _ssm(proj, row0, m_out, n_seq, n_chunks, chunk, ops, s0_re, s0_im, y_prev=None):
    rows = n_seq * n_chunks
    tok = rows * chunk
    nlb = SSM_GROUPS // SSM_GB
    width = chunk * 128
    sw = SSM_GB * SSM_STATE
    blk0 = row0 // tok
    per_lb = lambda a, b: pl.BlockSpec((1, a, b), lambda i: (i, 0, 0))
    lanes = lambda r, w: pl.BlockSpec((r, w), lambda i: (0, i))
    in_specs = [pl.BlockSpec((tok, 128), lambda i: (blk0, COL_U // 128 + i)),
                pl.BlockSpec((1, chunk, 128, 128), lambda i: (i, 0, 0, 0)),
                per_lb(width, sw), per_lb(width, sw), per_lb(width, sw), per_lb(width, sw),
                lanes(1, sw), lanes(1, sw), per_lb(1, width), lanes(n_seq, sw), lanes(n_seq, sw)]
    args = [proj, ops['lag_blocks'], ops['w_re'], ops['w_im'], ops['v_re'], ops['v_im'],
            ops['lbl_re'], ops['lbl_im'], ops['d_row'], s0_re, s0_im]
    aliases = {}
    if y_prev is not None:
        in_specs.append(pl.BlockSpec(memory_space=pl.ANY))
        args.append(y_prev)
        aliases = {len(args) - 1: 0}
    state = jax.ShapeDtypeStruct((n_seq, SSM_GROUPS * SSM_STATE), F32)
    return pl.pallas_call(
        functools.partial(_ssm_kernel, chunk=chunk, n_chunks=n_chunks, n_seq=n_seq, aliased=int(y_prev is not None)),
        grid=(nlb,),
        in_specs=in_specs,
        out_specs=[pl.BlockSpec((tok, 128), lambda i: (blk0, i)), lanes(n_seq, sw), lanes(n_seq, sw)],
        out_shape=[jax.ShapeDtypeStruct((m_out, D_SSM), F32), state, state],
        scratch_shapes=[pltpu.VMEM((rows, width), F32), pltpu.VMEM((width, width), BF16)]
        + [pltpu.VMEM((sw // 128, rows, 128), F32)] * 4,
        input_output_aliases=aliases,
        compiler_params=_cparams(("parallel",)),
        name=f"ssm_c{n_chunks}",
    )(*args)


def _softmax_sink_pv(parts, sink):
    masked = [jnp.where(valid, s, NEG_INF) for s, valid, _ in parts]
    m = sink
    for s in masked:
        m = jnp.maximum(m, jnp.max(s, axis=-1, keepdims=True))
    denom = jnp.exp(sink - m)
    out = None
    for s, (_, _, v) in zip(masked, parts):
        e = jnp.exp(s - m)
        denom = denom + jnp.sum(e, axis=-1, keepdims=True)
        pv = jnp.dot(e.astype(BF16), v, preferred_element_type=F32)
        out = pv if out is None else out + pv
    return out / denom


ATTN_STACK = 4


def _attn_prompt_kernel(q_ref, kp_ref, kc_ref, vp_ref, vc_ref, sink_ref, o_ref):
    qb = pl.program_id(1)
    rows = ATTN_STACK * WINDOW
    t = lax.broadcasted_iota(I32, (rows, 2 * WINDOW), 0) & (WINDOW - 1)
    col = lax.broadcasted_iota(I32, (rows, 2 * WINDOW), 1)
    valid = jnp.logical_or(jnp.logical_and(jnp.logical_and(col < WINDOW, col > t), qb > 0),
                           jnp.logical_and(col >= WINDOW, (col - WINDOW) <= t))
    scale = HEAD_DIM ** -0.5
    for kv in range(N_KV_HEADS):
        lanes = slice(kv * HEAD_DIM, (kv + 1) * HEAD_DIM)
        k = jnp.concatenate([kp_ref[:, lanes], kc_ref[:, lanes]], axis=0).astype(BF16)
        v = jnp.concatenate([vp_ref[:, lanes], vc_ref[:, lanes]], axis=0).astype(BF16)
        for h0 in range(kv * Q_PER_KV, (kv + 1) * Q_PER_KV, ATTN_STACK):
            heads = range(h0, h0 + ATTN_STACK)
            q = jnp.concatenate([q_ref[:, h * HEAD_DIM:(h + 1) * HEAD_DIM] for h in heads], axis=0).astype(BF16)
            sink = jnp.concatenate(
                [jnp.broadcast_to(sink_ref[0:1, h:h + 1], (WINDOW, 1)) for h in heads], axis=0)
            s = _qk(q, k) * scale
            o = _softmax_sink_pv([(s, valid, v)], sink)
            for i, h in enumerate(heads):
                o_ref[:, h * HEAD_DIM:(h + 1) * HEAD_DIM] = o[i * WINDOW:(i + 1) * WINDOW, :]


def _attn_prompt(proj, sinks, n_batch, seq, m_out):
    nb = seq // WINDOW
    cq, ck, cv = COL_Q // D_ATTN, COL_K // D_KV, COL_V // D_KV
    cur = lambda col: (lambda n, b: (n * nb + b, col))
    prev = lambda col: (lambda n, b: (n * nb + jnp.maximum(b - 1, 0), col))
    return pl.pallas_call(
        _attn_prompt_kernel,
        grid=(n_batch, nb),
        in_specs=[
            pl.BlockSpec((WINDOW, D_ATTN), cur(cq)),
            pl.BlockSpec((WINDOW, D_KV), prev(ck)),
            pl.BlockSpec((WINDOW, D_KV), cur(ck)),
            pl.BlockSpec((WINDOW, D_KV), prev(cv)),
            pl.BlockSpec((WINDOW, D_KV), cur(cv)),
            pl.BlockSpec((1, N_HEADS), lambda n, b: (0, 0)),
        ],
        out_specs=pl.BlockSpec((WINDOW, D_ATTN), lambda n, b: (n * nb + b, 0)),
        out_shape=jax.ShapeDtypeStruct((m_out, D_ATTN), F32),
        compiler_params=_cparams(("parallel", "arbitrary")),
        name="attn_prompt",
    )(proj, proj, proj, proj, proj, sinks)


SAMPLE_GB = 8


def _attn_sample_kernel(q_ref, kn_ref, vn_ref, kc_ref, vc_ref, sink_ref, o_ref, *, t_new):
    tok = SAMPLE_GB * t_new
    rows = Q_PER_KV * tok
    ncache = SAMPLE_GB * WINDOW
    r_c = lax.broadcasted_iota(I32, (rows, ncache), 0)
    c_c = lax.broadcasted_iota(I32, (rows, ncache), 1)
    rb_c = (r_c % tok) // t_new
    rt_c = r_c % t_new
    valid_c = jnp.logical_and(rb_c == c_c // WINDOW, (c_c % WINDOW) > rt_c)
    r_n = lax.broadcasted_iota(I32, (rows, tok), 0)
    c_n = lax.broadcasted_iota(I32, (rows, tok), 1)
    valid_n = jnp.logical_and((r_n % tok) // t_new == c_n // t_new, (c_n % t_new) <= (r_n % t_new))
    scale = HEAD_DIM ** -0.5
    for kv in range(N_KV_HEADS):
        lanes = slice(kv * HEAD_DIM, (kv + 1) * HEAD_DIM)
        kc = kc_ref[:, lanes].astype(BF16)
        vc = vc_ref[:, lanes].astype(BF16)
        kn = kn_ref[:, lanes].astype(BF16)
        vn = vn_ref[:, lanes].astype(BF16)
        heads = [kv * Q_PER_KV + i for i in range(Q_PER_KV)]
        q = jnp.concatenate([q_ref[:, h * HEAD_DIM:(h + 1) * HEAD_DIM] for h in heads], axis=0).astype(BF16)
        sink = jnp.concatenate(
            [jnp.broadcast_to(sink_ref[0:1, h:h + 1], (tok, 1)) for h in heads], axis=0)
        o = _softmax_sink_pv([(_qk(q, kc) * scale, valid_c, vc), (_qk(q, kn) * scale, valid_n, vn)], sink)
        for i, h in enumerate(heads):
            o_ref[:, h * HEAD_DIM:(h + 1) * HEAD_DIM] = o[i * tok:(i + 1) * tok, :]


def _attn_sample_aliased(q_ref, kn_ref, vn_ref, kc_ref, vc_ref, sink_ref, o_prev, o_ref, *, t_new):
    del o_prev
    _attn_sample_kernel(q_ref, kn_ref, vn_ref, kc_ref, vc_ref, sink_ref, o_ref, t_new=t_new)


def _attn_sample(proj, row0, cache_k, cache_v, sinks, n_batch, t_new, o_prev):
    tok = SAMPLE_GB * t_new
    blk0 = row0 // tok
    cq, ck, cv = COL_Q // D_ATTN, COL_K // D_KV, COL_V // D_KV
    new = lambda col: (lambda g: (blk0 + g, col))
    return pl.pallas_call(
        functools.partial(_attn_sample_aliased, t_new=t_new),
        grid=(n_batch // SAMPLE_GB,),
        in_specs=[
            pl.BlockSpec((tok, D_ATTN), new(cq)),
            pl.BlockSpec((tok, D_KV), new(ck)),
            pl.BlockSpec((tok, D_KV), new(cv)),
            pl.BlockSpec((SAMPLE_GB * WINDOW, D_KV), lambda g: (g, 0)),
            pl.BlockSpec((SAMPLE_GB * WINDOW, D_KV), lambda g: (g, 0)),
            pl.BlockSpec((1, N_HEADS), lambda g: (0, 0)),
            pl.BlockSpec(memory_space=pl.ANY),
        ],
        out_specs=pl.BlockSpec((tok, D_ATTN), lambda g: (blk0 + g, 0)),
        out_shape=jax.ShapeDtypeStruct(o_prev.shape, F32),
        input_output_aliases={6: 0},
        compiler_params=_cparams(("parallel",)),
        name="attn_sample",
    )(proj, proj, proj, cache_k, cache_v, sinks, o_prev)


TM4 = 256
ROUTER_LANES = 128
GATE_W = 512
GATE_BLOCKS = D_MODEL // GATE_W
ROW_TILES = D_MODEL // 128


def _gelu_tanh(x):
    return 0.5 * x * (1.0 + jnp.tanh(0.7978845608028654 * (x + 0.044715 * (x * x * x))))


def _merge_kernel(y_ref, o_ref, *refs, n_first):
    gs_refs, ga_refs = refs[:GATE_BLOCKS], refs[GATE_BLOCKS:2 * GATE_BLOCKS]
    (xa_ref, xb_ref, wglu_ref, bglu_ref, wbs_ref, wba_ref, wout_ref, g2_ref, wr_ref, wrl_ref, br_ref,
     x1_ref, h2_ref, ti_ref, tg_ref, rank_ref, cnt_ref, seen_scr) = refs[2 * GATE_BLOCKS:]

    @pl.when(pl.program_id(0) == 0)
    def _reset():
        seen_scr[...] = jnp.zeros(seen_scr.shape, F32)

    y1 = _gelu_tanh(y_ref[...])
    z = jnp.dot(y1.astype(BF16), wglu_ref[...], preferred_element_type=F32) + bglu_ref[...]
    y2 = y1 * _sigmoid(z)
    bs = jnp.dot(y2.astype(BF16), wbs_ref[...], preferred_element_type=F32)
    ba = jnp.dot(o_ref[...].astype(BF16), wba_ref[...], preferred_element_type=F32)
    merged = jnp.concatenate(
        [gs[...] * bs[:, c * GATE_W:(c + 1) * GATE_W] + ga[...] * ba[:, c * GATE_W:(c + 1) * GATE_W]
         for c, (gs, ga) in enumerate(zip(gs_refs, ga_refs))], axis=1)
    x = _two_source(pl.program_id(0), n_first, xa_ref, xb_ref)
    x1 = x + jnp.dot(merged.astype(BF16), wout_ref[...], preferred_element_type=F32)
    x1_ref[...] = x1
    ms = jnp.mean(x1 * x1, axis=-1, keepdims=True)
    h2 = x1 * lax.rsqrt(ms + RMS_EPS) * g2_ref[...]
    for s in range(ROW_TILES):
        h2_ref[pl.ds(s, TM4, stride=ROW_TILES), :] = h2[:, s * 128:(s + 1) * 128]
    h_hi = h2.astype(BF16)
    h_lo = (h2 - h_hi.astype(F32)).astype(BF16)
    logits = (jnp.dot(h_hi, wr_ref[...], preferred_element_type=F32)
              + (jnp.dot(h_hi, wrl_ref[...], preferred_element_type=F32)
                 + jnp.dot(h_lo, wr_ref[...], preferred_element_type=F32))) + br_ref[...]
    lane = lax.broadcasted_iota(I32, logits.shape, 1)
    cur = logits
    top_i = jnp.zeros(logits.shape, I32)
    top_v = jnp.zeros(logits.shape, F32)
    chosen = []
    for k in range(TOP_K):
        mx = jnp.max(cur, axis=-1, keepdims=True)
        idx = jnp.min(jnp.where(cur == mx, lane, ROUTER_LANES), axis=-1, keepdims=True)
        top_i = jnp.where(lane == k, idx, top_i)
        top_v = jnp.where(lane == k, mx, top_v)
        chosen.append(lane == idx)
        cur = jnp.where(chosen[-1], -jnp.inf, cur)
    v0 = jnp.max(jnp.where(lane < TOP_K, top_v, -jnp.inf), axis=-1, keepdims=True)
    e = jnp.where(lane < TOP_K, jnp.exp(top_v - v0), 0.0)
    ti_ref[...] = top_i
    tg_ref[...] = e / jnp.sum(e, axis=-1, keepdims=True)
    hits = sum(jnp.where(c, 1.0, 0.0) for c in chosen)
    r = lax.broadcasted_iota(I32, (TM4, TM4), 0)
    c = lax.broadcasted_iota(I32, (TM4, TM4), 1)
    before = jnp.dot(jnp.where(c < r, 1.0, 0.0).astype(BF16), hits.astype(BF16), preferred_element_type=F32)
    before = before + seen_scr[...]
    rank = jnp.zeros(logits.shape, F32)
    for k in range(TOP_K):
        rank = jnp.where(lane == k, jnp.sum(jnp.where(chosen[k], before, 0.0), axis=-1, keepdims=True), rank)
    rank_ref[...] = rank.astype(I32)
    seen_scr[...] = seen_scr[...] + jnp.sum(hits, axis=0, keepdims=True)
    cnt_ref[...] = seen_scr[...].astype(I32)


def _merge(y_ssm, o_attn, proj, xa, xb, wglu, bglu, wbs, wba, wout, g2, wr_hi, wr_lo, br):
    assert xa.shape[0] % TM4 == 0 and xb.shape[0] % TM4 == 0
    m = xa.shape[0] + xb.shape[0]
    n_first = xa.shape[0] // TM4
    row = lambda w: pl.BlockSpec((TM4, w), lambda i: (i, 0))
    const = lambda a, b: pl.BlockSpec((a, b), lambda i: (0, 0), pipeline_mode=pl.Buffered(1))
    return pl.pallas_call(
        functools.partial(_merge_kernel, n_first=n_first),
        grid=(m // TM4,),
        in_specs=[
            row(D_SSM), row(D_ATTN),
            *[pl.BlockSpec((TM4, GATE_W), lambda i, c=col // GATE_W + k: (i, c))
              for col in (COL_GS, COL_GA) for k in range(GATE_BLOCKS)],
            *_two_source_specs(TM4, D_MODEL, n_first),
            const(D_SSM, D_SSM), const(1, D_SSM), const(D_SSM, D_MODEL), const(D_ATTN, D_MODEL),
            const(D_MODEL, D_MODEL), const(1, D_MODEL), const(D_MODEL, ROUTER_LANES), const(D_MODEL, ROUTER_LANES),
            const(1, ROUTER_LANES),
        ],
        out_specs=[row(D_MODEL), pl.BlockSpec((TM4 * ROW_TILES, 128), lambda i: (i, 0)),
                   row(ROUTER_LANES), row(ROUTER_LANES), row(ROUTER_LANES),
                   pl.BlockSpec((1, ROUTER_LANES), lambda i: (0, 0))],
        out_shape=[jax.ShapeDtypeStruct((m, D_MODEL), F32), jax.ShapeDtypeStruct((m * ROW_TILES, 128), F32),
                   jax.ShapeDtypeStruct((m, ROUTER_LANES), I32), jax.ShapeDtypeStruct((m, ROUTER_LANES), F32),
                   jax.ShapeDtypeStruct((m, ROUTER_LANES), I32), jax.ShapeDtypeStruct((1, ROUTER_LANES), I32)],
        scratch_shapes=[pltpu.VMEM((1, ROUTER_LANES), F32)],
        compiler_params=_cparams(("arbitrary",)),
        name="merge",
    )(y_ssm, o_attn, *([proj] * (2 * GATE_BLOCKS)), xa, xb, wglu, bglu, wbs, wba, wout, g2, wr_hi, wr_lo, br)


SUB = 128
SUPER = 1280
TF = 256
NF = D_FF // TF
MOE_WIDE = 8
MOE_WHOLE = (SUPER // SUB - 1, SUPER // SUB)


def _moe_sizes(m):
    a = m * TOP_K
    ns_max = N_EXPERTS + a // SUPER
    nb_max = N_EXPERTS + a // SUB
    return a, ns_max, nb_max


def _route(top_i, rank, counts, m):
    a, ns_max, nb_max = _moe_sizes(m)
    experts = jnp.arange(N_EXPERTS, dtype=I32)
    nsb = (counts + SUPER - 1) // SUPER
    sb_end = jnp.cumsum(nsb)
    sb_start = sb_end - nsb
    nb = (counts + SUB - 1) // SUB
    b_end = jnp.cumsum(nb)
    n_c = nb_max * SUB
    cslot = jnp.sum(jnp.where(top_i[:, :, None] == experts, (b_end - nb) * SUB, 0), axis=-1) + rank
    row_of = jnp.arange(TOP_K, dtype=I32)[None, :] * m + jnp.arange(m, dtype=I32)[:, None]
    dest = (a + jnp.arange(n_c, dtype=I32)).at[cslot.reshape(a)].set(row_of.reshape(a), unique_indices=True)
    tok = jnp.minimum(dest, a - 1) % m
    s_idx = jnp.arange(ns_max, dtype=I32)
    ns_used = sb_end[-1]
    s_eff = jnp.maximum(jnp.minimum(s_idx, ns_used - 1), 0)
    se = jnp.minimum(jnp.sum((s_eff[:, None] >= sb_end[None, :]).astype(I32), axis=1), N_EXPERTS - 1)
    of_se = lambda v: jnp.sum(jnp.where(se[:, None] == experts, v, 0), axis=-1)
    rows_in = jnp.clip(of_se(counts) - (s_eff - of_se(sb_start)) * SUPER, 0, SUPER)
    nsub = jnp.where(s_idx < ns_used, (rows_in + SUB - 1) // SUB, 0).astype(I32)
    cstart = (of_se(b_end - nb) * SUB + (s_eff - of_se(sb_start)) * SUPER).astype(I32)
    return dict(se=se.astype(I32), nsub=nsub, cstart=cstart, ns_used=ns_used.reshape(1).astype(I32),
                tok=tok.astype(I32), dest=dest)


_COPY_UNROLL = 16


def _moe_kernel(se_ref, nsub_ref, cstart_ref, nused_ref, tok_ref, dest_ref,
                h_hbm, wg_ref, wl_ref, bg_ref, bl_ref, wd_ref, bd_ref, y_hbm,
                wg_s, wl_s, wd_s, x_ref, xstage, acc, sem, sem_x):
    s = pl.program_id(0)
    f = pl.program_id(1)
    n = nsub_ref[s]

    def token_rows(sb, wait):
        base = cstart_ref[sb]

        def body(g, c):
            for u in range(_COPY_UNROLL):
                r = g * _COPY_UNROLL + u
                tok = 0 if wait else tok_ref[base + r]
                cp = pltpu.make_async_copy(
                    h_hbm.at[pl.ds(pl.multiple_of(tok * ROW_TILES, ROW_TILES), ROW_TILES), :],
                    xstage.at[pl.ds(pl.multiple_of(r * ROW_TILES, ROW_TILES), ROW_TILES), :], sem_x.at[0])
                cp.wait() if wait else cp.start()
            return c
        lax.fori_loop(0, nsub_ref[sb] * (SUB // _COPY_UNROLL), body, 0)

    def result_rows(sb, wait):
        sl = sb % 2
        base = cstart_ref[sb]

        def body(g, c):
            for u in range(_COPY_UNROLL):
                r = pl.multiple_of(g * _COPY_UNROLL, _COPY_UNROLL) + u
                row = 0 if wait else dest_ref[base + r]
                cp = pltpu.make_async_copy(acc.at[sl, pl.ds(r, 1), :], y_hbm.at[pl.ds(row, 1), :], sem.at[sl])
                cp.wait() if wait else cp.start()
            return c
        lax.fori_loop(0, nsub_ref[sb] * (SUB // _COPY_UNROLL), body, 0)

    @pl.when(n > 0)
    def _work():
        o_ref = acc.at[s % 2]
        wg_s[...] = wg_ref[0].astype(BF16)
        wl_s[...] = wl_ref[0].astype(BF16)
        wd_s[...] = wd_ref[0].astype(BF16)

        @pl.when(f == 0)
        def _init():
            @pl.when(s == 0)
            def _first_rows():
                token_rows(0, False)
            token_rows(s, True)

            def to_bf16(i, c):
                for t in range(ROW_TILES):
                    src = pl.ds(i * (SUB * ROW_TILES) + t, SUB, stride=ROW_TILES)
                    x_ref[pl.ds(pl.multiple_of(i * SUB, SUB), SUB), t * 128:(t + 1) * 128] = xstage[src, :].astype(BF16)
                return c
            lax.fori_loop(0, n, to_bf16, 0)

            @pl.when(s + 1 < nused_ref[0])
            def _next_rows():
                token_rows(s + 1, False)

            @pl.when(s >= 2)
            def _free_slot():
                result_rows(s - 2, True)
            o_ref[...] = jnp.broadcast_to(bd_ref[0], o_ref.shape)

        bg = bg_ref[0]
        bl = bl_ref[0]

        def expert_rows(r0, rows):
            xs = x_ref[pl.ds(r0, rows), :]
            hg = jnp.dot(xs, wg_s[...], preferred_element_type=F32) + bg
            hl = jnp.dot(xs, wl_s[...], preferred_element_type=F32) + bl
            hg = jnp.minimum(hg, SWIGLU_LIMIT)
            hl = jnp.clip(hl, -SWIGLU_LIMIT, SWIGLU_LIMIT)
            act = hg * _sigmoid(SWIGLU_ALPHA * hg) * (hl + 1.0)
            o_ref[pl.ds(r0, rows), :] += jnp.dot(act.astype(BF16), wd_s[...], preferred_element_type=F32)

        for whole in MOE_WHOLE:
            @pl.when(n == whole)
            def _whole(whole=whole):
                expert_rows(0, whole * SUB)

        @pl.when(functools.reduce(jnp.logical_and, [n != whole for whole in MOE_WHOLE]))
        def _general():
            def wide_body(i, c):
                expert_rows(pl.multiple_of(i * (MOE_WIDE * SUB), MOE_WIDE * SUB), MOE_WIDE * SUB)
                return c
            n_wide = n // MOE_WIDE
            lax.fori_loop(0, n_wide, wide_body, 0)

            def single_body(i, c):
                expert_rows(pl.multiple_of(i * SUB, SUB), SUB)
                return c
            lax.fori_loop(n_wide * MOE_WIDE, n, single_body, 0)

        @pl.when(f == NF - 1)
        def _send():
            result_rows(s, False)

    @pl.when(jnp.logical_and(s == pl.num_programs(0) - 1, f == NF - 1))
    def _drain():
        n_used = nused_ref[0]

        @pl.when(n_used >= 2)
        def _():
            result_rows(n_used - 2, True)
        result_rows(n_used - 1, True)


def _moe(h2, route, w_up, b_up, w_down, b_down, m):
    a, ns_max, nb_max = _moe_sizes(m)
    sp = lambda shape, imap: pl.BlockSpec(shape, lambda s, f, se, *_: imap(s, f, se))
    return pl.pallas_call(
        _moe_kernel,
        grid_spec=pltpu.PrefetchScalarGridSpec(
            num_scalar_prefetch=6,
            grid=(ns_max, NF),
            in_specs=[
                pl.BlockSpec(memory_space=pl.ANY),
                sp((1, D_MODEL, TF), lambda s, f, se: (se[s], 0, f)),
                sp((1, D_MODEL, TF), lambda s, f, se: (se[s], 0, NF + f)),
                sp((1, 1, TF), lambda s, f, se: (se[s], 0, f)),
                sp((1, 1, TF), lambda s, f, se: (se[s], 0, NF + f)),
                sp((1, TF, D_MODEL), lambda s, f, se: (se[s], f, 0)),
                sp((1, 1, D_MODEL), lambda s, f, se: (se[s], 0, 0)),
            ],
            out_specs=pl.BlockSpec(memory_space=pl.ANY),
            scratch_shapes=[pltpu.VMEM((D_MODEL, TF), BF16), pltpu.VMEM((D_MODEL, TF), BF16),
                            pltpu.VMEM((TF, D_MODEL), BF16), pltpu.VMEM((SUPER, D_MODEL), BF16),
                            pltpu.VMEM((SUPER * ROW_TILES, 128), F32), pltpu.VMEM((2, SUPER, D_MODEL), F32),
                            pltpu.SemaphoreType.DMA((2,)), pltpu.SemaphoreType.DMA((1,))],
        ),
        out_shape=jax.ShapeDtypeStruct((a + nb_max * SUB, D_MODEL), F32),
        compiler_params=_cparams(("arbitrary", "arbitrary")),
        name="moe_experts",
    )(route['se'], route['nsub'], route['cstart'], route['ns_used'], route['tok'], route['dest'],
      h2, w_up, w_up, b_up, b_up, w_down, b_down)


TC8 = 64


def _combine_kernel(*refs, n_first):
    y_refs = refs[:TOP_K]
    x1_ref, gate_ref, g_ref, o1_ref, o2_ref = refs[TOP_K:]
    b = pl.program_id(0)
    acc = x1_ref[...]
    gate = gate_ref[...]
    for k in range(TOP_K):
        acc = acc + gate[:, k:k + 1] * y_refs[k][...]
    ms = jnp.mean(acc * acc, axis=-1, keepdims=True)
    res = acc * lax.rsqrt(ms + RMS_EPS) * g_ref[...]

    @pl.when(b < n_first)
    def _to_first():
        o1_ref[...] = res

    @pl.when(b >= n_first)
    def _to_second():
        o2_ref[...] = res


def _combine(y_tok, x1, gate, g_final, m_first):
    m = x1.shape[0]
    n_first = m_first // TC8
    return pl.pallas_call(
        functools.partial(_combine_kernel, n_first=n_first),
        grid=(m // TC8,),
        in_specs=[
            *[pl.BlockSpec((TC8, D_MODEL), lambda b, k=k: (k * (m // TC8) + b, 0)) for k in range(TOP_K)],
            pl.BlockSpec((TC8, D_MODEL), lambda b: (b, 0)),
            pl.BlockSpec((TC8, ROUTER_LANES), lambda b: (b, 0)),
            pl.BlockSpec((1, D_MODEL), lambda b: (0, 0)),
        ],
        out_specs=[pl.BlockSpec((TC8, D_MODEL), lambda b: (jnp.minimum(b, n_first - 1), 0)),
                   pl.BlockSpec((TC8, D_MODEL), lambda b: (jnp.maximum(b - n_first, 0), 0))],
        out_shape=[jax.ShapeDtypeStruct((m_first, D_MODEL), F32), jax.ShapeDtypeStruct((m - m_first, D_MODEL), F32)],
        compiler_params=_cparams(("arbitrary",)),
        name="moe_combine",
    )(*([y_tok] * TOP_K), x1, gate, g_final)


SSM_CHUNK = 8


def _rope_tables(seq, t_new, n_batch, n_dec):
    half = HEAD_DIM // 2
    inv = ROPE_THETA ** (-jnp.arange(half, dtype=F32) / half)
    pos = jnp.concatenate([jnp.tile(jnp.arange(seq), n_batch), jnp.tile(PAST_LEN + jnp.arange(t_new), n_dec)])
    ang = pos.astype(F32)[:, None] * inv[None, :]
    cos = jnp.cos(ang)
    sin = jnp.sin(ang)
    return jnp.tile(cos, (1, 4)), jnp.tile(jnp.concatenate([-sin, sin], axis=1), (1, 2))


def kernel(x_prompt, x_sample, state_ssm_re, state_ssm_im, cache_k, cache_v, attn_norm_g, w_in, b_in, ssm_a_re, ssm_a_im, ssm_log_dt, ssm_b_re, ssm_b_im, ssm_c_re, ssm_c_im, ssm_d, w_glu, b_glu, attn_sinks, w_branch_ssm, w_branch_attn, w_out, ffn_norm_g, w_router, b_router, w_up, b_up, w_down, b_down, final_norm_g):
    n_batch, seq, _ = x_prompt.shape
    n_dec, t_new, _ = x_sample.shape
    mp = n_batch * seq
    ms = n_dec * t_new
    m = mp + ms
    g, p, h = SSM_GROUPS, SSM_STATE, SSM_GROUP
    xp = x_prompt.reshape(mp, D_MODEL)
    xs = x_sample.reshape(ms, D_MODEL)

    cos_t, sin_t = _rope_tables(seq, t_new, n_batch, n_dec)
    proj = _inproj(xp, xs, attn_norm_g[0].reshape(1, D_MODEL), w_in[0].astype(BF16), b_in[0].reshape(1, N_IN),
                   cos_t, sin_t)

    ssm_params = (ssm_a_re[0], ssm_a_im[0], ssm_log_dt[0], ssm_b_re[0], ssm_b_im[0], ssm_c_re[0], ssm_c_im[0], ssm_d[0])
    zeros = jnp.zeros((n_batch, g * p), F32)
    y_ssm, sp_re, sp_im = _ssm(proj, 0, m, n_batch, seq // SSM_CHUNK, SSM_CHUNK,
                               _ssm_operators(*ssm_params, SSM_CHUNK, BF16), zeros, zeros)
    y_ssm, ss_re, ss_im = _ssm(proj, mp, m, n_dec, 1, t_new, _ssm_operators(*ssm_params, t_new, F32),
                               state_ssm_re[0].reshape(n_dec, g * p), state_ssm_im[0].reshape(n_dec, g * p), y_prev=y_ssm)

    sinks = attn_sinks[0].reshape(1, N_HEADS)
    o_attn = _attn_prompt(proj, sinks, n_batch, seq, m)
    o_attn = _attn_sample(proj, mp, cache_k[0].reshape(n_dec * WINDOW, D_KV), cache_v[0].reshape(n_dec * WINDOW, D_KV),
                          sinks, n_dec, t_new, o_attn)

    wr = jnp.pad(w_router[0], ((0, 0), (0, ROUTER_LANES - N_EXPERTS)))
    wr_hi = wr.astype(BF16)
    wr_lo = (wr - wr_hi.astype(F32)).astype(BF16)
    br = jnp.pad(b_router[0], (0, ROUTER_LANES - N_EXPERTS), constant_values=NEG_INF).reshape(1, ROUTER_LANES)
    x1, h2, top_i, gate, rank, counts = _merge(
        y_ssm, o_attn, proj, xp, xs, w_glu[0].astype(BF16), b_glu[0].reshape(1, D_SSM), w_branch_ssm[0].astype(BF16),
        w_branch_attn[0].astype(BF16), w_out[0].astype(BF16), ffn_norm_g[0].reshape(1, D_MODEL), wr_hi, wr_lo, br)

    route = _route(top_i[:, :TOP_K], rank[:, :TOP_K], counts[0, :N_EXPERTS], m)
    y_tok = _moe(h2, route, w_up[0], b_up[0].reshape(N_EXPERTS, 1, 2 * D_FF), w_down[0],
                    b_down[0].reshape(N_EXPERTS, 1, D_MODEL), m)
    y_p, y_s = _combine(y_tok, x1, gate, final_norm_g.reshape(1, D_MODEL), mp)

    y_prompt = y_p.reshape(n_batch, seq, D_MODEL)
    y_sample = y_s.reshape(n_dec, t_new, D_MODEL)
    to_state = lambda s: s.reshape(1, s.shape[0], g, p)
    k_all = proj[:, COL_K:COL_K + D_KV]
    v_all = proj[:, COL_V:COL_V + D_KV]
    kv_p = lambda a: a[:mp].reshape(n_batch, seq, N_KV_HEADS, HEAD_DIM)[:, -WINDOW:][None]
    kv_s = lambda a, c: jnp.concatenate([c[0][:, t_new:], a[mp:].reshape(n_dec, t_new, N_KV_HEADS, HEAD_DIM)], axis=1)[None]
    return (y_prompt, y_sample, to_state(sp_re), to_state(sp_im), kv_p(k_all), kv_p(v_all),
            to_state(ss_re), to_state(ss_im), kv_s(k_all, cache_k), kv_s(v_all, cache_v))
```

```python
import functools

import jax
import jax.numpy as jnp
from jax import lax
from jax.experimental import pallas as pl
from jax.experimental.pallas import tpu as pltpu

F32 = jnp.float32
BF16 = jnp.bfloat16
I32 = jnp.int32
HIGHEST = lax.Precision.HIGHEST

D_MODEL = 2048
RMS_EPS = 1e-5
SSM_GROUP = 16
SSM_GROUPS = 64
SSM_STATE = 64
D_SSM = 1024
HEAD_DIM = 64
N_HEADS = 16
N_KV_HEADS = 4
Q_PER_KV = 4
D_ATTN = 1024
D_KV = 256
WINDOW = 128
ROPE_THETA = 10000.0
NEG_INF = -1e30
PAST_LEN = 16384
N_IN = D_SSM + D_ATTN + 2 * D_KV + 2 * D_MODEL
N_EXPERTS = 32
TOP_K = 4
D_FF = 2048
SWIGLU_ALPHA = 1.702
SWIGLU_LIMIT = 7.0

COL_U, COL_Q, COL_K, COL_V, COL_GS, COL_GA = 0, 1024, 2048, 2304, 2560, 4608

VMEM_LIMIT = 56 * 1024 * 1024


def _sigmoid(x):
    return 0.5 * jnp.tanh(0.5 * x) + 0.5


def _qk(q, k):
    return lax.dot_general(q, k, (((1,), (1,)), ((), ())), preferred_element_type=F32)


def _cparams(sem, vmem=VMEM_LIMIT):
    return pltpu.CompilerParams(dimension_semantics=sem, vmem_limit_bytes=vmem)


TM1 = 512
TN1 = 512
_ROPE_TILE0 = COL_Q // TN1
_KV_TILE = COL_K // TN1
_GATE_TILE0 = COL_GS // TN1


def _two_source(i, n_first, a_ref, b_ref):
    return jnp.where(i < n_first, a_ref[...], b_ref[...])


def _two_source_specs(tile, width, n_first):
    return [pl.BlockSpec((tile, width), lambda i, *_: (jnp.minimum(i, n_first - 1), 0)),
            pl.BlockSpec((tile, width), lambda i, *_: (jnp.maximum(i - n_first, 0), 0))]


def _inproj_kernel(xa_ref, xb_ref, g_ref, w_ref, b_ref, cos_ref, sin_ref, o_ref, h_scr, *, n_first):
    j = pl.program_id(1)

    def norm_from(x_ref):
        x = x_ref[...]
        ms = jnp.mean(x * x, axis=-1, keepdims=True)
        h_scr[...] = (x * lax.rsqrt(ms + RMS_EPS) * g_ref[...]).astype(BF16)

    @pl.when(jnp.logical_and(j == 0, pl.program_id(0) < n_first))
    def _norm_a():
        norm_from(xa_ref)

    @pl.when(jnp.logical_and(j == 0, pl.program_id(0) >= n_first))
    def _norm_b():
        norm_from(xb_ref)

    acc = jnp.dot(h_scr[...], w_ref[...], preferred_element_type=F32) + b_ref[...]

    @pl.when(j >= _GATE_TILE0)
    def _gate():
        o_ref[...] = _sigmoid(acc)

    @pl.when(j < _ROPE_TILE0)
    def _plain():
        o_ref[...] = acc

    @pl.when(jnp.logical_and(j >= _ROPE_TILE0, j <= _KV_TILE))
    def _rope():
        c = jnp.concatenate([cos_ref[...]] * (TN1 // 128), axis=1)
        s = jnp.concatenate([sin_ref[...]] * (TN1 // 128), axis=1)
        lane = lax.broadcasted_iota(I32, acc.shape, 1)
        first_half = (lane & (HEAD_DIM // 2)) == 0
        partner = jnp.where(first_half, pltpu.roll(acc, TN1 - HEAD_DIM // 2, 1), pltpu.roll(acc, HEAD_DIM // 2, 1))
        roped = acc * c + partner * s
        n_rot = jnp.where(j == _KV_TILE, D_KV, TN1)
        o_ref[...] = jnp.where(lane < n_rot, roped, acc)


def _inproj(xa, xb, g, w_bf, b, cos_t, sin_t):
    assert xa.shape[0] % TM1 == 0 and xb.shape[0] % TM1 == 0
    m = xa.shape[0] + xb.shape[0]
    n_first = xa.shape[0] // TM1
    return pl.pallas_call(
        functools.partial(_inproj_kernel, n_first=n_first),
        grid=(m // TM1, N_IN // TN1),
        in_specs=_two_source_specs(TM1, D_MODEL, n_first) + [
            pl.BlockSpec((1, D_MODEL), lambda i, j: (0, 0)),
            pl.BlockSpec((D_MODEL, TN1), lambda i, j: (0, j)),
            pl.BlockSpec((1, TN1), lambda i, j: (0, j)),
            pl.BlockSpec((TM1, 128), lambda i, j: (i, 0)),
            pl.BlockSpec((TM1, 128), lambda i, j: (i, 0)),
        ],
        out_specs=pl.BlockSpec((TM1, TN1), lambda i, j: (i, j)),
        out_shape=jax.ShapeDtypeStruct((m, N_IN), F32),
        scratch_shapes=[pltpu.VMEM((TM1, D_MODEL), BF16)],
        compiler_params=_cparams(("parallel", "arbitrary")),
        name="inproj",
    )(xa, xb, g, w_bf, b, cos_t, sin_t)


SSM_GB = 8


def _ssm_operators(a_re, a_im, log_dt, b_re, b_im, c_re, c_im, d_skip, chunk, w_dtype):
    a_re = a_re.astype(F32)
    a_im = a_im.astype(F32)
    dt = jnp.exp(log_dt.astype(F32))[:, None]
    mag = jnp.exp(a_re * dt)
    lb_re = mag * jnp.cos(a_im * dt)
    lb_im = mag * jnp.sin(a_im * dt)
    inv = 1.0 / (a_re * a_re + a_im * a_im)
    f_re = ((lb_re - 1.0) * a_re + lb_im * a_im) * inv
    f_im = (lb_im * a_re - (lb_re - 1.0) * a_im) * inv
    g, p, h = b_re.shape
    nlb = g // SSM_GB
    sw = SSM_GB * p
    br = jnp.transpose(b_re.astype(F32), (0, 2, 1))
    bi = jnp.transpose(b_im.astype(F32), (0, 2, 1))
    bb_re = f_re[:, None, :] * br - f_im[:, None, :] * bi
    bb_im = f_re[:, None, :] * bi + f_im[:, None, :] * br
    k = jnp.arange(chunk + 1, dtype=F32)[:, None, None]
    pmag = jnp.exp(a_re * dt * k)
    pw_re = (pmag * jnp.cos(a_im * dt * k))[:, :, None, :]
    pw_im = (pmag * jnp.sin(a_im * dt * k))[:, :, None, :]
    pb_re = pw_re[:chunk] * bb_re[None] - pw_im[:chunk] * bb_im[None]
    pb_im = pw_re[:chunk] * bb_im[None] + pw_im[:chunk] * bb_re[None]
    cr = c_re.astype(F32)
    ci = c_im.astype(F32)
    cv_re = cr[None] * pw_re[1:] - ci[None] * pw_im[1:]
    cv_im = -(cr[None] * pw_im[1:] + ci[None] * pw_re[1:])

    def lane_block_rows(x):
        return jnp.transpose(x.reshape(x.shape[0], nlb, SSM_GB * h, x.shape[-1]), (1, 0, 2, 3))

    def same_group(rows_per, cols_per, n_rows, n_cols):
        return (jnp.arange(n_rows)[:, None] // rows_per == jnp.arange(n_cols)[None, :] // cols_per).astype(F32)
    spread = jnp.tile(jnp.eye(p, dtype=F32), (1, SSM_GB))

    def block_diag_states(x, dtype):
        y = jnp.einsum('blrp,pc->blrc', lane_block_rows(x), spread, precision=HIGHEST) * same_group(h, p, 128, sw)
        return y.reshape(nlb, chunk * 128, sw).astype(dtype)
    c_cat = jnp.concatenate([cr.reshape(nlb, 128, p), -ci.reshape(nlb, 128, p)], axis=-1)
    pb_cat = jnp.concatenate([lane_block_rows(pb_re), lane_block_rows(pb_im)], axis=-1)
    lag_blocks = jnp.einsum('blrq,bcq->blrc', pb_cat, c_cat, precision=HIGHEST) * same_group(h, h, 128, 128)
    d_row = jnp.tile(d_skip.astype(F32).reshape(nlb, 1, 128), (1, 1, chunk))
    return dict(lag_blocks=lag_blocks.astype(BF16),
                w_re=block_diag_states(pb_re[::-1], w_dtype), w_im=block_diag_states(pb_im[::-1], w_dtype),
                v_re=block_diag_states(cv_re, BF16), v_im=block_diag_states(cv_im, BF16),
                lbl_re=pw_re[chunk].reshape(1, g * p), lbl_im=pw_im[chunk].reshape(1, g * p), d_row=d_row)


def _ssm_kernel(*refs, chunk, n_chunks, n_seq, aliased):
    (u_ref, t_ref, wr_ref, wi_ref, vr_ref, vi_ref, lr_ref, li_ref, d_ref, s0r_ref, s0i_ref) = refs[:11]
    y_ref, sr_ref, si_ref, z_scr, t_scr, lor_scr, loi_scr, pr_scr, pi_scr = refs[11 + aliased:]
    rows = n_seq * n_chunks
    for j in range(chunk):
        for t in range(chunk):
            blk = t_ref[0, t - j] if t >= j else jnp.zeros((128, 128), BF16)
            t_scr[j * 128:(j + 1) * 128, t * 128:(t + 1) * 128] = blk
    for j in range(chunk):
        z_scr[:, j * 128:(j + 1) * 128] = u_ref[pl.ds(j, rows, stride=chunk), :]
    z = z_scr[...]
    zw = z.astype(wr_ref.dtype)
    prec = HIGHEST if wr_ref.dtype == F32 else None
    lor = jnp.dot(zw, wr_ref[0], preferred_element_type=F32, precision=prec)
    loi = jnp.dot(zw, wi_ref[0], preferred_element_type=F32, precision=prec)
    lr = lr_ref[...]
    li = li_ref[...]
    sr = s0r_ref[...]
    si = s0i_ref[...]
    if n_chunks == 1:
        pr, pi = sr, si
        sr, si = lr * sr - li * si + lor, lr * si + li * sr + loi
    else:
        pieces = range(lor.shape[1] // 128)
        cut = lambda a: tuple(a[:, k * 128:(k + 1) * 128] for k in pieces)
        for k in pieces:
            lor_scr[k] = lor[:, k * 128:(k + 1) * 128]
            loi_scr[k] = loi[:, k * 128:(k + 1) * 128]
        lrs, lis = cut(lr), cut(li)

        def step(c, carry):
            idx = pl.ds(c, n_seq, stride=n_chunks)
            nxt = []
            for k, (a, b) in enumerate(zip(*carry)):
                pr_scr[k, idx, :] = a
                pi_scr[k, idx, :] = b
                nxt.append((lrs[k] * a - lis[k] * b + lor_scr[k, idx, :], lrs[k] * b + lis[k] * a + loi_scr[k, idx, :]))
            return tuple(n[0] for n in nxt), tuple(n[1] for n in nxt)
        srs, sis = lax.fori_loop(0, n_chunks, step, (cut(sr), cut(si)), unroll=8)
        sr, si = jnp.concatenate(srs, axis=1), jnp.concatenate(sis, axis=1)
        pr = jnp.concatenate([pr_scr[k] for k in pieces], axis=1)
        pi = jnp.concatenate([pi_scr[k] for k in pieces], axis=1)
    sr_ref[...] = sr
    si_ref[...] = si
    y = jnp.dot(z.astype(BF16), t_scr[...], preferred_element_type=F32)
    y += _qk(pr.astype(BF16), vr_ref[0])
    y += _qk(pi.astype(BF16), vi_ref[0])
    y += z * d_ref[0]
    for j in range(chunk):
        y_ref[pl.ds(j, rows, stride=chunk), :] = y[:, j * 128:(j + 1) * 128]


def _ssm(proj, row0, m_out, n_seq, n_chunks, chunk, ops, s0_re, s0_im, y_prev=None):
    rows = n_seq * n_chunks
    tok = rows * chunk
    nlb = SSM_GROUPS // SSM_GB
    width = chunk * 128
    sw = SSM_GB * SSM_STATE
    blk0 = row0 // tok
    per_lb = lambda a, b: pl.BlockSpec((1, a, b), lambda i: (i, 0, 0))
    lanes = lambda r, w: pl.BlockSpec((r, w), lambda i: (0, i))
    in_specs = [pl.BlockSpec((tok, 128), lambda i: (blk0, COL_U // 128 + i)),
                pl.BlockSpec((1, chunk, 128, 128), lambda i: (i, 0, 0, 0)),
                per_lb(width, sw), per_lb(width, sw), per_lb(width, sw), per_lb(width, sw),
                lanes(1, sw), lanes(1, sw), per_lb(1, width), lanes(n_seq, sw), lanes(n_seq, sw)]
    args = [proj, ops['lag_blocks'], ops['w_re'], ops['w_im'], ops['v_re'], ops['v_im'],
            ops['lbl_re'], ops['lbl_im'], ops['d_row'], s0_re, s0_im]
    aliases = {}
    if y_prev is not None:
        in_specs.append(pl.BlockSpec(memory_space=pl.ANY))
        args.append(y_prev)
        aliases = {len(args) - 1: 0}
    state = jax.ShapeDtypeStruct((n_seq, SSM_GROUPS * SSM_STATE), F32)
    return pl.pallas_call(
        functools.partial(_ssm_kernel, chunk=chunk, n_chunks=n_chunks, n_seq=n_seq, aliased=int(y_prev is not None)),
        grid=(nlb,),
        in_specs=in_specs,
        out_specs=[pl.BlockSpec((tok, 128), lambda i: (blk0, i)), lanes(n_seq, sw), lanes(n_seq, sw)],
        out_shape=[jax.ShapeDtypeStruct((m_out, D_SSM), F32), state, state],
        scratch_shapes=[pltpu.VMEM((rows, width), F32), pltpu.VMEM((width, width), BF16)]
        + [pltpu.VMEM((sw // 128, rows, 128), F32)] * 4,
        input_output_aliases=aliases,
        compiler_params=_cparams(("parallel",)),
        name=f"ssm_c{n_chunks}",
    )(*args)


def _softmax_sink_pv(parts, sink):
    masked = [jnp.where(valid, s, NEG_INF) for s, valid, _ in parts]
    m = sink
    for s in masked:
        m = jnp.maximum(m, jnp.max(s, axis=-1, keepdims=True))
    denom = jnp.exp(sink - m)
    out = None
    for s, (_, _, v) in zip(masked, parts):
        e = jnp.exp(s - m)
        denom = denom + jnp.sum(e, axis=-1, keepdims=True)
        pv = jnp.dot(e.astype(BF16), v, preferred_element_type=F32)
        out = pv if out is None else out + pv
    return out / denom


ATTN_STACK = 4


def _attn_prompt_kernel(q_ref, kp_ref, kc_ref, vp_ref, vc_ref, sink_ref, o_ref):
    qb = pl.program_id(1)
    rows = ATTN_STACK * WINDOW
    t = lax.broadcasted_iota(I32, (rows, 2 * WINDOW), 0) & (WINDOW - 1)
    col = lax.broadcasted_iota(I32, (rows, 2 * WINDOW), 1)
    valid = jnp.logical_or(jnp.logical_and(jnp.logical_and(col < WINDOW, col > t), qb > 0),
                           jnp.logical_and(col >= WINDOW, (col - WINDOW) <= t))
    scale = HEAD_DIM ** -0.5
    for kv in range(N_KV_HEADS):
        lanes = slice(kv * HEAD_DIM, (kv + 1) * HEAD_DIM)
        k = jnp.concatenate([kp_ref[:, lanes], kc_ref[:, lanes]], axis=0).astype(BF16)
        v = jnp.concatenate([vp_ref[:, lanes], vc_ref[:, lanes]], axis=0).astype(BF16)
        for h0 in range(kv * Q_PER_KV, (kv + 1) * Q_PER_KV, ATTN_STACK):
            heads = range(h0, h0 + ATTN_STACK)
            q = jnp.concatenate([q_ref[:, h * HEAD_DIM:(h + 1) * HEAD_DIM] for h in heads], axis=0).astype(BF16)
            sink = jnp.concatenate(
                [jnp.broadcast_to(sink_ref[0:1, h:h + 1], (WINDOW, 1)) for h in heads], axis=0)
            s = _qk(q, k) * scale
            o = _softmax_sink_pv([(s, valid, v)], sink)
            for i, h in enumerate(heads):
                o_ref[:, h * HEAD_DIM:(h + 1) * HEAD_DIM] = o[i * WINDOW:(i + 1) * WINDOW, :]


def _attn_prompt(proj, sinks, n_batch, seq, m_out):
    nb = seq // WINDOW
    cq, ck, cv = COL_Q // D_ATTN, COL_K // D_KV, COL_V // D_KV
    cur = lambda col: (lambda n, b: (n * nb + b, col))
    prev = lambda col: (lambda n, b: (n * nb + jnp.maximum(b - 1, 0), col))
    return pl.pallas_call(
        _attn_prompt_kernel,
        grid=(n_batch, nb),
        in_specs=[
            pl.BlockSpec((WINDOW, D_ATTN), cur(cq)),
            pl.BlockSpec((WINDOW, D_KV), prev(ck)),
            pl.BlockSpec((WINDOW, D_KV), cur(ck)),
            pl.BlockSpec((WINDOW, D_KV), prev(cv)),
            pl.BlockSpec((WINDOW, D_KV), cur(cv)),
            pl.BlockSpec((1, N_HEADS), lambda n, b: (0, 0)),
        ],
        out_specs=pl.BlockSpec((WINDOW, D_ATTN), lambda n, b: (n * nb + b, 0)),
        out_shape=jax.ShapeDtypeStruct((m_out, D_ATTN), F32),
        compiler_params=_cparams(("parallel", "arbitrary")),
        name="attn_prompt",
    )(proj, proj, proj, proj, proj, sinks)


SAMPLE_GB = 8


def _attn_sample_kernel(q_ref, kn_ref, vn_ref, kc_ref, vc_ref, sink_ref, o_ref, *, t_new):
    tok = SAMPLE_GB * t_new
    rows = Q_PER_KV * tok
    ncache = SAMPLE_GB * WINDOW
    r_c = lax.broadcasted_iota(I32, (rows, ncache), 0)
    c_c = lax.broadcasted_iota(I32, (rows, ncache), 1)
    rb_c = (r_c % tok) // t_new
    rt_c = r_c % t_new
    valid_c = jnp.logical_and(rb_c == c_c // WINDOW, (c_c % WINDOW) > rt_c)
    r_n = lax.broadcasted_iota(I32, (rows, tok), 0)
    c_n = lax.broadcasted_iota(I32, (rows, tok), 1)
    valid_n = jnp.logical_and((r_n % tok) // t_new == c_n // t_new, (c_n % t_new) <= (r_n % t_new))
    scale = HEAD_DIM ** -0.5
    for kv in range(N_KV_HEADS):
        lanes = slice(kv * HEAD_DIM, (kv + 1) * HEAD_DIM)
        kc = kc_ref[:, lanes].astype(BF16)
        vc = vc_ref[:, lanes].astype(BF16)
        kn = kn_ref[:, lanes].astype(BF16)
        vn = vn_ref[:, lanes].astype(BF16)
        heads = [kv * Q_PER_KV + i for i in range(Q_PER_KV)]
        q = jnp.concatenate([q_ref[:, h * HEAD_DIM:(h + 1) * HEAD_DIM] for h in heads], axis=0).astype(BF16)
        sink = jnp.concatenate(
            [jnp.broadcast_to(sink_ref[0:1, h:h + 1], (tok, 1)) for h in heads], axis=0)
        o = _softmax_sink_pv([(_qk(q, kc) * scale, valid_c, vc), (_qk(q, kn) * scale, valid_n, vn)], sink)
        for i, h in enumerate(heads):
            o_ref[:, h * HEAD_DIM:(h + 1) * HEAD_DIM] = o[i * tok:(i + 1) * tok, :]


def _attn_sample_aliased(q_ref, kn_ref, vn_ref, kc_ref, vc_ref, sink_ref, o_prev, o_ref, *, t_new):
    del o_prev
    _attn_sample_kernel(q_ref, kn_ref, vn_ref, kc_ref, vc_ref, sink_ref, o_ref, t_new=t_new)


def _attn_sample(proj, row0, cache_k, cache_v, sinks, n_batch, t_new, o_prev):
    tok = SAMPLE_GB * t_new
    blk0 = row0 // tok
    cq, ck, cv = COL_Q // D_ATTN, COL_K // D_KV, COL_V // D_KV
    new = lambda col: (lambda g: (blk0 + g, col))
    return pl.pallas_call(
        functools.partial(_attn_sample_aliased, t_new=t_new),
        grid=(n_batch // SAMPLE_GB,),
        in_specs=[
            pl.BlockSpec((tok, D_ATTN), new(cq)),
            pl.BlockSpec((tok, D_KV), new(ck)),
            pl.BlockSpec((tok, D_KV), new(cv)),
            pl.BlockSpec((SAMPLE_GB * WINDOW, D_KV), lambda g: (g, 0)),
            pl.BlockSpec((SAMPLE_GB * WINDOW, D_KV), lambda g: (g, 0)),
            pl.BlockSpec((1, N_HEADS), lambda g: (0, 0)),
            pl.BlockSpec(memory_space=pl.ANY),
        ],
        out_specs=pl.BlockSpec((tok, D_ATTN), lambda g: (blk0 + g, 0)),
        out_shape=jax.ShapeDtypeStruct(o_prev.shape, F32),
        input_output_aliases={6: 0},
        compiler_params=_cparams(("parallel",)),
        name="attn_sample",
    )(proj, proj, proj, cache_k, cache_v, sinks, o_prev)


TM4 = 256
ROUTER_LANES = 128
GATE_W = 512
GATE_BLOCKS = D_MODEL // GATE_W
ROW_TILES = D_MODEL // 128


def _gelu_tanh(x):
    return 0.5 * x * (1.0 + jnp.tanh(0.7978845608028654 * (x + 0.044715 * (x * x * x))))


def _merge_kernel(y_ref, o_ref, *refs, n_first):
    gs_refs, ga_refs = refs[:GATE_BLOCKS], refs[GATE_BLOCKS:2 * GATE_BLOCKS]
    (xa_ref, xb_ref, wglu_ref, bglu_ref, wbs_ref, wba_ref, wout_ref, g2_ref, wr_ref, wrl_ref, br_ref,
     x1_ref, h2_ref, ti_ref, tg_ref, rank_ref, cnt_ref, seen_scr) = refs[2 * GATE_BLOCKS:]

    @pl.when(pl.program_id(0) == 0)
    def _reset():
        seen_scr[...] = jnp.zeros(seen_scr.shape, F32)

    y1 = _gelu_tanh(y_ref[...])
    z = jnp.dot(y1.astype(BF16), wglu_ref[...], preferred_element_type=F32) + bglu_ref[...]
    y2 = y1 * _sigmoid(z)
    bs = jnp.dot(y2.astype(BF16), wbs_ref[...], preferred_element_type=F32)
    ba = jnp.dot(o_ref[...].astype(BF16), wba_ref[...], preferred_element_type=F32)
    merged = jnp.concatenate(
        [gs[...] * bs[:, c * GATE_W:(c + 1) * GATE_W] + ga[...] * ba[:, c * GATE_W:(c + 1) * GATE_W]
         for c, (gs, ga) in enumerate(zip(gs_refs, ga_refs))], axis=1)
    x = _two_source(pl.program_id(0), n_first, xa_ref, xb_ref)
    x1 = x + jnp.dot(merged.astype(BF16), wout_ref[...], preferred_element_type=F32)
    x1_ref[...] = x1
    ms = jnp.mean(x1 * x1, axis=-1, keepdims=True)
    h2 = x1 * lax.rsqrt(ms + RMS_EPS) * g2_ref[...]
    for s in range(ROW_TILES):
        h2_ref[pl.ds(s, TM4, stride=ROW_TILES), :] = h2[:, s * 128:(s + 1) * 128]
    h_hi = h2.astype(BF16)
    h_lo = (h2 - h_hi.astype(F32)).astype(BF16)
    logits = (jnp.dot(h_hi, wr_ref[...], preferred_element_type=F32)
              + (jnp.dot(h_hi, wrl_ref[...], preferred_element_type=F32)
                 + jnp.dot(h_lo, wr_ref[...], preferred_element_type=F32))) + br_ref[...]
    lane = lax.broadcasted_iota(I32, logits.shape, 1)
    cur = logits
    top_i = jnp.zeros(logits.shape, I32)
    top_v = jnp.zeros(logits.shape, F32)
    chosen = []
    for k in range(TOP_K):
        mx = jnp.max(cur, axis=-1, keepdims=True)
        idx = jnp.min(jnp.where(cur == mx, lane, ROUTER_LANES), axis=-1, keepdims=True)
        top_i = jnp.where(lane == k, idx, top_i)
        top_v = jnp.where(lane == k, mx, top_v)
        chosen.append(lane == idx)
        cur = jnp.where(chosen[-1], -jnp.inf, cur)
    v0 = jnp.max(jnp.where(lane < TOP_K, top_v, -jnp.inf), axis=-1, keepdims=True)
    e = jnp.where(lane < TOP_K, jnp.exp(top_v - v0), 0.0)
    ti_ref[...] = top_i
    tg_ref[...] = e / jnp.sum(e, axis=-1, keepdims=True)
    hits = sum(jnp.where(c, 1.0, 0.0) for c in chosen)
    r = lax.broadcasted_iota(I32, (TM4, TM4), 0)
    c = lax.broadcasted_iota(I32, (TM4, TM4), 1)
    before = jnp.dot(jnp.where(c < r, 1.0, 0.0).astype(BF16), hits.astype(BF16), preferred_element_type=F32)
    before = before + seen_scr[...]
    rank = jnp.zeros(logits.shape, F32)
    for k in range(TOP_K):
        rank = jnp.where(lane == k, jnp.sum(jnp.where(chosen[k], before, 0.0), axis=-1, keepdims=True), rank)
    rank_ref[...] = rank.astype(I32)
    seen_scr[...] = seen_scr[...] + jnp.sum(hits, axis=0, keepdims=True)
    cnt_ref[...] = seen_scr[...].astype(I32)


def _merge(y_ssm, o_attn, proj, xa, xb, wglu, bglu, wbs, wba, wout, g2, wr_hi, wr_lo, br):
    assert xa.shape[0] % TM4 == 0 and xb.shape[0] % TM4 == 0
    m = xa.shape[0] + xb.shape[0]
    n_first = xa.shape[0] // TM4
    row = lambda w: pl.BlockSpec((TM4, w), lambda i: (i, 0))
    const = lambda a, b: pl.BlockSpec((a, b), lambda i: (0, 0), pipeline_mode=pl.Buffered(1))
    return pl.pallas_call(
        functools.partial(_merge_kernel, n_first=n_first),
        grid=(m // TM4,),
        in_specs=[
            row(D_SSM), row(D_ATTN),
            *[pl.BlockSpec((TM4, GATE_W), lambda i, c=col // GATE_W + k: (i, c))
              for col in (COL_GS, COL_GA) for k in range(GATE_BLOCKS)],
            *_two_source_specs(TM4, D_MODEL, n_first),
            const(D_SSM, D_SSM), const(1, D_SSM), const(D_SSM, D_MODEL), const(D_ATTN, D_MODEL),
            const(D_MODEL, D_MODEL), const(1, D_MODEL), const(D_MODEL, ROUTER_LANES), const(D_MODEL, ROUTER_LANES),
            const(1, ROUTER_LANES),
        ],
        out_specs=[row(D_MODEL), pl.BlockSpec((TM4 * ROW_TILES, 128), lambda i: (i, 0)),
                   row(ROUTER_LANES), row(ROUTER_LANES), row(ROUTER_LANES),
                   pl.BlockSpec((1, ROUTER_LANES), lambda i: (0, 0))],
        out_shape=[jax.ShapeDtypeStruct((m, D_MODEL), F32), jax.ShapeDtypeStruct((m * ROW_TILES, 128), F32),
                   jax.ShapeDtypeStruct((m, ROUTER_LANES), I32), jax.ShapeDtypeStruct((m, ROUTER_LANES), F32),
                   jax.ShapeDtypeStruct((m, ROUTER_LANES), I32), jax.ShapeDtypeStruct((1, ROUTER_LANES), I32)],
        scratch_shapes=[pltpu.VMEM((1, ROUTER_LANES), F32)],
        compiler_params=_cparams(("arbitrary",)),
        name="merge",
    )(y_ssm, o_attn, *([proj] * (2 * GATE_BLOCKS)), xa, xb, wglu, bglu, wbs, wba, wout, g2, wr_hi, wr_lo, br)


SUB = 128
SUPER = 1280
TF = 256
NF = D_FF // TF
MOE_WIDE = 8
MOE_WHOLE = (SUPER // SUB - 1, SUPER // SUB)


def _moe_sizes(m):
    a = m * TOP_K
    ns_max = N_EXPERTS + a // SUPER
    nb_max = N_EXPERTS + a // SUB
    return a, ns_max, nb_max


def _route(top_i, rank, counts, m):
    a, ns_max, nb_max = _moe_sizes(m)
    experts = jnp.arange(N_EXPERTS, dtype=I32)
    nsb = (counts + SUPER - 1) // SUPER
    sb_end = jnp.cumsum(nsb)
    sb_start = sb_end - nsb
    nb = (counts + SUB - 1) // SUB
    b_end = jnp.cumsum(nb)
    n_c = nb_max * SUB
    cslot = jnp.sum(jnp.where(top_i[:, :, None] == experts, (b_end - nb) * SUB, 0), axis=-1) + rank
    row_of = jnp.arange(TOP_K, dtype=I32)[None, :] * m + jnp.arange(m, dtype=I32)[:, None]
    dest = (a + jnp.arange(n_c, dtype=I32)).at[cslot.reshape(a)].set(row_of.reshape(a), unique_indices=True)
    tok = jnp.minimum(dest, a - 1) % m
    s_idx = jnp.arange(ns_max, dtype=I32)
    ns_used = sb_end[-1]
    s_eff = jnp.maximum(jnp.minimum(s_idx, ns_used - 1), 0)
    se = jnp.minimum(jnp.sum((s_eff[:, None] >= sb_end[None, :]).astype(I32), axis=1), N_EXPERTS - 1)
    of_se = lambda v: jnp.sum(jnp.where(se[:, None] == experts, v, 0), axis=-1)
    rows_in = jnp.clip(of_se(counts) - (s_eff - of_se(sb_start)) * SUPER, 0, SUPER)
    nsub = jnp.where(s_idx < ns_used, (rows_in + SUB - 1) // SUB, 0).astype(I32)
    cstart = (of_se(b_end - nb) * SUB + (s_eff - of_se(sb_start)) * SUPER).astype(I32)
    return dict(se=se.astype(I32), nsub=nsub, cstart=cstart, ns_used=ns_used.reshape(1).astype(I32),
                tok=tok.astype(I32), dest=dest)


_COPY_UNROLL = 16


def _moe_kernel(se_ref, nsub_ref, cstart_ref, nused_ref, tok_ref, dest_ref,
                h_hbm, wg_ref, wl_ref, bg_ref, bl_ref, wd_ref, bd_ref, y_hbm,
                wg_s, wl_s, wd_s, x_ref, xstage, acc, sem, sem_x):
    s = pl.program_id(0)
    f = pl.program_id(1)
    n = nsub_ref[s]

    def token_rows(sb, wait):
        base = cstart_ref[sb]

        def body(g, c):
            for u in range(_COPY_UNROLL):
                r = g * _COPY_UNROLL + u
                tok = 0 if wait else tok_ref[base + r]
                cp = pltpu.make_async_copy(
                    h_hbm.at[pl.ds(pl.multiple_of(tok * ROW_TILES, ROW_TILES), ROW_TILES), :],
                    xstage.at[pl.ds(pl.multiple_of(r * ROW_TILES, ROW_TILES), ROW_TILES), :], sem_x.at[0])
                cp.wait() if wait else cp.start()
            return c
        lax.fori_loop(0, nsub_ref[sb] * (SUB // _COPY_UNROLL), body, 0)

    def result_rows(sb, wait):
        sl = sb % 2
        base = cstart_ref[sb]

        def body(g, c):
            for u in range(_COPY_UNROLL):
                r = pl.multiple_of(g * _COPY_UNROLL, _COPY_UNROLL) + u
                row = 0 if wait else dest_ref[base + r]
                cp = pltpu.make_async_copy(acc.at[sl, pl.ds(r, 1), :], y_hbm.at[pl.ds(row, 1), :], sem.at[sl])
                cp.wait() if wait else cp.start()
            return c
        lax.fori_loop(0, nsub_ref[sb] * (SUB // _COPY_UNROLL), body, 0)

    @pl.when(n > 0)
    def _work():
        o_ref = acc.at[s % 2]
        wg_s[...] = wg_ref[0].astype(BF16)
        wl_s[...] = wl_ref[0].astype(BF16)
        wd_s[...] = wd_ref[0].astype(BF16)

        @pl.when(f == 0)
        def _init():
            @pl.when(s == 0)
            def _first_rows():
                token_rows(0, False)
            token_rows(s, True)

            def to_bf16(i, c):
                for t in range(ROW_TILES):
                    src = pl.ds(i * (SUB * ROW_TILES) + t, SUB, stride=ROW_TILES)
                    x_ref[pl.ds(pl.multiple_of(i * SUB, SUB), SUB), t * 128:(t + 1) * 128] = xstage[src, :].astype(BF16)
                return c
            lax.fori_loop(0, n, to_bf16, 0)

            @pl.when(s + 1 < nused_ref[0])
            def _next_rows():
                token_rows(s + 1, False)

            @pl.when(s >= 2)
            def _free_slot():
                result_rows(s - 2, True)
            o_ref[...] = jnp.broadcast_to(bd_ref[0], o_ref.shape)

        bg = bg_ref[0]
        bl = bl_ref[0]

        def expert_rows(r0, rows):
            xs = x_ref[pl.ds(r0, rows), :]
            hg = jnp.dot(xs, wg_s[...], preferred_element_type=F32) + bg
            hl = jnp.dot(xs, wl_s[...], preferred_element_type=F32) + bl
            hg = jnp.minimum(hg, SWIGLU_LIMIT)
            hl = jnp.clip(hl, -SWIGLU_LIMIT, SWIGLU_LIMIT)
            act = hg * _sigmoid(SWIGLU_ALPHA * hg) * (hl + 1.0)
            o_ref[pl.ds(r0, rows), :] += jnp.dot(act.astype(BF16), wd_s[...], preferred_element_type=F32)

        for whole in MOE_WHOLE:
            @pl.when(n == whole)
            def _whole(whole=whole):
                expert_rows(0, whole * SUB)

        @pl.when(functools.reduce(jnp.logical_and, [n != whole for whole in MOE_WHOLE]))
        def _general():
            def wide_body(i, c):
                expert_rows(pl.multiple_of(i * (MOE_WIDE * SUB), MOE_WIDE * SUB), MOE_WIDE * SUB)
                return c
            n_wide = n // MOE_WIDE
            lax.fori_loop(0, n_wide, wide_body, 0)

            def single_body(i, c):
                expert_rows(pl.multiple_of(i * SUB, SUB), SUB)
                return c
            lax.fori_loop(n_wide * MOE_WIDE, n, single_body, 0)

        @pl.when(f == NF - 1)
        def _send():
            result_rows(s, False)

    @pl.when(jnp.logical_and(s == pl.num_programs(0) - 1, f == NF - 1))
    def _drain():
        n_used = nused_ref[0]

        @pl.when(n_used >= 2)
        def _():
            result_rows(n_used - 2, True)
        result_rows(n_used - 1, True)


def _moe(h2, route, w_up, b_up, w_down, b_down, m):
    a, ns_max, nb_max = _moe_sizes(m)
    sp = lambda shape, imap: pl.BlockSpec(shape, lambda s, f, se, *_: imap(s, f, se))
    return pl.pallas_call(
        _moe_kernel,
        grid_spec=pltpu.PrefetchScalarGridSpec(
            num_scalar_prefetch=6,
            grid=(ns_max, NF),
            in_specs=[
                pl.BlockSpec(memory_space=pl.ANY),
                sp((1, D_MODEL, TF), lambda s, f, se: (se[s], 0, f)),
                sp((1, D_MODEL, TF), lambda s, f, se: (se[s], 0, NF + f)),
                sp((1, 1, TF), lambda s, f, se: (se[s], 0, f)),
                sp((1, 1, TF), lambda s, f, se: (se[s], 0, NF + f)),
                sp((1, TF, D_MODEL), lambda s, f, se: (se[s], f, 0)),
                sp((1, 1, D_MODEL), lambda s, f, se: (se[s], 0, 0)),
            ],
            out_specs=pl.BlockSpec(memory_space=pl.ANY),
            scratch_shapes=[pltpu.VMEM((D_MODEL, TF), BF16), pltpu.VMEM((D_MODEL, TF), BF16),
                            pltpu.VMEM((TF, D_MODEL), BF16), pltpu.VMEM((SUPER, D_MODEL), BF16),
                            pltpu.VMEM((SUPER * ROW_TILES, 128), F32), pltpu.VMEM((2, SUPER, D_MODEL), F32),
                            pltpu.SemaphoreType.DMA((2,)), pltpu.SemaphoreType.DMA((1,))],
        ),
        out_shape=jax.ShapeDtypeStruct((a + nb_max * SUB, D_MODEL), F32),
        compiler_params=_cparams(("arbitrary", "arbitrary")),
        name="moe_experts",
    )(route['se'], route['nsub'], route['cstart'], route['ns_used'], route['tok'], route['dest'],
      h2, w_up, w_up, b_up, b_up, w_down, b_down)


TC8 = 128


def _combine_kernel(*refs, n_first):
    y_refs = refs[:TOP_K]
    x1_ref, gate_ref, g_ref, o1_ref, o2_ref = refs[TOP_K:]
    b = pl.program_id(0)
    acc = x1_ref[...]
    gate = gate_ref[...]
    for k in range(TOP_K):
        acc = acc + gate[:, k:k + 1] * y_refs[k][...]
    ms = jnp.mean(acc * acc, axis=-1, keepdims=True)
    res = acc * lax.rsqrt(ms + RMS_EPS) * g_ref[...]

    @pl.when(b < n_first)
    def _to_first():
        o1_ref[...] = res

    @pl.when(b >= n_first)
    def _to_second():
        o2_ref[...] = res


def _combine(y_tok, x1, gate, g_final, m_first):
    m = x1.shape[0]
    n_first = m_first // TC8
    return pl.pallas_call(
        functools.partial(_combine_kernel, n_first=n_first),
        grid=(m // TC8,),
        in_specs=[
            *[pl.BlockSpec((TC8, D_MODEL), lambda b, k=k: (k * (m // TC8) + b, 0)) for k in range(TOP_K)],
            pl.BlockSpec((TC8, D_MODEL), lambda b: (b, 0)),
            pl.BlockSpec((TC8, ROUTER_LANES), lambda b: (b, 0)),
            pl.BlockSpec((1, D_MODEL), lambda b: (0, 0)),
        ],
        out_specs=[pl.BlockSpec((TC8, D_MODEL), lambda b: (jnp.minimum(b, n_first - 1), 0)),
                   pl.BlockSpec((TC8, D_MODEL), lambda b: (jnp.maximum(b - n_first, 0), 0))],
        out_shape=[jax.ShapeDtypeStruct((m_first, D_MODEL), F32), jax.ShapeDtypeStruct((m - m_first, D_MODEL), F32)],
        compiler_params=_cparams(("arbitrary",)),
        name="moe_combine",
    )(*([y_tok] * TOP_K), x1, gate, g_final)


SSM_CHUNK = 8


def _rope_tables(seq, t_new, n_batch, n_dec):
    half = HEAD_DIM // 2
    inv = ROPE_THETA ** (-jnp.arange(half, dtype=F32) / half)
    pos = jnp.concatenate([jnp.tile(jnp.arange(seq), n_batch), jnp.tile(PAST_LEN + jnp.arange(t_new), n_dec)])
    ang = pos.astype(F32)[:, None] * inv[None, :]
    cos = jnp.cos(ang)
    sin = jnp.sin(ang)
    return jnp.tile(cos, (1, 4)), jnp.tile(jnp.concatenate([-sin, sin], axis=1), (1, 2))


def kernel(x_prompt, x_sample, state_ssm_re, state_ssm_im, cache_k, cache_v, attn_norm_g, w_in, b_in, ssm_a_re, ssm_a_im, ssm_log_dt, ssm_b_re, ssm_b_im, ssm_c_re, ssm_c_im, ssm_d, w_glu, b_glu, attn_sinks, w_branch_ssm, w_branch_attn, w_out, ffn_norm_g, w_router, b_router, w_up, b_up, w_down, b_down, final_norm_g):
    n_batch, seq, _ = x_prompt.shape
    n_dec, t_new, _ = x_sample.shape
    mp = n_batch * seq
    ms = n_dec * t_new
    m = mp + ms
    g, p, h = SSM_GROUPS, SSM_STATE, SSM_GROUP
    xp = x_prompt.reshape(mp, D_MODEL)
    xs = x_sample.reshape(ms, D_MODEL)

    cos_t, sin_t = _rope_tables(seq, t_new, n_batch, n_dec)
    proj = _inproj(xp, xs, attn_norm_g[0].reshape(1, D_MODEL), w_in[0].astype(BF16), b_in[0].reshape(1, N_IN),
                   cos_t, sin_t)

    ssm_params = (ssm_a_re[0], ssm_a_im[0], ssm_log_dt[0], ssm_b_re[0], ssm_b_im[0], ssm_c_re[0], ssm_c_im[0], ssm_d[0])
    zeros = jnp.zeros((n_batch, g * p), F32)
    y_ssm, sp_re, sp_im = _ssm(proj, 0, m, n_batch, seq // SSM_CHUNK, SSM_CHUNK,
                               _ssm_operators(*ssm_params, SSM_CHUNK, BF16), zeros, zeros)
    y_ssm, ss_re, ss_im = _ssm(proj, mp, m, n_dec, 1, t_new, _ssm_operators(*ssm_params, t_new, F32),
                               state_ssm_re[0].reshape(n_dec, g * p), state_ssm_im[0].reshape(n_dec, g * p), y_prev=y_ssm)

    sinks = attn_sinks[0].reshape(1, N_HEADS)
    o_attn = _attn_prompt(proj, sinks, n_batch, seq, m)
    o_attn = _attn_sample(proj, mp, cache_k[0].reshape(n_dec * WINDOW, D_KV), cache_v[0].reshape(n_dec * WINDOW, D_KV),
                          sinks, n_dec, t_new, o_attn)

    wr = jnp.pad(w_router[0], ((0, 0), (0, ROUTER_LANES - N_EXPERTS)))
    wr_hi = wr.astype(BF16)
    wr_lo = (wr - wr_hi.astype(F32)).astype(BF16)
    br = jnp.pad(b_router[0], (0, ROUTER_LANES - N_EXPERTS), constant_values=NEG_INF).reshape(1, ROUTER_LANES)
    x1, h2, top_i, gate, rank, counts = _merge(
        y_ssm, o_attn, proj, xp, xs, w_glu[0].astype(BF16), b_glu[0].reshape(1, D_SSM), w_branch_ssm[0].astype(BF16),
        w_branch_attn[0].astype(BF16), w_out[0].astype(BF16), ffn_norm_g[0].reshape(1, D_MODEL), wr_hi, wr_lo, br)

    route = _route(top_i[:, :TOP_K], rank[:, :TOP_K], counts[0, :N_EXPERTS], m)
    y_tok = _moe(h2, route, w_up[0], b_up[0].reshape(N_EXPERTS, 1, 2 * D_FF), w_down[0],
                    b_down[0].reshape(N_EXPERTS, 1, D_MODEL), m)
    y_p, y_s = _combine(y_tok, x1, gate, final_norm_g.reshape(1, D_MODEL), mp)

    y_prompt = y_p.reshape(n_batch, seq, D_MODEL)
    y_sample = y_s.reshape(n_dec, t_new, D_MODEL)
    to_state = lambda s: s.reshape(1, s.shape[0], g, p)
    k_all = proj[:, COL_K:COL_K + D_KV]
    v_all = proj[:, COL_V:COL_V + D_KV]
    kv_p = lambda a: a[:mp].reshape(n_batch, seq, N_KV_HEADS, HEAD_DIM)[:, -WINDOW:][None]
    kv_s = lambda a, c: jnp.concatenate([c[0][:, t_new:], a[mp:].reshape(n_dec, t_new, N_KV_HEADS, HEAD_DIM)], axis=1)[None]
    return (y_prompt, y_sample, to_state(sp_re), to_state(sp_im), kv_p(k_all), kv_p(v_all),
            to_state(ss_re), to_state(ss_im), kv_s(k_all, cache_k), kv_s(v_all, cache_v))
```

```python
import functools

import jax
import jax.numpy as jnp
from jax import lax
from jax.experimental import pallas as pl
from jax.experimental.pallas import tpu as pltpu

F32 = jnp.float32
BF16 = jnp.bfloat16
I32 = jnp.int32
HIGHEST = lax.Precision.HIGHEST

D_MODEL = 2048
RMS_EPS = 1e-5
SSM_GROUP = 16
SSM_GROUPS = 64
SSM_STATE = 64
D_SSM = 1024
HEAD_DIM = 64
N_HEADS = 16
N_KV_HEADS = 4
Q_PER_KV = 4
D_ATTN = 1024
D_KV = 256
WINDOW = 128
ROPE_THETA = 10000.0
NEG_INF = -1e30
PAST_LEN = 16384
N_IN = D_SSM + D_ATTN + 2 * D_KV + 2 * D_MODEL
N_EXPERTS = 32
TOP_K = 4
D_FF = 2048
SWIGLU_ALPHA = 1.702
SWIGLU_LIMIT = 7.0

COL_U, COL_Q, COL_K, COL_V, COL_GS, COL_GA = 0, 1024, 2048, 2304, 2560, 4608

VMEM_LIMIT = 56 * 1024 * 1024


def _sigmoid(x):
    return 0.5 * jnp.tanh(0.5 * x) + 0.5


def _qk(q, k):
    return lax.dot_general(q, k, (((1,), (1,)), ((), ())), preferred_element_type=F32)


def _cparams(sem, vmem=VMEM_LIMIT):
    return pltpu.CompilerParams(dimension_semantics=sem, vmem_limit_bytes=vmem)


TM1 = 512
TN1 = 512
_ROPE_TILE0 = COL_Q // TN1
_KV_TILE = COL_K // TN1
_GATE_TILE0 = COL_GS // TN1


def _two_source(i, n_first, a_ref, b_ref):
    return jnp.where(i < n_first, a_ref[...], b_ref[...])


def _two_source_specs(tile, width, n_first):
    return [pl.BlockSpec((tile, width), lambda i, *_: (jnp.minimum(i, n_first - 1), 0)),
            pl.BlockSpec((tile, width), lambda i, *_: (jnp.maximum(i - n_first, 0), 0))]


def _inproj_kernel(xa_ref, xb_ref, g_ref, w_ref, b_ref, cos_ref, sin_ref, o_ref, h_scr, *, n_first):
    j = pl.program_id(1)

    def norm_from(x_ref):
        x = x_ref[...]
        ms = jnp.mean(x * x, axis=-1, keepdims=True)
        h_scr[...] = (x * lax.rsqrt(ms + RMS_EPS) * g_ref[...]).astype(BF16)

    @pl.when(jnp.logical_and(j == 0, pl.program_id(0) < n_first))
    def _norm_a():
        norm_from(xa_ref)

    @pl.when(jnp.logical_and(j == 0, pl.program_id(0) >= n_first))
    def _norm_b():
        norm_from(xb_ref)

    acc = jnp.dot(h_scr[...], w_ref[...], preferred_element_type=F32) + b_ref[...]

    @pl.when(j >= _GATE_TILE0)
    def _gate():
        o_ref[...] = _sigmoid(acc)

    @pl.when(j < _ROPE_TILE0)
    def _plain():
        o_ref[...] = acc

    @pl.when(jnp.logical_and(j >= _ROPE_TILE0, j <= _KV_TILE))
    def _rope():
        c = jnp.concatenate([cos_ref[...]] * (TN1 // 128), axis=1)
        s = jnp.concatenate([sin_ref[...]] * (TN1 // 128), axis=1)
        lane = lax.broadcasted_iota(I32, acc.shape, 1)
        first_half = (lane & (HEAD_DIM // 2)) == 0
        partner = jnp.where(first_half, pltpu.roll(acc, TN1 - HEAD_DIM // 2, 1), pltpu.roll(acc, HEAD_DIM // 2, 1))
        roped = acc * c + partner * s
        n_rot = jnp.where(j == _KV_TILE, D_KV, TN1)
        o_ref[...] = jnp.where(lane < n_rot, roped, acc)


def _inproj(xa, xb, g, w_bf, b, cos_t, sin_t):
    assert xa.shape[0] % TM1 == 0 and xb.shape[0] % TM1 == 0
    m = xa.shape[0] + xb.shape[0]
    n_first = xa.shape[0] // TM1
    return pl.pallas_call(
        functools.partial(_inproj_kernel, n_first=n_first),
        grid=(m // TM1, N_IN // TN1),
        in_specs=_two_source_specs(TM1, D_MODEL, n_first) + [
            pl.BlockSpec((1, D_MODEL), lambda i, j: (0, 0)),
            pl.BlockSpec((D_MODEL, TN1), lambda i, j: (0, j)),
            pl.BlockSpec((1, TN1), lambda i, j: (0, j)),
            pl.BlockSpec((TM1, 128), lambda i, j: (i, 0)),
            pl.BlockSpec((TM1, 128), lambda i, j: (i, 0)),
        ],
        out_specs=pl.BlockSpec((TM1, TN1), lambda i, j: (i, j)),
        out_shape=jax.ShapeDtypeStruct((m, N_IN), F32),
        scratch_shapes=[pltpu.VMEM((TM1, D_MODEL), BF16)],
        compiler_params=_cparams(("parallel", "arbitrary")),
        name="inproj",
    )(xa, xb, g, w_bf, b, cos_t, sin_t)


SSM_GB = 8


def _ssm_operators(a_re, a_im, log_dt, b_re, b_im, c_re, c_im, d_skip, chunk, w_dtype):
    a_re = a_re.astype(F32)
    a_im = a_im.astype(F32)
    dt = jnp.exp(log_dt.astype(F32))[:, None]
    mag = jnp.exp(a_re * dt)
    lb_re = mag * jnp.cos(a_im * dt)
    lb_im = mag * jnp.sin(a_im * dt)
    inv = 1.0 / (a_re * a_re + a_im * a_im)
    f_re = ((lb_re - 1.0) * a_re + lb_im * a_im) * inv
    f_im = (lb_im * a_re - (lb_re - 1.0) * a_im) * inv
    g, p, h = b_re.shape
    nlb = g // SSM_GB
    sw = SSM_GB * p
    br = jnp.transpose(b_re.astype(F32), (0, 2, 1))
    bi = jnp.transpose(b_im.astype(F32), (0, 2, 1))
    bb_re = f_re[:, None, :] * br - f_im[:, None, :] * bi
    bb_im = f_re[:, None, :] * bi + f_im[:, None, :] * br
    k = jnp.arange(chunk + 1, dtype=F32)[:, None, None]
    pmag = jnp.exp(a_re * dt * k)
    pw_re = (pmag * jnp.cos(a_im * dt * k))[:, :, None, :]
    pw_im = (pmag * jnp.sin(a_im * dt * k))[:, :, None, :]
    pb_re = pw_re[:chunk] * bb_re[None] - pw_im[:chunk] * bb_im[None]
    pb_im = pw_re[:chunk] * bb_im[None] + pw_im[:chunk] * bb_re[None]
    cr = c_re.astype(F32)
    ci = c_im.astype(F32)
    cv_re = cr[None] * pw_re[1:] - ci[None] * pw_im[1:]
    cv_im = -(cr[None] * pw_im[1:] + ci[None] * pw_re[1:])

    def lane_block_rows(x):
        return jnp.transpose(x.reshape(x.shape[0], nlb, SSM_GB * h, x.shape[-1]), (1, 0, 2, 3))

    def same_group(rows_per, cols_per, n_rows, n_cols):
        return (jnp.arange(n_rows)[:, None] // rows_per == jnp.arange(n_cols)[None, :] // cols_per).astype(F32)
    spread = jnp.tile(jnp.eye(p, dtype=F32), (1, SSM_GB))

    def block_diag_states(x, dtype):
        y = jnp.einsum('blrp,pc->blrc', lane_block_rows(x), spread, precision=HIGHEST) * same_group(h, p, 128, sw)
        return y.reshape(nlb, chunk * 128, sw).astype(dtype)
    c_cat = jnp.concatenate([cr.reshape(nlb, 128, p), -ci.reshape(nlb, 128, p)], axis=-1)
    pb_cat = jnp.concatenate([lane_block_rows(pb_re), lane_block_rows(pb_im)], axis=-1)
    lag_blocks = jnp.einsum('blrq,bcq->blrc', pb_cat, c_cat, precision=HIGHEST) * same_group(h, h, 128, 128)
    d_row = jnp.tile(d_skip.astype(F32).reshape(nlb, 1, 128), (1, 1, chunk))
    return dict(lag_blocks=lag_blocks.astype(BF16),
                w_re=block_diag_states(pb_re[::-1], w_dtype), w_im=block_diag_states(pb_im[::-1], w_dtype),
                v_re=block_diag_states(cv_re, BF16), v_im=block_diag_states(cv_im, BF16),
                lbl_re=pw_re[chunk].reshape(1, g * p), lbl_im=pw_im[chunk].reshape(1, g * p), d_row=d_row)


def _ssm_kernel(*refs, chunk, n_chunks, n_seq, aliased):
    (u_ref, t_ref, wr_ref, wi_ref, vr_ref, vi_ref, lr_ref, li_ref, d_ref, s0r_ref, s0i_ref) = refs[:11]
    y_ref, sr_ref, si_ref, z_scr, t_scr, lor_scr, loi_scr, pr_scr, pi_scr = refs[11 + aliased:]
    rows = n_seq * n_chunks
    for j in range(chunk):
        for t in range(chunk):
            blk = t_ref[0, t - j] if t >= j else jnp.zeros((128, 128), BF16)
            t_scr[j * 128:(j + 1) * 128, t * 128:(t + 1) * 128] = blk
    for j in range(chunk):
        z_scr[:, j * 128:(j + 1) * 128] = u_ref[pl.ds(j, rows, stride=chunk), :]
    z = z_scr[...]
    zw = z.astype(wr_ref.dtype)
    prec = HIGHEST if wr_ref.dtype == F32 else None
    lor = jnp.dot(zw, wr_ref[0], preferred_element_type=F32, precision=prec)
    loi = jnp.dot(zw, wi_ref[0], preferred_element_type=F32, precision=prec)
    lr = lr_ref[...]
    li = li_ref[...]
    sr = s0r_ref[...]
    si = s0i_ref[...]
    if n_chunks == 1:
        pr, pi = sr, si
        sr, si = lr * sr - li * si + lor, lr * si + li * sr + loi
    else:
        pieces = range(lor.shape[1] // 128)
        cut = lambda a: tuple(a[:, k * 128:(k + 1) * 128] for k in pieces)
        for k in pieces:
            lor_scr[k] = lor[:, k * 128:(k + 1) * 128]
            loi_scr[k] = loi[:, k * 128:(k + 1) * 128]
        lrs, lis = cut(lr), cut(li)

        def step(c, carry):
            idx = pl.ds(c, n_seq, stride=n_chunks)
            nxt = []
            for k, (a, b) in enumerate(zip(*carry)):
                pr_scr[k, idx, :] = a
                pi_scr[k, idx, :] = b
                nxt.append((lrs[k] * a - lis[k] * b + lor_scr[k, idx, :], lrs[k] * b + lis[k] * a + loi_scr[k, idx, :]))
            return tuple(n[0] for n in nxt), tuple(n[1] for n in nxt)
        srs, sis = lax.fori_loop(0, n_chunks, step, (cut(sr), cut(si)), unroll=8)
        sr, si = jnp.concatenate(srs, axis=1), jnp.concatenate(sis, axis=1)
        pr = jnp.concatenate([pr_scr[k] for k in pieces], axis=1)
        pi = jnp.concatenate([pi_scr[k] for k in pieces], axis=1)
    sr_ref[...] = sr
    si_ref[...] = si
    y = jnp.dot(z.astype(BF16), t_scr[...], preferred_element_type=F32)
    y += _qk(pr.astype(BF16), vr_ref[0])
    y += _qk(pi.astype(BF16), vi_ref[0])
    y += z * d_ref[0]
    for j in range(chunk):
        y_ref[pl.ds(j, rows, stride=chunk), :] = y[:, j * 128:(j + 1) * 128]


def _ssm(proj, row0, m_out, n_seq, n_chunks, chunk, ops, s0_re, s0_im, y_prev=None):
    rows = n_seq * n_chunks
    tok = rows * chunk
    nlb = SSM_GROUPS // SSM_GB
    width = chunk * 128
    sw = SSM_GB * SSM_STATE
    blk0 = row0 // tok
    per_lb = lambda a, b: pl.BlockSpec((1, a, b), lambda i: (i, 0, 0))
    lanes = lambda r, w: pl.BlockSpec((r, w), lambda i: (0, i))
    in_specs = [pl.BlockSpec((tok, 128), lambda i: (blk0, COL_U // 128 + i)),
                pl.BlockSpec((1, chunk, 128, 128), lambda i: (i, 0, 0, 0)),
                per_lb(width, sw), per_lb(width, sw), per_lb(width, sw), per_lb(width, sw),
                lanes(1, sw), lanes(1, sw), per_lb(1, width), lanes(n_seq, sw), lanes(n_seq, sw)]
    args = [proj, ops['lag_blocks'], ops['w_re'], ops['w_im'], ops['v_re'], ops['v_im'],
            ops['lbl_re'], ops['lbl_im'], ops['d_row'], s0_re, s0_im]
    aliases = {}
    if y_prev is not None:
        in_specs.append(pl.BlockSpec(memory_space=pl.ANY))
        args.append(y_prev)
        aliases = {len(args) - 1: 0}
    state = jax.ShapeDtypeStruct((n_seq, SSM_GROUPS * SSM_STATE), F32)
    return pl.pallas_call(
        functools.partial(_ssm_kernel, chunk=chunk, n_chunks=n_chunks, n_seq=n_seq, aliased=int(y_prev is not None)),
        grid=(nlb,),
        in_specs=in_specs,
        out_specs=[pl.BlockSpec((tok, 128), lambda i: (blk0, i)), lanes(n_seq, sw), lanes(n_seq, sw)],
        out_shape=[jax.ShapeDtypeStruct((m_out, D_SSM), F32), state, state],
        scratch_shapes=[pltpu.VMEM((rows, width), F32), pltpu.VMEM((width, width), BF16)]
        + [pltpu.VMEM((sw // 128, rows, 128), F32)] * 4,
        input_output_aliases=aliases,
        compiler_params=_cparams(("parallel",)),
        name=f"ssm_c{n_chunks}",
    )(*args)


def _softmax_sink_pv(parts, sink):
    masked = [jnp.where(valid, s, NEG_INF) for s, valid, _ in parts]
    m = sink
    for s in masked:
        m = jnp.maximum(m, jnp.max(s, axis=-1, keepdims=True))
    denom = jnp.exp(sink - m)
    out = None
    for s, (_, _, v) in zip(masked, parts):
        e = jnp.exp(s - m)
        denom = denom + jnp.sum(e, axis=-1, keepdims=True)
        pv = jnp.dot(e.astype(BF16), v, preferred_element_type=F32)
        out = pv if out is None else out + pv
    return out / denom


ATTN_STACK = 4


def _attn_prompt_kernel(q_ref, kp_ref, kc_ref, vp_ref, vc_ref, sink_ref, o_ref):
    qb = pl.program_id(1)
    rows = ATTN_STACK * WINDOW
    t = lax.broadcasted_iota(I32, (rows, 2 * WINDOW), 0) & (WINDOW - 1)
    col = lax.broadcasted_iota(I32, (rows, 2 * WINDOW), 1)
    valid = jnp.logical_or(jnp.logical_and(jnp.logical_and(col < WINDOW, col > t), qb > 0),
                           jnp.logical_and(col >= WINDOW, (col - WINDOW) <= t))
    scale = HEAD_DIM ** -0.5
    for kv in range(N_KV_HEADS):
        lanes = slice(kv * HEAD_DIM, (kv + 1) * HEAD_DIM)
        k = jnp.concatenate([kp_ref[:, lanes], kc_ref[:, lanes]], axis=0).astype(BF16)
        v = jnp.concatenate([vp_ref[:, lanes], vc_ref[:, lanes]], axis=0).astype(BF16)
        for h0 in range(kv * Q_PER_KV, (kv + 1) * Q_PER_KV, ATTN_STACK):
            heads = range(h0, h0 + ATTN_STACK)
            q = jnp.concatenate([q_ref[:, h * HEAD_DIM:(h + 1) * HEAD_DIM] for h in heads], axis=0).astype(BF16)
            sink = jnp.concatenate(
                [jnp.broadcast_to(sink_ref[0:1, h:h + 1], (WINDOW, 1)) for h in heads], axis=0)
            s = _qk(q, k) * scale
            o = _softmax_sink_pv([(s, valid, v)], sink)
            for i, h in enumerate(heads):
                o_ref[:, h * HEAD_DIM:(h + 1) * HEAD_DIM] = o[i * WINDOW:(i + 1) * WINDOW, :]


def _attn_prompt(proj, sinks, n_batch, seq, m_out):
    nb = seq // WINDOW
    cq, ck, cv = COL_Q // D_ATTN, COL_K // D_KV, COL_V // D_KV
    cur = lambda col: (lambda n, b: (n * nb + b, col))
    prev = lambda col: (lambda n, b: (n * nb + jnp.maximum(b - 1, 0), col))
    return pl.pallas_call(
        _attn_prompt_kernel,
        grid=(n_batch, nb),
        in_specs=[
            pl.BlockSpec((WINDOW, D_ATTN), cur(cq)),
            pl.BlockSpec((WINDOW, D_KV), prev(ck)),
            pl.BlockSpec((WINDOW, D_KV), cur(ck)),
            pl.BlockSpec((WINDOW, D_KV), prev(cv)),
            pl.BlockSpec((WINDOW, D_KV), cur(cv)),
            pl.BlockSpec((1, N_HEADS), lambda n, b: (0, 0)),
        ],
        out_specs=pl.BlockSpec((WINDOW, D_ATTN), lambda n, b: (n * nb + b, 0)),
        out_shape=jax.ShapeDtypeStruct((m_out, D_ATTN), F32),
        compiler_params=_cparams(("parallel", "arbitrary")),
        name="attn_prompt",
    )(proj, proj, proj, proj, proj, sinks)


SAMPLE_GB = 8


def _attn_sample_kernel(q_ref, kn_ref, vn_ref, kc_ref, vc_ref, sink_ref, o_ref, *, t_new):
    tok = SAMPLE_GB * t_new
    rows = Q_PER_KV * tok
    ncache = SAMPLE_GB * WINDOW
    r_c = lax.broadcasted_iota(I32, (rows, ncache), 0)
    c_c = lax.broadcasted_iota(I32, (rows, ncache), 1)
    rb_c = (r_c % tok) // t_new
    rt_c = r_c % t_new
    valid_c = jnp.logical_and(rb_c == c_c // WINDOW, (c_c % WINDOW) > rt_c)
    r_n = lax.broadcasted_iota(I32, (rows, tok), 0)
    c_n = lax.broadcasted_iota(I32, (rows, tok), 1)
    valid_n = jnp.logical_and((r_n % tok) // t_new == c_n // t_new, (c_n % t_new) <= (r_n % t_new))
    scale = HEAD_DIM ** -0.5
    for kv in range(N_KV_HEADS):
        lanes = slice(kv * HEAD_DIM, (kv + 1) * HEAD_DIM)
        kc = kc_ref[:, lanes].astype(BF16)
        vc = vc_ref[:, lanes].astype(BF16)
        kn = kn_ref[:, lanes].astype(BF16)
        vn = vn_ref[:, lanes].astype(BF16)
        heads = [kv * Q_PER_KV + i for i in range(Q_PER_KV)]
        q = jnp.concatenate([q_ref[:, h * HEAD_DIM:(h + 1) * HEAD_DIM] for h in heads], axis=0).astype(BF16)
        sink = jnp.concatenate(
            [jnp.broadcast_to(sink_ref[0:1, h:h + 1], (tok, 1)) for h in heads], axis=0)
        o = _softmax_sink_pv([(_qk(q, kc) * scale, valid_c, vc), (_qk(q, kn) * scale, valid_n, vn)], sink)
        for i, h in enumerate(heads):
            o_ref[:, h * HEAD_DIM:(h + 1) * HEAD_DIM] = o[i * tok:(i + 1) * tok, :]


def _attn_sample_aliased(q_ref, kn_ref, vn_ref, kc_ref, vc_ref, sink_ref, o_prev, o_ref, *, t_new):
    del o_prev
    _attn_sample_kernel(q_ref, kn_ref, vn_ref, kc_ref, vc_ref, sink_ref, o_ref, t_new=t_new)


def _attn_sample(proj, row0, cache_k, cache_v, sinks, n_batch, t_new, o_prev):
    tok = SAMPLE_GB * t_new
    blk0 = row0 // tok
    cq, ck, cv = COL_Q // D_ATTN, COL_K // D_KV, COL_V // D_KV
    new = lambda col: (lambda g: (blk0 + g, col))
    return pl.pallas_call(
        functools.partial(_attn_sample_aliased, t_new=t_new),
        grid=(n_batch // SAMPLE_GB,),
        in_specs=[
            pl.BlockSpec((tok, D_ATTN), new(cq)),
            pl.BlockSpec((tok, D_KV), new(ck)),
            pl.BlockSpec((tok, D_KV), new(cv)),
            pl.BlockSpec((SAMPLE_GB * WINDOW, D_KV), lambda g: (g, 0)),
            pl.BlockSpec((SAMPLE_GB * WINDOW, D_KV), lambda g: (g, 0)),
            pl.BlockSpec((1, N_HEADS), lambda g: (0, 0)),
            pl.BlockSpec(memory_space=pl.ANY),
        ],
        out_specs=pl.BlockSpec((tok, D_ATTN), lambda g: (blk0 + g, 0)),
        out_shape=jax.ShapeDtypeStruct(o_prev.shape, F32),
        input_output_aliases={6: 0},
        compiler_params=_cparams(("parallel",)),
        name="attn_sample",
    )(proj, proj, proj, cache_k, cache_v, sinks, o_prev)


TM4 = 256
ROUTER_LANES = 128
GATE_W = 512
GATE_BLOCKS = D_MODEL // GATE_W
ROW_TILES = D_MODEL // 128


def _gelu_tanh(x):
    return 0.5 * x * (1.0 + jnp.tanh(0.7978845608028654 * (x + 0.044715 * (x * x * x))))


def _merge_kernel(y_ref, o_ref, *refs, n_first):
    gs_refs, ga_refs = refs[:GATE_BLOCKS], refs[GATE_BLOCKS:2 * GATE_BLOCKS]
    (xa_ref, xb_ref, wglu_ref, bglu_ref, wbs_ref, wba_ref, wout_ref, g2_ref, wr_ref, wrl_ref, br_ref,
     x1_ref, h2_ref, ti_ref, tg_ref, rank_ref, cnt_ref, seen_scr) = refs[2 * GATE_BLOCKS:]

    @pl.when(pl.program_id(0) == 0)
    def _reset():
        seen_scr[...] = jnp.zeros(seen_scr.shape, F32)

    y1 = _gelu_tanh(y_ref[...])
    z = jnp.dot(y1.astype(BF16), wglu_ref[...], preferred_element_type=F32) + bglu_ref[...]
    y2 = y1 * _sigmoid(z)
    bs = jnp.dot(y2.astype(BF16), wbs_ref[...], preferred_element_type=F32)
    ba = jnp.dot(o_ref[...].astype(BF16), wba_ref[...], preferred_element_type=F32)
    merged = jnp.concatenate(
        [gs[...] * bs[:, c * GATE_W:(c + 1) * GATE_W] + ga[...] * ba[:, c * GATE_W:(c + 1) * GATE_W]
         for c, (gs, ga) in enumerate(zip(gs_refs, ga_refs))], axis=1)
    x = _two_source(pl.program_id(0), n_first, xa_ref, xb_ref)
    x1 = x + jnp.dot(merged.astype(BF16), wout_ref[...], preferred_element_type=F32)
    x1_ref[...] = x1
    ms = jnp.mean(x1 * x1, axis=-1, keepdims=True)
    h2 = x1 * lax.rsqrt(ms + RMS_EPS) * g2_ref[...]
    for s in range(ROW_TILES):
        h2_ref[pl.ds(s, TM4, stride=ROW_TILES), :] = h2[:, s * 128:(s + 1) * 128]
    h_hi = h2.astype(BF16)
    h_lo = (h2 - h_hi.astype(F32)).astype(BF16)
    logits = (jnp.dot(h_hi, wr_ref[...], preferred_element_type=F32)
              + (jnp.dot(h_hi, wrl_ref[...], preferred_element_type=F32)
                 + jnp.dot(h_lo, wr_ref[...], preferred_element_type=F32))) + br_ref[...]
    lane = lax.broadcasted_iota(I32, logits.shape, 1)
    cur = logits
    top_i = jnp.zeros(logits.shape, I32)
    top_v = jnp.zeros(logits.shape, F32)
    chosen = []
    for k in range(TOP_K):
        mx = jnp.max(cur, axis=-1, keepdims=True)
        idx = jnp.min(jnp.where(cur == mx, lane, ROUTER_LANES), axis=-1, keepdims=True)
        top_i = jnp.where(lane == k, idx, top_i)
        top_v = jnp.where(lane == k, mx, top_v)
        chosen.append(lane == idx)
        cur = jnp.where(chosen[-1], -jnp.inf, cur)
    v0 = jnp.max(jnp.where(lane < TOP_K, top_v, -jnp.inf), axis=-1, keepdims=True)
    e = jnp.where(lane < TOP_K, jnp.exp(top_v - v0), 0.0)
    ti_ref[...] = top_i
    tg_ref[...] = e / jnp.sum(e, axis=-1, keepdims=True)
    hits = sum(jnp.where(c, 1.0, 0.0) for c in chosen)
    r = lax.broadcasted_iota(I32, (TM4, TM4), 0)
    c = lax.broadcasted_iota(I32, (TM4, TM4), 1)
    before = jnp.dot(jnp.where(c < r, 1.0, 0.0).astype(BF16), hits.astype(BF16), preferred_element_type=F32)
    before = before + seen_scr[...]
    rank = jnp.zeros(logits.shape, F32)
    for k in range(TOP_K):
        rank = jnp.where(lane == k, jnp.sum(jnp.where(chosen[k], before, 0.0), axis=-1, keepdims=True), rank)
    rank_ref[...] = rank.astype(I32)
    seen_scr[...] = seen_scr[...] + jnp.sum(hits, axis=0, keepdims=True)
    cnt_ref[...] = seen_scr[...].astype(I32)


def _merge(y_ssm, o_attn, proj, xa, xb, wglu, bglu, wbs, wba, wout, g2, wr_hi, wr_lo, br):
    assert xa.shape[0] % TM4 == 0 and xb.shape[0] % TM4 == 0
    m = xa.shape[0] + xb.shape[0]
    n_first = xa.shape[0] // TM4
    row = lambda w: pl.BlockSpec((TM4, w), lambda i: (i, 0))
    const = lambda a, b: pl.BlockSpec((a, b), lambda i: (0, 0), pipeline_mode=pl.Buffered(1))
    return pl.pallas_call(
        functools.partial(_merge_kernel, n_first=n_first),
        grid=(m // TM4,),
        in_specs=[
            row(D_SSM), row(D_ATTN),
            *[pl.BlockSpec((TM4, GATE_W), lambda i, c=col // GATE_W + k: (i, c))
              for col in (COL_GS, COL_GA) for k in range(GATE_BLOCKS)],
            *_two_source_specs(TM4, D_MODEL, n_first),
            const(D_SSM, D_SSM), const(1, D_SSM), const(D_SSM, D_MODEL), const(D_ATTN, D_MODEL),
            const(D_MODEL, D_MODEL), const(1, D_MODEL), const(D_MODEL, ROUTER_LANES), const(D_MODEL, ROUTER_LANES),
            const(1, ROUTER_LANES),
        ],
        out_specs=[row(D_MODEL), pl.BlockSpec((TM4 * ROW_TILES, 128), lambda i: (i, 0)),
                   row(ROUTER_LANES), row(ROUTER_LANES), row(ROUTER_LANES),
                   pl.BlockSpec((1, ROUTER_LANES), lambda i: (0, 0))],
        out_shape=[jax.ShapeDtypeStruct((m, D_MODEL), F32), jax.ShapeDtypeStruct((m * ROW_TILES, 128), F32),
                   jax.ShapeDtypeStruct((m, ROUTER_LANES), I32), jax.ShapeDtypeStruct((m, ROUTER_LANES), F32),
                   jax.ShapeDtypeStruct((m, ROUTER_LANES), I32), jax.ShapeDtypeStruct((1, ROUTER_LANES), I32)],
        scratch_shapes=[pltpu.VMEM((1, ROUTER_LANES), F32)],
        compiler_params=_cparams(("arbitrary",)),
        name="merge",
    )(y_ssm, o_attn, *([proj] * (2 * GATE_BLOCKS)), xa, xb, wglu, bglu, wbs, wba, wout, g2, wr_hi, wr_lo, br)


SUB = 128
SUPER = 1280
TF = 256
NF = D_FF // TF
MOE_WIDE = 8
MOE_WHOLE = (SUPER // SUB - 1, SUPER // SUB)


def _moe_sizes(m):
    a = m * TOP_K
    ns_max = N_EXPERTS + a // SUPER
    nb_max = N_EXPERTS + a // SUB
    return a, ns_max, nb_max


def _route(top_i, rank, counts, m):
    a, ns_max, nb_max = _moe_sizes(m)
    experts = jnp.arange(N_EXPERTS, dtype=I32)
    nsb = (counts + SUPER - 1) // SUPER
    sb_end = jnp.cumsum(nsb)
    sb_start = sb_end - nsb
    nb = (counts + SUB - 1) // SUB
    b_end = jnp.cumsum(nb)
    n_c = nb_max * SUB
    cslot = jnp.sum(jnp.where(top_i[:, :, None] == experts, (b_end - nb) * SUB, 0), axis=-1) + rank
    row_of = jnp.arange(TOP_K, dtype=I32)[None, :] * m + jnp.arange(m, dtype=I32)[:, None]
    dest = (a + jnp.arange(n_c, dtype=I32)).at[cslot.reshape(a)].set(row_of.reshape(a), unique_indices=True)
    tok = jnp.minimum(dest, a - 1) % m
    s_idx = jnp.arange(ns_max, dtype=I32)
    ns_used = sb_end[-1]
    s_eff = jnp.maximum(jnp.minimum(s_idx, ns_used - 1), 0)
    se = jnp.minimum(jnp.sum((s_eff[:, None] >= sb_end[None, :]).astype(I32), axis=1), N_EXPERTS - 1)
    of_se = lambda v: jnp.sum(jnp.where(se[:, None] == experts, v, 0), axis=-1)
    rows_in = jnp.clip(of_se(counts) - (s_eff - of_se(sb_start)) * SUPER, 0, SUPER)
    nsub = jnp.where(s_idx < ns_used, (rows_in + SUB - 1) // SUB, 0).astype(I32)
    cstart = (of_se(b_end - nb) * SUB + (s_eff - of_se(sb_start)) * SUPER).astype(I32)
    return dict(se=se.astype(I32), nsub=nsub, cstart=cstart, ns_used=ns_used.reshape(1).astype(I32),
                tok=tok.astype(I32), dest=dest)


_COPY_UNROLL = 16


def _moe_kernel(se_ref, nsub_ref, cstart_ref, nused_ref, tok_ref, dest_ref,
                h_hbm, wg_ref, wl_ref, bg_ref, bl_ref, wd_ref, bd_ref, y_hbm,
                wg_s, wl_s, wd_s, x_ref, xstage, acc, sem, sem_x):
    s = pl.program_id(0)
    f = pl.program_id(1)
    n = nsub_ref[s]

    def token_rows(sb, wait):
        base = cstart_ref[sb]

        def body(g, c):
            for u in range(_COPY_UNROLL):
                r = g * _COPY_UNROLL + u
                tok = 0 if wait else tok_ref[base + r]
                cp = pltpu.make_async_copy(
                    h_hbm.at[pl.ds(pl.multiple_of(tok * ROW_TILES, ROW_TILES), ROW_TILES), :],
                    xstage.at[pl.ds(pl.multiple_of(r * ROW_TILES, ROW_TILES), ROW_TILES), :], sem_x.at[0])
                cp.wait() if wait else cp.start()
            return c
        lax.fori_loop(0, nsub_ref[sb] * (SUB // _COPY_UNROLL), body, 0)

    def result_rows(sb, wait):
        sl = sb % 2
        base = cstart_ref[sb]

        def body(g, c):
            for u in range(_COPY_UNROLL):
                r = pl.multiple_of(g * _COPY_UNROLL, _COPY_UNROLL) + u
                row = 0 if wait else dest_ref[base + r]
                cp = pltpu.make_async_copy(acc.at[sl, pl.ds(r, 1), :], y_hbm.at[pl.ds(row, 1), :], sem.at[sl])
                cp.wait() if wait else cp.start(priority=u % 2)
            return c
        lax.fori_loop(0, nsub_ref[sb] * (SUB // _COPY_UNROLL), body, 0)

    @pl.when(n > 0)
    def _work():
        o_ref = acc.at[s % 2]
        wg_s[...] = wg_ref[0].astype(BF16)
        wl_s[...] = wl_ref[0].astype(BF16)
        wd_s[...] = wd_ref[0].astype(BF16)

        @pl.when(f == 0)
        def _init():
            @pl.when(s == 0)
            def _first_rows():
                token_rows(0, False)
            token_rows(s, True)

            def to_bf16(i, c):
                for t in range(ROW_TILES):
                    src = pl.ds(i * (SUB * ROW_TILES) + t, SUB, stride=ROW_TILES)
                    x_ref[pl.ds(pl.multiple_of(i * SUB, SUB), SUB), t * 128:(t + 1) * 128] = xstage[src, :].astype(BF16)
                return c
            lax.fori_loop(0, n, to_bf16, 0)

            @pl.when(s + 1 < nused_ref[0])
            def _next_rows():
                token_rows(s + 1, False)

            @pl.when(s >= 2)
            def _free_slot():
                result_rows(s - 2, True)
            o_ref[...] = jnp.broadcast_to(bd_ref[0], o_ref.shape)

        bg = bg_ref[0]
        bl = bl_ref[0]

        def expert_rows(r0, rows):
            xs = x_ref[pl.ds(r0, rows), :]
            hg = jnp.dot(xs, wg_s[...], preferred_element_type=F32) + bg
            hl = jnp.dot(xs, wl_s[...], preferred_element_type=F32) + bl
            hg = jnp.minimum(hg, SWIGLU_LIMIT)
            hl = jnp.clip(hl, -SWIGLU_LIMIT, SWIGLU_LIMIT)
            act = hg * _sigmoid(SWIGLU_ALPHA * hg) * (hl + 1.0)
            o_ref[pl.ds(r0, rows), :] += jnp.dot(act.astype(BF16), wd_s[...], preferred_element_type=F32)

        for whole in MOE_WHOLE:
            @pl.when(n == whole)
            def _whole(whole=whole):
                expert_rows(0, whole * SUB)

        @pl.when(functools.reduce(jnp.logical_and, [n != whole for whole in MOE_WHOLE]))
        def _general():
            def wide_body(i, c):
                expert_rows(pl.multiple_of(i * (MOE_WIDE * SUB), MOE_WIDE * SUB), MOE_WIDE * SUB)
                return c
            n_wide = n // MOE_WIDE
            lax.fori_loop(0, n_wide, wide_body, 0)

            def single_body(i, c):
                expert_rows(pl.multiple_of(i * SUB, SUB), SUB)
                return c
            lax.fori_loop(n_wide * MOE_WIDE, n, single_body, 0)

        @pl.when(f == NF - 1)
        def _send():
            result_rows(s, False)

    @pl.when(jnp.logical_and(s == pl.num_programs(0) - 1, f == NF - 1))
    def _drain():
        n_used = nused_ref[0]

        @pl.when(n_used >= 2)
        def _():
            result_rows(n_used - 2, True)
        result_rows(n_used - 1, True)


def _moe(h2, route, w_up, b_up, w_down, b_down, m):
    a, ns_max, nb_max = _moe_sizes(m)
    sp = lambda shape, imap: pl.BlockSpec(shape, lambda s, f, se, *_: imap(s, f, se))
    return pl.pallas_call(
        _moe_kernel,
        grid_spec=pltpu.PrefetchScalarGridSpec(
            num_scalar_prefetch=6,
            grid=(ns_max, NF),
            in_specs=[
                pl.BlockSpec(memory_space=pl.ANY),
                sp((1, D_MODEL, TF), lambda s, f, se: (se[s], 0, f)),
                sp((1, D_MODEL, TF), lambda s, f, se: (se[s], 0, NF + f)),
                sp((1, 1, TF), lambda s, f, se: (se[s], 0, f)),
                sp((1, 1, TF), lambda s, f, se: (se[s], 0, NF + f)),
                sp((1, TF, D_MODEL), lambda s, f, se: (se[s], f, 0)),
                sp((1, 1, D_MODEL), lambda s, f, se: (se[s], 0, 0)),
            ],
            out_specs=pl.BlockSpec(memory_space=pl.ANY),
            scratch_shapes=[pltpu.VMEM((D_MODEL, TF), BF16), pltpu.VMEM((D_MODEL, TF), BF16),
                            pltpu.VMEM((TF, D_MODEL), BF16), pltpu.VMEM((SUPER, D_MODEL), BF16),
                            pltpu.VMEM((SUPER * ROW_TILES, 128), F32), pltpu.VMEM((2, SUPER, D_MODEL), F32),
                            pltpu.SemaphoreType.DMA((2,)), pltpu.SemaphoreType.DMA((1,))],
        ),
        out_shape=jax.ShapeDtypeStruct((a + nb_max * SUB, D_MODEL), F32),
        compiler_params=_cparams(("arbitrary", "arbitrary")),
        name="moe_experts",
    )(route['se'], route['nsub'], route['cstart'], route['ns_used'], route['tok'], route['dest'],
      h2, w_up, w_up, b_up, b_up, w_down, b_down)


TC8 = 128


def _combine_kernel(*refs, n_first):
    y_refs = refs[:TOP_K]
    x1_ref, gate_ref, g_ref, o1_ref, o2_ref = refs[TOP_K:]
    b = pl.program_id(0)
    acc = x1_ref[...]
    gate = gate_ref[...]
    for k in range(TOP_K):
        acc = acc + gate[:, k:k + 1] * y_refs[k][...]
    ms = jnp.mean(acc * acc, axis=-1, keepdims=True)
    res = acc * lax.rsqrt(ms + RMS_EPS) * g_ref[...]

    @pl.when(b < n_first)
    def _to_first():
        o1_ref[...] = res

    @pl.when(b >= n_first)
    def _to_second():
        o2_ref[...] = res


def _combine(y_tok, x1, gate, g_final, m_first):
    m = x1.shape[0]
    n_first = m_first // TC8
    return pl.pallas_call(
        functools.partial(_combine_kernel, n_first=n_first),
        grid=(m // TC8,),
        in_specs=[
            *[pl.BlockSpec((TC8, D_MODEL), lambda b, k=k: (k * (m // TC8) + b, 0)) for k in range(TOP_K)],
            pl.BlockSpec((TC8, D_MODEL), lambda b: (b, 0)),
            pl.BlockSpec((TC8, ROUTER_LANES), lambda b: (b, 0)),
            pl.BlockSpec((1, D_MODEL), lambda b: (0, 0)),
        ],
        out_specs=[pl.BlockSpec((TC8, D_MODEL), lambda b: (jnp.minimum(b, n_first - 1), 0)),
                   pl.BlockSpec((TC8, D_MODEL), lambda b: (jnp.maximum(b - n_first, 0), 0))],
        out_shape=[jax.ShapeDtypeStruct((m_first, D_MODEL), F32), jax.ShapeDtypeStruct((m - m_first, D_MODEL), F32)],
        compiler_params=_cparams(("arbitrary",)),
        name="moe_combine",
    )(*([y_tok] * TOP_K), x1, gate, g_final)


SSM_CHUNK = 8


def _rope_tables(seq, t_new, n_batch, n_dec):
    half = HEAD_DIM // 2
    inv = ROPE_THETA ** (-jnp.arange(half, dtype=F32) / half)
    pos = jnp.concatenate([jnp.tile(jnp.arange(seq), n_batch), jnp.tile(PAST_LEN + jnp.arange(t_new), n_dec)])
    ang = pos.astype(F32)[:, None] * inv[None, :]
    cos = jnp.cos(ang)
    sin = jnp.sin(ang)
    return jnp.tile(cos, (1, 4)), jnp.tile(jnp.concatenate([-sin, sin], axis=1), (1, 2))


def kernel(x_prompt, x_sample, state_ssm_re, state_ssm_im, cache_k, cache_v, attn_norm_g, w_in, b_in, ssm_a_re, ssm_a_im, ssm_log_dt, ssm_b_re, ssm_b_im, ssm_c_re, ssm_c_im, ssm_d, w_glu, b_glu, attn_sinks, w_branch_ssm, w_branch_attn, w_out, ffn_norm_g, w_router, b_router, w_up, b_up, w_down, b_down, final_norm_g):
    n_batch, seq, _ = x_prompt.shape
    n_dec, t_new, _ = x_sample.shape
    mp = n_batch * seq
    ms = n_dec * t_new
    m = mp + ms
    g, p, h = SSM_GROUPS, SSM_STATE, SSM_GROUP
    xp = x_prompt.reshape(mp, D_MODEL)
    xs = x_sample.reshape(ms, D_MODEL)

    cos_t, sin_t = _rope_tables(seq, t_new, n_batch, n_dec)
    proj = _inproj(xp, xs, attn_norm_g[0].reshape(1, D_MODEL), w_in[0].astype(BF16), b_in[0].reshape(1, N_IN),
                   cos_t, sin_t)

    ssm_params = (ssm_a_re[0], ssm_a_im[0], ssm_log_dt[0], ssm_b_re[0], ssm_b_im[0], ssm_c_re[0], ssm_c_im[0], ssm_d[0])
    zeros = jnp.zeros((n_batch, g * p), F32)
    y_ssm, sp_re, sp_im = _ssm(proj, 0, m, n_batch, seq // SSM_CHUNK, SSM_CHUNK,
                               _ssm_operators(*ssm_params, SSM_CHUNK, BF16), zeros, zeros)
    y_ssm, ss_re, ss_im = _ssm(proj, mp, m, n_dec, 1, t_new, _ssm_operators(*ssm_params, t_new, F32),
                               state_ssm_re[0].reshape(n_dec, g * p), state_ssm_im[0].reshape(n_dec, g * p), y_prev=y_ssm)

    sinks = attn_sinks[0].reshape(1, N_HEADS)
    o_attn = _attn_prompt(proj, sinks, n_batch, seq, m)
    o_attn = _attn_sample(proj, mp, cache_k[0].reshape(n_dec * WINDOW, D_KV), cache_v[0].reshape(n_dec * WINDOW, D_KV),
                          sinks, n_dec, t_new, o_attn)

    wr = jnp.pad(w_router[0], ((0, 0), (0, ROUTER_LANES - N_EXPERTS)))
    wr_hi = wr.astype(BF16)
    wr_lo = (wr - wr_hi.astype(F32)).astype(BF16)
    br = jnp.pad(b_router[0], (0, ROUTER_LANES - N_EXPERTS), constant_values=NEG_INF).reshape(1, ROUTER_LANES)
    x1, h2, top_i, gate, rank, counts = _merge(
        y_ssm, o_attn, proj, xp, xs, w_glu[0].astype(BF16), b_glu[0].reshape(1, D_SSM), w_branch_ssm[0].astype(BF16),
        w_branch_attn[0].astype(BF16), w_out[0].astype(BF16), ffn_norm_g[0].reshape(1, D_MODEL), wr_hi, wr_lo, br)

    route = _route(top_i[:, :TOP_K], rank[:, :TOP_K], counts[0, :N_EXPERTS], m)
    y_tok = _moe(h2, route, w_up[0], b_up[0].reshape(N_EXPERTS, 1, 2 * D_FF), w_down[0],
                    b_down[0].reshape(N_EXPERTS, 1, D_MODEL), m)
    y_p, y_s = _combine(y_tok, x1, gate, final_norm_g.reshape(1, D_MODEL), mp)

    y_prompt = y_p.reshape(n_batch, seq, D_MODEL)
    y_sample = y_s.reshape(n_dec, t_new, D_MODEL)
    to_state = lambda s: s.reshape(1, s.shape[0], g, p)
    k_all = proj[:, COL_K:COL_K + D_KV]
    v_all = proj[:, COL_V:COL_V + D_KV]
    kv_p = lambda a: a[:mp].reshape(n_batch, seq, N_KV_HEADS, HEAD_DIM)[:, -WINDOW:][None]
    kv_s = lambda a, c: jnp.concatenate([c[0][:, t_new:], a[mp:].reshape(n_dec, t_new, N_KV_HEADS, HEAD_DIM)], axis=1)[None]
    return (y_prompt, y_sample, to_state(sp_re), to_state(sp_im), kv_p(k_all), kv_p(v_all),
            to_state(ss_re), to_state(ss_im), kv_s(k_all, cache_k), kv_s(v_all, cache_v))
```
